```python
import math
import jax
import jax.numpy as jnp
from jax import lax
import numpy as np

D_MODEL = 1024
BATCH = 8
SEQ = 2048
DEPTH = 2

CTX_LEN = 256
GRID_W = 64
D_MIX = D_MODEL
EPS = 1e-6
F32 = jnp.float32

LRU_WIDTH = D_MIX // 4
LRU_HEADS = 4
LRU_HD = LRU_WIDTH // LRU_HEADS
CONV_W = 4
RG_C = 8.0
DA_WIDTH = D_MIX // 4
DA_HEADS = 4
DA_VD = DA_WIDTH // DA_HEADS
DA_QD = DA_VD // 2
ROPE_THETA = 10000.0
Q_BLOCK = 128
SG_WIDTH = D_MIX // 4
SG_HEADS = 4
SG_HD = SG_WIDTH // SG_HEADS
SG_CHUNK = 128
HG_WIDTH = D_MIX // 4
HG_HEADS = 4
HG_HD = HG_WIDTH // HG_HEADS
HG_CHUNK = 64
N_EXPERTS = 16
N_GROUPS = 4
EXP_PER_GROUP = N_EXPERTS // N_GROUPS
TOP_K = 2
D_EXPERT = 512

IN_COLS = 2 * LRU_WIDTH + 3 * DA_WIDTH + 2 * SG_WIDTH + 5 * HG_WIDTH
SPLITS = (2 * LRU_WIDTH, 2 * LRU_WIDTH + 3 * DA_WIDTH, 2 * LRU_WIDTH + 3 * DA_WIDTH + 2 * SG_WIDTH)

kernel_name = 'hybrid_parallel_heads_diffusion_block'


def rms_norm(x, g):
    xf = x.astype(F32)
    y = xf * lax.rsqrt(jnp.mean(xf * xf, axis=-1, keepdims=True) + EPS)
    return (y * g.astype(F32)).astype(x.dtype)


def modulate(h, shift, scale):
    return h * (1.0 + scale) + shift


def tflip(t, rev):
    return jnp.flip(t, axis=1) if rev else t


def centred_dwconv(x, w, b):
    T = x.shape[1]
    left = CONV_W // 2
    xp = jnp.pad(x, ((0, 0), (left, CONV_W - 1 - left), (0, 0)))
    return sum(xp[:, j:j + T] * w[j] for j in range(CONV_W)) + b


def block_diag(x, w, b):
    B_, T, _ = x.shape
    xh = x.reshape(B_, T, LRU_HEADS, LRU_HD)
    return jnp.einsum('bthi,hij->bthj', xh, w).reshape(B_, T, LRU_WIDTH) + b


def rglru_coeffs(u, w_r, b_r, w_i, b_i, lam):
    r = jax.nn.sigmoid(block_diag(u, w_r, b_r).astype(F32))
    i = jax.nn.sigmoid(block_diag(u, w_i, b_i).astype(F32))
    log_a = -RG_C * r * jax.nn.softplus(-lam.astype(F32))
    b = jnp.sqrt(-jnp.expm1(2.0 * log_a)) * i * u.astype(F32)
    return log_a, b


def _lin_combine(e1, e2):
    a1, b1 = e1
    a2, b2 = e2
    return a1 * a2, a2 * b1 + b2


def lru_scan(log_a, b, h0):
    A, Bc = lax.associative_scan(_lin_combine, (jnp.exp(log_a), b), axis=1)
    return A * h0[:, None] + Bc


def lru_final(log_a, b):
    c = jnp.cumsum(log_a, axis=1)
    return jnp.sum(jnp.exp(c[:, -1:] - c) * b, axis=1)


def rglru_mixer(z, zc, conv_w, conv_b, w_r, b_r, w_i, b_i, lam, ctx_out):
    xa, ga = jnp.split(z, 2, axis=-1)
    xca, gca = jnp.split(zc, 2, axis=-1)
    u = centred_dwconv(xa, conv_w, conv_b)
    uc = centred_dwconv(xca, conv_w, conv_b)
    h_sum, hc_sum = 0.0, 0.0
    for d in range(2):
        rev = d == 1
        la_c, b_c = rglru_coeffs(tflip(uc, rev), w_r[d], b_r[d], w_i[d], b_i[d], lam[d])
        if ctx_out:
            hc = lru_scan(la_c, b_c, jnp.zeros_like(b_c[:, 0]))
            hc_last = hc[:, -1]
            hc_sum = hc_sum + tflip(hc, rev)
        else:
            hc_last = lru_final(la_c, b_c)
        la, b = rglru_coeffs(tflip(u, rev), w_r[d], b_r[d], w_i[d], b_i[d], lam[d])
        h_sum = h_sum + tflip(lru_scan(la, b, hc_last), rev)
    y = (h_sum * jax.nn.gelu(ga.astype(F32))).astype(z.dtype)
    yc = (hc_sum * jax.nn.gelu(gca.astype(F32))).astype(z.dtype) if ctx_out else None
    return y, yc


def rope_2d(t, cos, sin):
    sh = t.shape
    tr = t.reshape(sh[:-1] + (2, 2, DA_QD // 4))
    x1, x2 = tr[..., 0, :], tr[..., 1, :]
    c = cos[None, :, None, None]
    s = sin[None, :, None, None]
    out = jnp.stack([x1 * c - x2 * s, x2 * c + x1 * s], axis=-2)
    return out.reshape(sh)


def diff_attend(q, k, v, lam):
    s = jnp.einsum('bqhnd,bkhnd->bnhqk', q, k).astype(F32) * (DA_QD ** -0.5)
    p = jax.nn.softmax(s, axis=-1)
    pd = p[:, 0] - lam * p[:, 1]
    return jnp.einsum('bhqk,bkhe->bqhe', pd.astype(v.dtype), v)


def diff_attn_mixer(z, zc, lam_vecs, sub_g, lam_init, cos, sin, ctx_out):
    B_, T, _ = z.shape
    Tc = zc.shape[1]
    q, k, v = jnp.split(z, 3, axis=-1)
    qc, kc, vc = jnp.split(zc, 3, axis=-1)
    q = rope_2d(q.reshape(B_, T, DA_HEADS, 2, DA_QD), cos, sin)
    k = rope_2d(k.reshape(B_, T, DA_HEADS, 2, DA_QD), cos, sin)
    v = v.reshape(B_, T, DA_HEADS, DA_VD)
    kc = kc.reshape(B_, Tc, DA_HEADS, 2, DA_QD)
    vc = vc.reshape(B_, Tc, DA_HEADS, DA_VD)
    lf = lam_vecs.astype(F32)
    lam = jnp.exp(jnp.sum(lf[0] * lf[1])) - jnp.exp(jnp.sum(lf[2] * lf[3])) + lam_init
    K = jnp.concatenate([kc, k], axis=1)
    V = jnp.concatenate([vc, v], axis=1)
    nb = T // Q_BLOCK
    qb = jnp.moveaxis(q.reshape(B_, nb, Q_BLOCK, DA_HEADS, 2, DA_QD), 1, 0)
    o = lax.map(lambda blk: diff_attend(blk, K, V, lam), qb)
    o = jnp.moveaxis(o, 0, 1).reshape(B_, T, DA_HEADS, DA_VD)
    y = (rms_norm(o, sub_g) * (1.0 - lam_init)).reshape(B_, T, DA_WIDTH).astype(z.dtype)
    yc = None
    if ctx_out:
        oc = diff_attend(qc.reshape(B_, Tc, DA_HEADS, 2, DA_QD), kc, vc, lam)
        yc = (rms_norm(oc, sub_g) * (1.0 - lam_init)).reshape(B_, Tc, DA_WIDTH).astype(z.dtype)
    return y, yc


def spatial_gating(z, norm_g, w_sp, b_sp):
    B_, T, _ = z.shape
    u, v = jnp.split(jax.nn.gelu(z), 2, axis=-1)
    v = rms_norm(v, norm_g).reshape(B_, T // SG_CHUNK, SG_CHUNK, SG_HEADS, SG_HD)
    vm = jnp.einsum('hpq,bnqhc->bnphc', w_sp, v) + b_sp.T[None, None, :, :, None]
    return u * vm.reshape(B_, T, SG_WIDTH)


def hgrn_gates(f_pre, lb):
    f = lb + (1.0 - lb) * jax.nn.sigmoid(f_pre.astype(F32))
    return jnp.log(f), 1.0 - f


def gla_scan(q, k, v, log_f, S0):
    B_, T, H, Dk = q.shape
    nc = T // HG_CHUNK
    to_chunks = lambda t: jnp.transpose(t.reshape(B_, nc, HG_CHUNK, H, t.shape[-1]), (1, 0, 3, 2, 4))
    mask = jnp.tril(jnp.ones((HG_CHUNK, HG_CHUNK), dtype=bool))[:, :, None]

    def step(S, inp):
        qc, kc, vc, lfc = inp
        b = jnp.cumsum(lfc, axis=2)
        diff = b[:, :, :, None, :] - b[:, :, None, :, :]
        dec = jnp.exp(jnp.where(mask, diff, -jnp.inf))
        att = jnp.einsum('bhtd,bhsd,bhtsd->bhts', qc, kc, dec)
        o = att @ vc + jnp.einsum('bhtd,bhdv->bhtv', qc * jnp.exp(b), S)
        b_last = b[:, :, -1:]
        S_new = jnp.exp(b_last[:, :, 0])[..., None] * S + jnp.einsum('bhsd,bhsv->bhdv', kc * jnp.exp(b_last - b), vc)
        return S_new, o

    S_T, o = lax.scan(step, S0, (to_chunks(q), to_chunks(k), to_chunks(v), to_chunks(log_f)))
    o = jnp.transpose(o, (1, 0, 3, 2, 4)).reshape(B_, T, H, v.shape[-1])
    return o, S_T


def gla_final(k, v, log_f):
    c = jnp.cumsum(log_f, axis=1)
    return jnp.einsum('bthd,bthv->bhdv', k * jnp.exp(c[:, -1:] - c), v)


def hgrn2_mixer(z, zc, lb, norm_g, ctx_out):
    heads = lambda t: t.reshape(t.shape[0], t.shape[1], HG_HEADS, HG_HD).astype(F32)
    q, f_fw, f_bw, i, g = jnp.split(z, 5, axis=-1)
    qc, fc_fw, fc_bw, ic, gc = jnp.split(zc, 5, axis=-1)
    q = heads(q) * (HG_HD ** -0.5)
    i = heads(i)
    ic = heads(ic)
    B_ = z.shape[0]
    o_sum, oc_sum = 0.0, 0.0
    for d, (fl, fcl) in enumerate(((f_fw, fc_fw), (f_bw, fc_bw))):
        rev = d == 1
        lb_h = lb[d].reshape(HG_HEADS, HG_HD)
        lf_c, k_c = hgrn_gates(tflip(heads(fcl), rev), lb_h)
        v_c = tflip(ic, rev)
        if ctx_out:
            S0 = jnp.zeros((B_, HG_HEADS, HG_HD, HG_HD), F32)
            o_c, S_c = gla_scan(tflip(heads(qc) * (HG_HD ** -0.5), rev), k_c, v_c, lf_c, S0)
            oc_sum = oc_sum + tflip(o_c, rev)
        else:
            S_c = gla_final(k_c, v_c, lf_c)
        lf, k = hgrn_gates(tflip(heads(fl), rev), lb_h)
        o, _ = gla_scan(tflip(q, rev), k, tflip(i, rev), lf, S_c)
        o_sum = o_sum + tflip(o, rev)
    gn = norm_g.reshape(HG_HEADS, HG_HD)
    y = (rms_norm(o_sum, gn).reshape(z.shape[0], z.shape[1], HG_WIDTH) * jax.nn.silu(g.astype(F32))).astype(z.dtype)
    yc = None
    if ctx_out:
        yc = (rms_norm(oc_sum, gn).reshape(zc.shape[0], zc.shape[1], HG_WIDTH) * jax.nn.silu(gc.astype(F32))).astype(z.dtype)
    return y, yc


def moe(h, w_router, b_router, w1, w3, w2):
    logits = (h @ w_router).astype(F32) + b_router.astype(F32)
    scores = jax.nn.softmax(logits, axis=-1)
    grp = scores.reshape(scores.shape[:-1] + (N_GROUPS, EXP_PER_GROUP))
    grp_score = jnp.sum(lax.top_k(grp, TOP_K)[0], axis=-1)
    g_sel = jnp.argmax(grp_score, axis=-1)
    in_group = g_sel[..., None] == (jnp.arange(N_EXPERTS) // EXP_PER_GROUP)
    masked = jnp.where(in_group, scores, -jnp.inf)
    top_w, top_i = lax.top_k(masked, TOP_K)
    top_w = top_w / jnp.sum(top_w, axis=-1, keepdims=True)
    gate = jnp.sum(jax.nn.one_hot(top_i, N_EXPERTS, dtype=F32) * top_w[..., None], axis=-2)
    out = 0.0
    for e in range(N_EXPERTS):
        he = jax.nn.silu(h @ w1[e]) * (h @ w3[e])
        out = out + gate[..., e:e + 1].astype(h.dtype) * (he @ w2[e])
    return out


def setup_inputs(seed: int = 0) -> dict:
    key = jax.random.key(seed)
    ks = iter(jax.random.split(key, 40))
    nrm = lambda shape, s: s * jax.random.normal(next(ks), shape, F32)
    L = DEPTH
    a_pow = jax.random.uniform(next(ks), (L, 2, LRU_WIDTH), F32, minval=0.9, maxval=0.999)
    a0 = a_pow ** (1.0 / RG_C)
    lru_lam = jnp.log(a0) - jnp.log1p(-a0)
    return {
        'x': nrm((BATCH, SEQ, D_MODEL), 1.0),
        'c': nrm((BATCH, D_MODEL), 1.0),
        'ctx': nrm((BATCH, CTX_LEN, D_MODEL), 1.0),
        'c_ctx': nrm((D_MODEL,), 1.0),
        'w_ada': nrm((L, D_MODEL, 6 * D_MODEL), 0.5 * D_MODEL ** -0.5),
        'b_ada': nrm((L, 6 * D_MODEL), 0.02),
        'norm1_g': 1.0 + nrm((L, D_MODEL), 0.02),
        'norm2_g': 1.0 + nrm((L, D_MODEL), 0.02),
        'w_in': nrm((L, D_MODEL, IN_COLS), D_MODEL ** -0.5),
        'w_out': nrm((L, D_MIX, D_MODEL), D_MIX ** -0.5),
        'lru_conv_w': nrm((L, CONV_W, LRU_WIDTH), CONV_W ** -0.5),
        'lru_conv_b': nrm((L, LRU_WIDTH), 0.02),
        'lru_wr': nrm((L, 2, LRU_HEADS, LRU_HD, LRU_HD), LRU_HD ** -0.5),
        'lru_br': nrm((L, 2, LRU_WIDTH), 0.02),
        'lru_wi': nrm((L, 2, LRU_HEADS, LRU_HD, LRU_HD), LRU_HD ** -0.5),
        'lru_bi': nrm((L, 2, LRU_WIDTH), 0.02),
        'lru_lam': lru_lam,
        'da_lam': nrm((L, 4, DA_QD), 0.1),
        'da_subln_g': 1.0 + nrm((L, DA_VD), 0.02),
        'sg_norm_g': 1.0 + nrm((L, SG_WIDTH), 0.02),
        'sg_w': nrm((L, SG_HEADS, SG_CHUNK, SG_CHUNK), SG_CHUNK ** -0.5),
        'sg_b': 1.0 + nrm((L, SG_HEADS, SG_CHUNK), 0.02),
        'hg_lb': nrm((2, L, HG_WIDTH), 0.1),
        'hg_norm_g': 1.0 + nrm((L, HG_WIDTH), 0.02),
        'router_w': nrm((D_MODEL, N_EXPERTS), D_MODEL ** -0.5),
        'router_b': nrm((N_EXPERTS,), 0.01),
        'moe_w1': nrm((L, N_EXPERTS, D_MODEL, D_EXPERT), D_MODEL ** -0.5),
        'moe_w3': nrm((L, N_EXPERTS, D_MODEL, D_EXPERT), D_MODEL ** -0.5),
        'moe_w2': nrm((L, N_EXPERTS, D_EXPERT, D_MODEL), D_EXPERT ** -0.5),
        'final_norm_g': 1.0 + nrm((D_MODEL,), 0.02),
    }


def reference(x, c, ctx, c_ctx, w_ada, b_ada, norm1_g, norm2_g, w_in, w_out,
              lru_conv_w, lru_conv_b, lru_wr, lru_br, lru_wi, lru_bi, lru_lam,
              da_lam, da_subln_g, sg_norm_g, sg_w, sg_b, hg_lb, hg_norm_g,
              router_w, router_b, moe_w1, moe_w3, moe_w2, final_norm_g):
    seq = x.shape[1]
    rows = seq // GRID_W
    row_ids = jnp.repeat(jnp.arange(rows, dtype=F32), GRID_W)
    col_ids = jnp.tile(jnp.arange(GRID_W, dtype=F32), rows)
    n_freq = DA_QD // 4
    freqs = ROPE_THETA ** (-jnp.arange(n_freq, dtype=F32) / n_freq)
    ang = jnp.stack([row_ids[:, None] * freqs, col_ids[:, None] * freqs], axis=1)
    cos = jnp.cos(ang).astype(x.dtype)
    sin = jnp.sin(ang).astype(x.dtype)
    lb_cum = jnp.cumsum(jax.nn.softmax(hg_lb.astype(F32), axis=1), axis=1)
    lb_all = lb_cum - lb_cum[:, :1]
    silu_c = jax.nn.silu(c)
    silu_cc = jax.nn.silu(c_ctx)
    xc = ctx
    for l in range(DEPTH):
        ctx_out = l < DEPTH - 1
        mod = (silu_c @ w_ada[l] + b_ada[l])[:, None, :]
        mod_c = silu_cc @ w_ada[l] + b_ada[l]
        sh1, sc1, gt1, sh2, sc2, gt2 = jnp.split(mod, 6, axis=-1)
        shc1, scc1, gtc1, shc2, scc2, gtc2 = jnp.split(mod_c, 6, axis=-1)
        h = modulate(rms_norm(x, norm1_g[l]), sh1, sc1)
        hc = modulate(rms_norm(xc, norm1_g[l]), shc1, scc1)
        za, zb, zs, zh = jnp.split(h @ w_in[l], SPLITS, axis=-1)
        zca, zcb, zcs, zch = jnp.split(hc @ w_in[l], SPLITS, axis=-1)
        ya, yca = rglru_mixer(za, zca, lru_conv_w[l], lru_conv_b[l], lru_wr[l], lru_br[l],
                              lru_wi[l], lru_bi[l], lru_lam[l], ctx_out)
        lam_init = 0.8 - 0.6 * math.exp(-0.3 * l)
        yb, ycb = diff_attn_mixer(zb, zcb, da_lam[l], da_subln_g[l], lam_init, cos, sin, ctx_out)
        ys = spatial_gating(zs, sg_norm_g[l], sg_w[l], sg_b[l])
        yh, ych = hgrn2_mixer(zh, zch, lb_all[:, l], hg_norm_g[l], ctx_out)
        x = x + gt1 * (jnp.concatenate([ya, yb, ys, yh], axis=-1) @ w_out[l])
        h2 = modulate(rms_norm(x, norm2_g[l]), sh2, sc2)
        x = x + gt2 * moe(h2, router_w, router_b, moe_w1[l], moe_w3[l], moe_w2[l])
        if ctx_out:
            ycs = spatial_gating(zcs, sg_norm_g[l], sg_w[l], sg_b[l])
            xc = xc + gtc1 * (jnp.concatenate([yca, ycb, ycs, ych], axis=-1) @ w_out[l])
            hc2 = modulate(rms_norm(xc, norm2_g[l]), shc2, scc2)
            xc = xc + gtc2 * moe(hc2, router_w, router_b, moe_w1[l], moe_w3[l], moe_w2[l])
    return rms_norm(x, final_norm_g)
```

```python
import functools
import math

import jax
import jax.numpy as jnp
from jax import lax
from jax.experimental import pallas as pl
from jax.experimental.pallas import tpu as pltpu

F32 = jnp.float32
BF16 = jnp.bfloat16
HIGHEST = lax.Precision.HIGHEST

D_MODEL = 1024
SEQ = 2048
CTX_LEN = 256
SEQ_ALL = CTX_LEN + SEQ
DEPTH = 2
GRID_W = 64
EPS = 1e-6
WIDTH = 256
HEADS = 4
HEAD_DIM = 64
CONV_W = 4
RG_C = 8.0
DA_QD = 32
ROPE_THETA = 10000.0
SG_CHUNK = 128
N_EXPERTS = 16
N_GROUPS = 4
EXP_PER_GROUP = 4
D_EXPERT = 512
IN_COLS = 3072
COL_A_X, COL_A_G = 0, 1
COL_B_Q, COL_B_K, COL_B_V = 2, 3, 4
COL_C_U, COL_C_V = 5, 6
COL_D_Q, COL_D_FF, COL_D_FB, COL_D_I, COL_D_G = 7, 8, 9, 10, 11

ADA_ROWS = 16
CTX_ROW = 8
VMEM_LIMIT = 56 * 1024 * 1024

LRU_BLK = 8
HG_CHUNK = 16
ATT_TQ = 256
DENSE_TM = 768
MOE_TM = 1152
MOE_TM_LATENT = 1024


def _params(*sem):
    return pltpu.CompilerParams(dimension_semantics=sem, vmem_limit_bytes=VMEM_LIMIT)


def _rms(xf, g):
    return xf * lax.rsqrt(jnp.mean(xf * xf, axis=-1, keepdims=True) + EPS) * g


def _sigmoid(x):
    return 1.0 / (1.0 + jnp.exp(-x))


def _silu(x):
    return x * _sigmoid(x)


def _gelu(x):
    return jax.nn.gelu(x)


def _head_ones(dtype):
    r = lax.broadcasted_iota(jnp.int32, (WIDTH, WIDTH), 0) // HEAD_DIM
    c = lax.broadcasted_iota(jnp.int32, (WIDTH, WIDTH), 1) // HEAD_DIM
    return (r == c).astype(dtype)


def _mod_picker(modl_ref, modc_ref, tm, has_ctx):
    ml = modl_ref[0]
    if not has_ctx:
        return lambda r: ml[r:r + 1]
    mc = modc_ref[0]
    row0 = (pl.program_id(0) % (SEQ_ALL // tm)) * tm
    is_ctx = row0 + lax.broadcasted_iota(jnp.int32, (tm, 1), 0) < CTX_LEN
    return lambda r: jnp.where(is_ctx, mc[r:r + 1], ml[r:r + 1])


def _ada_kernel(c_ref, w_ref, b_ref, o_ref):
    s = _silu(c_ref[...]).astype(BF16)
    o_ref[...] = jnp.dot(s, w_ref[...].astype(BF16), preferred_element_type=F32) + b_ref[...]


def _ada(cvec, w_ada, b_ada):
    tn = 1536
    return pl.pallas_call(
        _ada_kernel,
        grid=(DEPTH, 6 * D_MODEL // tn),
        in_specs=[
            pl.BlockSpec((ADA_ROWS, D_MODEL), lambda l, j: (0, 0)),
            pl.BlockSpec((None, D_MODEL, tn), lambda l, j: (l, 0, j)),
            pl.BlockSpec((None, 1, tn), lambda l, j: (l, 0, j)),
        ],
        out_specs=pl.BlockSpec((None, ADA_ROWS, tn), lambda l, j: (l, 0, j)),
        out_shape=jax.ShapeDtypeStruct((DEPTH, ADA_ROWS, 6 * D_MODEL), F32),
        compiler_params=_params("arbitrary", "arbitrary"),
        name="ada",
    )(cvec, w_ada, b_ada.reshape(DEPTH, 1, 6 * D_MODEL))


def _inproj_kernel(x_ref, g_ref, modl_ref, modc_ref, w_ref, z_ref, *, tm):
    pick = _mod_picker(modl_ref, modc_ref, tm, True)
    h = _rms(x_ref[...], g_ref[...]) * (1.0 + pick(1)) + pick(0)
    z_ref[...] = jnp.dot(h.astype(BF16), w_ref[...], preferred_element_type=F32)


def _mod_specs(tiles_per_batch):
    return [
        pl.BlockSpec((1, 6, D_MODEL), lambda i, *_: (i // tiles_per_batch, 0, 0)),
        pl.BlockSpec((1, 6, D_MODEL), lambda i, *_: (CTX_ROW, 0, 0)),
    ]


def _inproj(x, g, mod, w_bf16):
    n = x.shape[0]
    tm = DENSE_TM
    return pl.pallas_call(
        functools.partial(_inproj_kernel, tm=tm),
        grid=(n // tm,),
        in_specs=[
            pl.BlockSpec((tm, D_MODEL), lambda i: (i, 0)),
            pl.BlockSpec((1, D_MODEL), lambda i: (0, 0)),
            *_mod_specs(SEQ_ALL // tm),
            pl.BlockSpec((D_MODEL, IN_COLS), lambda i: (0, 0)),
        ],
        out_specs=pl.BlockSpec((tm, IN_COLS), lambda i: (i, 0)),
        out_shape=jax.ShapeDtypeStruct((n, IN_COLS), F32),
        compiler_params=_params("arbitrary"),
        name="inproj",
    )(x, g.reshape(1, D_MODEL), mod, mod, w_bf16)


def _lru_kernel(x_ref, gate_ref, cw_ref, cb_ref, wr_ref, br_ref, wi_ref, bi_ref, lam_ref,
                y_ref, a_s, b_s, h_s, *, out_rows):
    x = x_ref[...]
    rows = lax.broadcasted_iota(jnp.int32, (SEQ_ALL, 1), 0)
    seg = rows < CTX_LEN
    u = jnp.zeros_like(x)
    for j in range(CONV_W):
        off = j - CONV_W // 2
        xs = x if off == 0 else pltpu.roll(x, (-off) % SEQ_ALL, 0)
        src = rows + off
        ok = (src >= 0) & (src < SEQ_ALL) & ((src < CTX_LEN) == seg)
        u = u + jnp.where(ok, xs, 0.0) * cw_ref[j:j + 1, :]
    u = u + cb_ref[...]
    ub = u.astype(BF16)
    for d in range(2):
        r = _sigmoid(jnp.dot(ub, wr_ref[d], preferred_element_type=F32) + br_ref[d:d + 1, :])
        i = _sigmoid(jnp.dot(ub, wi_ref[d], preferred_element_type=F32) + bi_ref[d:d + 1, :])
        nl = -lam_ref[d:d + 1, :]
        softplus = jnp.maximum(nl, 0.0) + jnp.log(1.0 + jnp.exp(-jnp.abs(nl)))
        log_a = -RG_C * r * softplus
        a_s[d] = jnp.exp(log_a)
        b_s[d] = jnp.sqrt(1.0 - jnp.exp(2.0 * log_a)) * i * u

    n_blk = SEQ_ALL // LRU_BLK
    n_ctx_blk = CTX_LEN // LRU_BLK
    sub = lax.broadcasted_iota(jnp.int32, (LRU_BLK, 1), 0)

    def block_scan(a, b, reverse):
        s = 1
        while s < LRU_BLK:
            if reverse:
                a_sh = pltpu.roll(a, LRU_BLK - s, 0)
                b_sh = pltpu.roll(b, LRU_BLK - s, 0)
                ok = sub < LRU_BLK - s
            else:
                a_sh = pltpu.roll(a, s, 0)
                b_sh = pltpu.roll(b, s, 0)
                ok = sub >= s
            b = jnp.where(ok, a * b_sh + b, b)
            a = jnp.where(ok, a * a_sh, a)
            s *= 2
        return a, b

    def body(n, carry):
        hf, hb = carry
        rf = pl.multiple_of(n * LRU_BLK, LRU_BLK)
        af, bf = block_scan(a_s[0, pl.ds(rf, LRU_BLK), :], b_s[0, pl.ds(rf, LRU_BLK), :], False)
        h = af * hf + bf
        h_s[0, pl.ds(rf, LRU_BLK), :] = h
        hf = h[LRU_BLK - 1:LRU_BLK, :]
        nb = jnp.where(n < n_ctx_blk, n_ctx_blk - 1 - n, n_blk + n_ctx_blk - 1 - n)
        rb = pl.multiple_of(nb * LRU_BLK, LRU_BLK)
        ab, bb = block_scan(a_s[1, pl.ds(rb, LRU_BLK), :], b_s[1, pl.ds(rb, LRU_BLK), :], True)
        h = ab * hb + bb
        h_s[1, pl.ds(rb, LRU_BLK), :] = h
        hb = h[0:1, :]
        return hf, hb

    zero = jnp.zeros((1, WIDTH), F32)
    lax.fori_loop(0, n_blk, body, (zero, zero))
    first = SEQ_ALL - out_rows
    y_ref[...] = (h_s[0, first:, :] + h_s[1, first:, :]) * _gelu(gate_ref[first:, :])


def _lru(z, cw, cb, wr_bd, br, wi_bd, bi, lam, out_rows):
    nb = z.shape[0] // SEQ_ALL
    full = lambda shape: pl.BlockSpec(shape, lambda b: (0,) * len(shape))
    return pl.pallas_call(
        functools.partial(_lru_kernel, out_rows=out_rows),
        grid=(nb,),
        in_specs=[
            pl.BlockSpec((SEQ_ALL, WIDTH), lambda b: (b, COL_A_X)),
            pl.BlockSpec((SEQ_ALL, WIDTH), lambda b: (b, COL_A_G)),
            full((CONV_W, WIDTH)), full((1, WIDTH)),
            full((2, WIDTH, WIDTH)), full((2, WIDTH)),
            full((2, WIDTH, WIDTH)), full((2, WIDTH)),
            full((2, WIDTH)),
        ],
        out_specs=pl.BlockSpec((out_rows, WIDTH), lambda b: (b, 0)),
        out_shape=jax.ShapeDtypeStruct((nb * out_rows, WIDTH), F32),
        scratch_shapes=[
            pltpu.VMEM((2, SEQ_ALL, WIDTH), F32),
            pltpu.VMEM((2, SEQ_ALL, WIDTH), F32),
            pltpu.VMEM((2, SEQ_ALL, WIDTH), F32),
        ],
        compiler_params=_params("arbitrary"),
        name="rglru",
    )(z, z, cw, cb, wr_bd, br, wi_bd, bi, lam)


def _rope(x, cos, sin_signed):
    lane = lax.broadcasted_iota(jnp.int32, (1, WIDTH), 1)
    first_half = (lane % 16) < 8
    partner = jnp.where(first_half, pltpu.roll(x, WIDTH - 8, 1), pltpu.roll(x, 8, 1))
    return x * cos + partner * sin_signed


def _attn_kernel(lam_ref, q_ref, k_ref, v_ref, cos_ref, sin_ref, g_ref, y_ref, kt_s, v_s,
                 *, with_ctx, out_scale):
    j = pl.program_id(1)

    @pl.when(j == 0)
    def _():
        kr = _rope(k_ref[...], cos_ref[...], sin_ref[...])
        kt_s[...] = kr.T.astype(BF16)
        v = v_ref[...]
        for h in range(HEADS):
            v_s[h] = v[:, h * HEAD_DIM:(h + 1) * HEAD_DIM].astype(BF16)

    lam = lam_ref[0, 0]
    tile = j if with_ctx else j + 1
    row0 = pl.multiple_of(tile * ATT_TQ, ATT_TQ)
    q = _rope(q_ref[...], cos_ref[pl.ds(row0, ATT_TQ), :], sin_ref[pl.ds(row0, ATT_TQ), :])
    q = q * (DA_QD ** -0.5)

    def attend(n_keys):
        outs = []
        for h in range(HEADS):
            probs = []
            for n in range(2):
                c0 = h * HEAD_DIM + n * DA_QD
                s = jnp.dot(q[:, c0:c0 + DA_QD].astype(BF16), kt_s[c0:c0 + DA_QD, 0:n_keys],
                            preferred_element_type=F32)
                p = jnp.exp(s - jnp.max(s, axis=-1, keepdims=True))
                probs.append(p * (1.0 / jnp.sum(p, axis=-1, keepdims=True)))
            pd = (probs[0] - lam * probs[1]).astype(BF16)
            o = jnp.dot(pd, v_s[h, 0:n_keys, :], preferred_element_type=F32)
            o = o * lax.rsqrt(jnp.mean(o * o, axis=-1, keepdims=True) + EPS)
            outs.append(o)
        y_ref[...] = jnp.concatenate(outs, axis=-1) * g_ref[...] * out_scale

    if with_ctx:
        @pl.when(j == 0)
        def _():
            attend(CTX_LEN)

        @pl.when(j > 0)
        def _():
            attend(SEQ_ALL)
    else:
        attend(SEQ_ALL)


def _attn(z, lam, cos, sin_signed, sub_g4, with_ctx, out_scale):
    nb = z.shape[0] // SEQ_ALL
    tiles = SEQ_ALL // ATT_TQ
    nq = tiles if with_ctx else tiles - 1
    first = 0 if with_ctx else 1
    return pl.pallas_call(
        functools.partial(_attn_kernel, with_ctx=with_ctx, out_scale=out_scale),
        grid=(nb, nq),
        in_specs=[
            pl.BlockSpec(memory_space=pltpu.SMEM),
            pl.BlockSpec((ATT_TQ, WIDTH), lambda b, j: (b * tiles + j + first, COL_B_Q)),
            pl.BlockSpec((SEQ_ALL, WIDTH), lambda b, j: (b, COL_B_K)),
            pl.BlockSpec((SEQ_ALL, WIDTH), lambda b, j: (b, COL_B_V)),
            pl.BlockSpec((SEQ_ALL, WIDTH), lambda b, j: (0, 0)),
            pl.BlockSpec((SEQ_ALL, WIDTH), lambda b, j: (0, 0)),
            pl.BlockSpec((1, WIDTH), lambda b, j: (0, 0)),
        ],
        out_specs=pl.BlockSpec((ATT_TQ, WIDTH), lambda b, j: (b * nq + j, 0)),
        out_shape=jax.ShapeDtypeStruct((nb * nq * ATT_TQ, WIDTH), F32),
        scratch_shapes=[
            pltpu.VMEM((WIDTH, SEQ_ALL), BF16),
            pltpu.VMEM((HEADS, SEQ_ALL, HEAD_DIM), BF16),
        ],
        compiler_params=_params("arbitrary", "arbitrary"),
        name="diffattn",
    )(lam, z, z, z, cos, sin_signed, sub_g4)


def _sgu_kernel(u_ref, v_ref, g_ref, w_ref, b_ref, y_ref, *, out_rows):
    row_head = lax.broadcasted_iota(jnp.int32, (HEADS * SG_CHUNK, WIDTH), 0) // SG_CHUNK
    col_head = lax.broadcasted_iota(jnp.int32, (HEADS * SG_CHUNK, WIDTH), 1) // HEAD_DIM
    head_mask = row_head == col_head
    w = w_ref[...]
    bias = b_ref[...]
    g = g_ref[...]
    first = SEQ_ALL - out_rows
    for n in range(first // SG_CHUNK, SEQ_ALL // SG_CHUNK):
        rows = slice(n * SG_CHUNK, (n + 1) * SG_CHUNK)
        vn = _rms(_gelu(v_ref[rows, :]), g).astype(BF16)
        stacked = jnp.where(head_mask, jnp.concatenate([vn] * HEADS, axis=0), jnp.zeros((), BF16))
        vm = jnp.dot(w, stacked, preferred_element_type=F32) + bias
        y_ref[n * SG_CHUNK - first:(n + 1) * SG_CHUNK - first, :] = _gelu(u_ref[rows, :]) * vm


def _sgu(z, norm_g, w_cat, bias2d, out_rows):
    nb = z.shape[0] // SEQ_ALL
    return pl.pallas_call(
        functools.partial(_sgu_kernel, out_rows=out_rows),
        grid=(nb,),
        in_specs=[
            pl.BlockSpec((SEQ_ALL, WIDTH), lambda b: (b, COL_C_U)),
            pl.BlockSpec((SEQ_ALL, WIDTH), lambda b: (b, COL_C_V)),
            pl.BlockSpec((1, WIDTH), lambda b: (0, 0)),
            pl.BlockSpec((SG_CHUNK, HEADS * SG_CHUNK), lambda b: (0, 0)),
            pl.BlockSpec((SG_CHUNK, WIDTH), lambda b: (0, 0)),
        ],
        out_specs=pl.BlockSpec((out_rows, WIDTH), lambda b: (b, 0)),
        out_shape=jax.ShapeDtypeStruct((nb * out_rows, WIDTH), F32),
        compiler_params=_params("arbitrary"),
        name="sgu",
    )(z, z, norm_g, w_cat, bias2d)


def _hgrn_kernel(q_ref, ff_ref, fb_ref, i_ref, gate_ref, lb_ref, gn_ref, y_ref,
                 b_s, k_s, o_s, st_s, *, out_rows):
    c = HG_CHUNK
    pre = 128
    ones_bf = _head_ones(BF16)
    ones_f = _head_ones(F32)
    pr = lax.broadcasted_iota(jnp.int32, (pre, pre), 0)
    pc = lax.broadcasted_iota(jnp.int32, (pre, pre), 1)
    same = (pr // c) == (pc // c)
    tri = ((same & (pc <= pr)).astype(F32), (same & (pc >= pr)).astype(F32))

    for d, f_ref in enumerate((ff_ref, fb_ref)):
        lb = lb_ref[d:d + 1, :]
        for n in range(SEQ_ALL // pre):
            rows = slice(n * pre, (n + 1) * pre)
            f = lb + (1.0 - lb) * _sigmoid(f_ref[rows, :])
            k_s[d, rows, :] = 1.0 - f
            b_s[d, rows, :] = jnp.dot(tri[d], jnp.log(f), precision=HIGHEST,
                                      preferred_element_type=F32)
    st_s[...] = jnp.zeros_like(st_s)

    n_chunks = SEQ_ALL // c
    n_ctx = CTX_LEN // c
    sub = lax.broadcasted_iota(jnp.int32, (c, 1), 0)
    scale = HEAD_DIM ** -0.5

    def one_direction(d, chunk):
        r0 = pl.multiple_of(chunk * c, c)
        q = q_ref[pl.ds(r0, c), :] * scale
        v = i_ref[pl.ds(r0, c), :]
        b = b_s[d, pl.ds(r0, c), :]
        k = k_s[d, pl.ds(r0, c), :]
        tot = b[c - 1:c, :] if d == 0 else b[0:1, :]
        st = st_s[d]
        o = lax.dot_general((q * jnp.exp(b)).astype(BF16), st.astype(BF16),
                            (((1,), (1,)), ((), ())), preferred_element_type=F32)
        parts = []
        for s in range(c):
            ok = (sub >= s) if d == 0 else (sub <= s)
            diff = jnp.where(ok, b - b[s:s + 1, :], -jnp.inf)
            parts.append((q * k[s:s + 1, :] * jnp.exp(diff)).astype(BF16))
        att = jnp.dot(jnp.concatenate(parts, axis=0), ones_bf, preferred_element_type=F32)
        for s in range(c):
            o = o + att[s * c:(s + 1) * c, :] * v[s:s + 1, :]
        o_s[d, pl.ds(r0, c), :] = o
        ke = (k * jnp.exp(tot - b)).astype(BF16)
        upd = lax.dot_general(v.astype(BF16), ke, (((0,), (0,)), ((), ())),
                              preferred_element_type=F32)
        st_s[d] = st * jnp.exp(tot) + upd * ones_f

    def body(n, carry):
        one_direction(0, n)
        one_direction(1, jnp.where(n < n_ctx, n_ctx - 1 - n, n_chunks + n_ctx - 1 - n))
        return carry

    lax.fori_loop(0, n_chunks, body, 0)

    first = SEQ_ALL - out_rows
    o = o_s[0, first:, :] + o_s[1, first:, :]
    ms = jnp.dot(o * o, ones_f * (1.0 / HEAD_DIM), precision=HIGHEST, preferred_element_type=F32)
    y_ref[...] = o * lax.rsqrt(ms + EPS) * gn_ref[...] * _silu(gate_ref[first:, :])


def _hgrn(z, lb, gn, out_rows):
    nb = z.shape[0] // SEQ_ALL
    col = lambda cidx: pl.BlockSpec((SEQ_ALL, WIDTH), lambda b: (b, cidx))
    return pl.pallas_call(
        functools.partial(_hgrn_kernel, out_rows=out_rows),
        grid=(nb,),
        in_specs=[
            col(COL_D_Q), col(COL_D_FF), col(COL_D_FB), col(COL_D_I), col(COL_D_G),
            pl.BlockSpec((2, WIDTH), lambda b: (0, 0)),
            pl.BlockSpec((1, WIDTH), lambda b: (0, 0)),
        ],
        out_specs=pl.BlockSpec((out_rows, WIDTH), lambda b: (b, 0)),
        out_shape=jax.ShapeDtypeStruct((nb * out_rows, WIDTH), F32),
        scratch_shapes=[
            pltpu.VMEM((2, SEQ_ALL, WIDTH), F32),
            pltpu.VMEM((2, SEQ_ALL, WIDTH), F32),
            pltpu.VMEM((2, SEQ_ALL, WIDTH), F32),
            pltpu.VMEM((2, WIDTH, WIDTH), F32),
        ],
        compiler_params=_params("arbitrary"),
        name="hgrn2",
    )(z, z, z, z, z, lb, gn)


def _route(logits):
    rows = [logits[e:e + 1, :] for e in range(N_EXPERTS)]
    m = functools.reduce(jnp.maximum, rows)
    ex = [jnp.exp(r - m) for r in rows]
    inv = 1.0 / functools.reduce(jnp.add, ex)
    sc = [e * inv for e in ex]
    g_score = []
    for g in range(N_GROUPS):
        grp = sc[g * EXP_PER_GROUP:(g + 1) * EXP_PER_GROUP]
        pairs = [grp[a] + grp[b] for a in range(EXP_PER_GROUP) for b in range(a + 1, EXP_PER_GROUP)]
        g_score.append(functools.reduce(jnp.maximum, pairs))
    gates = []
    for g in range(N_GROUPS):
        g_ok = None
        for o in range(N_GROUPS):
            if o == g:
                continue
            t = (g_score[g] > g_score[o]) if o < g else (g_score[g] >= g_score[o])
            g_ok = t if g_ok is None else (g_ok & t)
        grp = sc[g * EXP_PER_GROUP:(g + 1) * EXP_PER_GROUP]
        picked = []
        for a in range(EXP_PER_GROUP):
            beaten = jnp.zeros_like(grp[a])
            for o in range(EXP_PER_GROUP):
                if o == a:
                    continue
                t = (grp[o] >= grp[a]) if o < a else (grp[o] > grp[a])
                beaten = beaten + jnp.where(t, 1.0, 0.0)
            picked.append(jnp.where((beaten < 1.5) & g_ok, grp[a], 0.0))
        denom = functools.reduce(jnp.add, picked)
        denom = jnp.where(g_ok, denom, 1.0)
        gates.extend([p / denom for p in picked])
    return jnp.concatenate(gates, axis=0)


def _outproj_kernel(x_ref, ya_ref, yb_ref, ys_ref, yh_ref, w_ref, modl_ref, modc_ref, g_ref,
                    wr_ref, br_ref, xo_ref, h_ref, gate_ref, *, tm, has_ctx):
    pick = _mod_picker(modl_ref, modc_ref, tm, has_ctx)
    acc = jnp.zeros((tm, D_MODEL), F32)
    for kblk, y_ref in enumerate((ya_ref, yb_ref, ys_ref, yh_ref)):
        acc = acc + jnp.dot(y_ref[...].astype(BF16), w_ref[kblk * WIDTH:(kblk + 1) * WIDTH, :],
                            preferred_element_type=F32)
    x = x_ref[...] + pick(2) * acc
    xo_ref[...] = x
    h = _rms(x, g_ref[...]) * (1.0 + pick(4)) + pick(3)
    h_ref[...] = h.astype(BF16)
    logits = lax.dot_general(wr_ref[...], h, (((1,), (1,)), ((), ())), precision=HIGHEST,
                             preferred_element_type=F32) + br_ref[...]
    gate_ref[...] = _route(logits)


def _outproj(x, ys, w_bf16, mod, g2, wr_t, br, has_ctx):
    n = ys[0].shape[0]
    if has_ctx:
        tm = DENSE_TM
        tpb = SEQ_ALL // tm
        x_spec = pl.BlockSpec((tm, D_MODEL), lambda i: (i, 0))
    else:
        tm = CTX_LEN
        tpb = SEQ // tm
        x_spec = pl.BlockSpec((tm, D_MODEL),
                              lambda i: ((i // tpb) * (SEQ_ALL // tm) + 1 + i % tpb, 0))
    tile = lambda w: pl.BlockSpec((tm, w), lambda i: (i, 0))
    return pl.pallas_call(
        functools.partial(_outproj_kernel, tm=tm, has_ctx=has_ctx),
        grid=(n // tm,),
        in_specs=[
            x_spec, tile(WIDTH), tile(WIDTH), tile(WIDTH), tile(WIDTH),
            pl.BlockSpec((D_MODEL, D_MODEL), lambda i: (0, 0)),
            *_mod_specs(tpb),
            pl.BlockSpec((1, D_MODEL), lambda i: (0, 0)),
            pl.BlockSpec((N_EXPERTS, D_MODEL), lambda i: (0, 0)),
            pl.BlockSpec((N_EXPERTS, 1), lambda i: (0, 0)),
        ],
        out_specs=[
            tile(D_MODEL), tile(D_MODEL),
            pl.BlockSpec((N_EXPERTS, tm), lambda i: (0, i)),
        ],
        out_shape=[
            jax.ShapeDtypeStruct((n, D_MODEL), F32),
            jax.ShapeDtypeStruct((n, D_MODEL), BF16),
            jax.ShapeDtypeStruct((N_EXPERTS, n), F32),
        ],
        compiler_params=_params("arbitrary"),
        name="outproj_router",
    )(x, *ys, w_bf16, mod, mod, g2.reshape(1, D_MODEL), wr_t, br)


def _moe_kernel(h_ref, gate_ref, x_ref, w1_ref, w3_ref, w2_ref, modl_ref, modc_ref, fg_ref,
                o_ref, acc_s, gt_s, *, tm, has_ctx, final_norm):
    e = pl.program_id(1)

    @pl.when(e == 0)
    def _():
        acc_s[...] = jnp.zeros_like(acc_s)
        gt_s[...] = gate_ref[...].T

    h = h_ref[...]
    lane = lax.broadcasted_iota(jnp.int32, (1, N_EXPERTS), 1)
    gate = jnp.sum(jnp.where(lane == e, gt_s[...], 0.0), axis=-1, keepdims=True)
    a = jnp.dot(h, w1_ref[...], preferred_element_type=F32)
    b = jnp.dot(h, w3_ref[...], preferred_element_type=F32)
    he = (_silu(a) * b * gate).astype(BF16)
    acc_s[...] += jnp.dot(he, w2_ref[...], preferred_element_type=F32)

    @pl.when(e == N_EXPERTS - 1)
    def _():
        pick = _mod_picker(modl_ref, modc_ref, tm, has_ctx)
        x = x_ref[...] + pick(5) * acc_s[...]
        o_ref[...] = _rms(x, fg_ref[...]) if final_norm else x


def _moe(h, gates, x, w1, w3, w2, mod, final_g, has_ctx, final_norm):
    n = x.shape[0]
    tm = MOE_TM if has_ctx else MOE_TM_LATENT
    tpb = (SEQ_ALL if has_ctx else SEQ) // tm
    return pl.pallas_call(
        functools.partial(_moe_kernel, tm=tm, has_ctx=has_ctx, final_norm=final_norm),
        grid=(n // tm, N_EXPERTS),
        in_specs=[
            pl.BlockSpec((tm, D_MODEL), lambda i, e: (i, 0)),
            pl.BlockSpec((N_EXPERTS, tm), lambda i, e: (0, i)),
            pl.BlockSpec((tm, D_MODEL), lambda i, e: (i, 0)),
            pl.BlockSpec((None, D_MODEL, D_EXPERT), lambda i, e: (e, 0, 0)),
            pl.BlockSpec((None, D_MODEL, D_EXPERT), lambda i, e: (e, 0, 0)),
            pl.BlockSpec((None, D_EXPERT, D_MODEL), lambda i, e: (e, 0, 0)),
            *_mod_specs(tpb),
            pl.BlockSpec((1, D_MODEL), lambda i, e: (0, 0)),
        ],
        out_specs=pl.BlockSpec((tm, D_MODEL), lambda i, e: (i, 0)),
        out_shape=jax.ShapeDtypeStruct((n, D_MODEL), F32),
        scratch_shapes=[
            pltpu.VMEM((tm, D_MODEL), F32),
            pltpu.VMEM((tm, N_EXPERTS), F32),
        ],
        compiler_params=_params("arbitrary", "arbitrary"),
        name="moe",
    )(h, gates, x, w1, w3, w2, mod, mod, final_g.reshape(1, D_MODEL))


def _block_diag(w):
    eye = jnp.eye(HEADS, dtype=w.dtype)
    return jnp.einsum('hij,hg->higj', w, eye).reshape(WIDTH, WIDTH)


def _rope_tables():
    rows = SEQ // GRID_W
    row_ids = jnp.repeat(jnp.arange(rows, dtype=F32), GRID_W)
    col_ids = jnp.tile(jnp.arange(GRID_W, dtype=F32), rows)
    n_freq = DA_QD // 4
    freqs = ROPE_THETA ** (-jnp.arange(n_freq, dtype=F32) / n_freq)
    ang = jnp.stack([row_ids[:, None] * freqs, col_ids[:, None] * freqs], axis=1)
    cos = jnp.cos(ang)
    sin = jnp.sin(ang)
    cos_l = jnp.broadcast_to(cos[:, None, None, :, None, :], (SEQ, HEADS, 2, 2, 2, n_freq))
    sin_l = jnp.broadcast_to(sin[:, None, None, :, None, :], (SEQ, HEADS, 2, 2, 2, n_freq))
    sign = jnp.array([-1.0, 1.0], F32)[None, None, None, None, :, None]
    cos_l = cos_l.reshape(SEQ, WIDTH)
    sin_l = (sin_l * sign).reshape(SEQ, WIDTH)
    cos_all = jnp.concatenate([jnp.ones((CTX_LEN, WIDTH), F32), cos_l], axis=0)
    sin_all = jnp.concatenate([jnp.zeros((CTX_LEN, WIDTH), F32), sin_l], axis=0)
    return cos_all, sin_all


def kernel(x, c, ctx, c_ctx, w_ada, b_ada, norm1_g, norm2_g, w_in, w_out, lru_conv_w, lru_conv_b,
           lru_wr, lru_br, lru_wi, lru_bi, lru_lam, da_lam, da_subln_g, sg_norm_g, sg_w, sg_b,
           hg_lb, hg_norm_g, router_w, router_b, moe_w1, moe_w3, moe_w2, final_norm_g):
    nb = x.shape[0]
    assert nb <= CTX_ROW and x.shape[1:] == (SEQ, D_MODEL) and ctx.shape[1:] == (CTX_LEN, D_MODEL)
    xs = jnp.concatenate([ctx, x], axis=1).reshape(nb * SEQ_ALL, D_MODEL)

    cvec = jnp.zeros((ADA_ROWS, D_MODEL), F32).at[:nb].set(c).at[CTX_ROW].set(c_ctx)
    mods = _ada(cvec, w_ada, b_ada).reshape(DEPTH, ADA_ROWS, 6, D_MODEL)

    cos_all, sin_all = _rope_tables()
    lb_cum = jnp.cumsum(jax.nn.softmax(hg_lb.astype(F32), axis=1), axis=1)
    lb_all = lb_cum - lb_cum[:, :1]
    wr_t = router_w.T
    br = router_b.reshape(N_EXPERTS, 1)

    for l in range(DEPTH):
        last = l == DEPTH - 1
        out_rows = SEQ if last else SEQ_ALL
        z = _inproj(xs, norm1_g[l], mods[l], w_in[l].astype(BF16))

        ya = _lru(z, lru_conv_w[l], lru_conv_b[l].reshape(1, WIDTH),
                  jax.vmap(_block_diag)(lru_wr[l]).astype(BF16), lru_br[l],
                  jax.vmap(_block_diag)(lru_wi[l]).astype(BF16), lru_bi[l], lru_lam[l], out_rows)

        lam_init = 0.8 - 0.6 * math.exp(-0.3 * l)
        lf = da_lam[l].astype(F32)
        lam = jnp.exp(jnp.sum(lf[0] * lf[1])) - jnp.exp(jnp.sum(lf[2] * lf[3])) + lam_init
        yb = _attn(z, lam.reshape(1, 1), cos_all, sin_all,
                   jnp.tile(da_subln_g[l], HEADS).reshape(1, WIDTH),
                   with_ctx=not last, out_scale=1.0 - lam_init)

        w_cat = jnp.transpose(sg_w[l], (1, 0, 2)).reshape(SG_CHUNK, HEADS * SG_CHUNK).astype(BF16)
        bias2d = jnp.repeat(sg_b[l].T, HEAD_DIM, axis=1)
        ys = _sgu(z, sg_norm_g[l].reshape(1, WIDTH), w_cat, bias2d, out_rows)

        yh = _hgrn(z, lb_all[:, l], hg_norm_g[l].reshape(1, WIDTH), out_rows)

        xs, h2, gates = _outproj(xs, (ya, yb, ys, yh), w_out[l].astype(BF16), mods[l],
                                 norm2_g[l], wr_t, br, has_ctx=not last)
        xs = _moe(h2, gates, xs, moe_w1[l].astype(BF16), moe_w3[l].astype(BF16),
                  moe_w2[l].astype(BF16), mods[l], final_norm_g, has_ctx=not last,
                  final_norm=last)

    return xs.reshape(nb, SEQ, D_MODEL)
```

```python
import functools
import math

import jax
import jax.numpy as jnp
from jax import lax
from jax.experimental import pallas as pl
from jax.experimental.pallas import tpu as pltpu

F32 = jnp.float32
BF16 = jnp.bfloat16
HIGHEST = lax.Precision.HIGHEST

D_MODEL = 1024
SEQ = 2048
CTX_LEN = 256
SEQ_ALL = CTX_LEN + SEQ
DEPTH = 2
GRID_W = 64
EPS = 1e-6
WIDTH = 256
HEADS = 4
HEAD_DIM = 64
CONV_W = 4
RG_C = 8.0
DA_QD = 32
ROPE_THETA = 10000.0
SG_CHUNK = 128
N_EXPERTS = 16
N_GROUPS = 4
EXP_PER_GROUP = 4
D_EXPERT = 512
IN_COLS = 3072
COL_A_X, COL_A_G = 0, 1
COL_B_Q, COL_B_K, COL_B_V = 2, 3, 4
COL_C_U, COL_C_V = 5, 6
COL_D_Q, COL_D_FF, COL_D_FB, COL_D_I, COL_D_G = 7, 8, 9, 10, 11

ADA_ROWS = 16
CTX_ROW = 8
VMEM_LIMIT = 56 * 1024 * 1024

LRU_BLK = 8
HG_CHUNK = 16
ATT_TQ = 256
DENSE_TM = 768
MOE_R = 512
HP_COLS = D_MODEL + 128
COMBINE_TM = 1152
COMBINE_TM_LATENT = 1024


def _params(*sem):
    return pltpu.CompilerParams(dimension_semantics=sem, vmem_limit_bytes=VMEM_LIMIT)


def _rms(xf, g):
    return xf * lax.rsqrt(jnp.mean(xf * xf, axis=-1, keepdims=True) + EPS) * g


def _sigmoid(x):
    return 1.0 / (1.0 + jnp.exp(-x))


def _silu(x):
    return x * _sigmoid(x)


def _gelu(x):
    return jax.nn.gelu(x)


def _head_ones(dtype):
    r = lax.broadcasted_iota(jnp.int32, (WIDTH, WIDTH), 0) // HEAD_DIM
    c = lax.broadcasted_iota(jnp.int32, (WIDTH, WIDTH), 1) // HEAD_DIM
    return (r == c).astype(dtype)


def _mod_picker(modl_ref, modc_ref, tm, has_ctx):
    ml = modl_ref[0]
    if not has_ctx:
        return lambda r: ml[r:r + 1]
    mc = modc_ref[0]
    row0 = (pl.program_id(0) % (SEQ_ALL // tm)) * tm
    is_ctx = row0 + lax.broadcasted_iota(jnp.int32, (tm, 1), 0) < CTX_LEN
    return lambda r: jnp.where(is_ctx, mc[r:r + 1], ml[r:r + 1])


def _ada_kernel(c_ref, w_ref, b_ref, o_ref):
    s = _silu(c_ref[...]).astype(BF16)
    o_ref[...] = jnp.dot(s, w_ref[...].astype(BF16), preferred_element_type=F32) + b_ref[...]


def _ada(cvec, w_ada, b_ada):
    tn = 1536
    return pl.pallas_call(
        _ada_kernel,
        grid=(DEPTH, 6 * D_MODEL // tn),
        in_specs=[
            pl.BlockSpec((ADA_ROWS, D_MODEL), lambda l, j: (0, 0)),
            pl.BlockSpec((None, D_MODEL, tn), lambda l, j: (l, 0, j)),
            pl.BlockSpec((None, 1, tn), lambda l, j: (l, 0, j)),
        ],
        out_specs=pl.BlockSpec((None, ADA_ROWS, tn), lambda l, j: (l, 0, j)),
        out_shape=jax.ShapeDtypeStruct((DEPTH, ADA_ROWS, 6 * D_MODEL), F32),
        compiler_params=_params("arbitrary", "arbitrary"),
        name="ada",
    )(cvec, w_ada, b_ada.reshape(DEPTH, 1, 6 * D_MODEL))


def _inproj_kernel(x_ref, g_ref, modl_ref, modc_ref, w_ref, z_ref, *, tm):
    pick = _mod_picker(modl_ref, modc_ref, tm, True)
    h = _rms(x_ref[...], g_ref[...]) * (1.0 + pick(1)) + pick(0)
    z_ref[...] = jnp.dot(h.astype(BF16), w_ref[...], preferred_element_type=F32)


def _mod_specs(tiles_per_batch):
    return [
        pl.BlockSpec((1, 6, D_MODEL), lambda i, *_: (i // tiles_per_batch, 0, 0)),
        pl.BlockSpec((1, 6, D_MODEL), lambda i, *_: (CTX_ROW, 0, 0)),
    ]


def _inproj(x, g, mod, w_bf16):
    n = x.shape[0]
    tm = DENSE_TM
    return pl.pallas_call(
        functools.partial(_inproj_kernel, tm=tm),
        grid=(n // tm,),
        in_specs=[
            pl.BlockSpec((tm, D_MODEL), lambda i: (i, 0)),
            pl.BlockSpec((1, D_MODEL), lambda i: (0, 0)),
            *_mod_specs(SEQ_ALL // tm),
            pl.BlockSpec((D_MODEL, IN_COLS), lambda i: (0, 0)),
        ],
        out_specs=pl.BlockSpec((tm, IN_COLS), lambda i: (i, 0)),
        out_shape=jax.ShapeDtypeStruct((n, IN_COLS), F32),
        compiler_params=_params("arbitrary"),
        name="inproj",
    )(x, g.reshape(1, D_MODEL), mod, mod, w_bf16)


def _lru_kernel(x_ref, gate_ref, cw_ref, cb_ref, wr_ref, br_ref, wi_ref, bi_ref, lam_ref,
                y_ref, a_s, b_s, h_s, *, out_rows):
    x = x_ref[...]
    rows = lax.broadcasted_iota(jnp.int32, (SEQ_ALL, 1), 0)
    seg = rows < CTX_LEN
    u = jnp.zeros_like(x)
    for j in range(CONV_W):
        off = j - CONV_W // 2
        xs = x if off == 0 else pltpu.roll(x, (-off) % SEQ_ALL, 0)
        src = rows + off
        ok = (src >= 0) & (src < SEQ_ALL) & ((src < CTX_LEN) == seg)
        u = u + jnp.where(ok, xs, 0.0) * cw_ref[j:j + 1, :]
    u = u + cb_ref[...]
    ub = u.astype(BF16)
    for d in range(2):
        r = _sigmoid(jnp.dot(ub, wr_ref[d], preferred_element_type=F32) + br_ref[d:d + 1, :])
        i = _sigmoid(jnp.dot(ub, wi_ref[d], preferred_element_type=F32) + bi_ref[d:d + 1, :])
        nl = -lam_ref[d:d + 1, :]
        softplus = jnp.maximum(nl, 0.0) + jnp.log(1.0 + jnp.exp(-jnp.abs(nl)))
        log_a = -RG_C * r * softplus
        a_s[d] = jnp.exp(log_a)
        b_s[d] = jnp.sqrt(1.0 - jnp.exp(2.0 * log_a)) * i * u

    n_blk = SEQ_ALL // LRU_BLK
    n_ctx_blk = CTX_LEN // LRU_BLK
    sub = lax.broadcasted_iota(jnp.int32, (LRU_BLK, 1), 0)

    def block_scan(a, b, reverse):
        s = 1
        while s < LRU_BLK:
            if reverse:
                a_sh = pltpu.roll(a, LRU_BLK - s, 0)
                b_sh = pltpu.roll(b, LRU_BLK - s, 0)
                ok = sub < LRU_BLK - s
            else:
                a_sh = pltpu.roll(a, s, 0)
                b_sh = pltpu.roll(b, s, 0)
                ok = sub >= s
            b = jnp.where(ok, a * b_sh + b, b)
            a = jnp.where(ok, a * a_sh, a)
            s *= 2
        return a, b

    def body(n, carry):
        hf, hb = carry
        rf = pl.multiple_of(n * LRU_BLK, LRU_BLK)
        af, bf = block_scan(a_s[0, pl.ds(rf, LRU_BLK), :], b_s[0, pl.ds(rf, LRU_BLK), :], False)
        h = af * hf + bf
        h_s[0, pl.ds(rf, LRU_BLK), :] = h
        hf = h[LRU_BLK - 1:LRU_BLK, :]
        nb = jnp.where(n < n_ctx_blk, n_ctx_blk - 1 - n, n_blk + n_ctx_blk - 1 - n)
        rb = pl.multiple_of(nb * LRU_BLK, LRU_BLK)
        ab, bb = block_scan(a_s[1, pl.ds(rb, LRU_BLK), :], b_s[1, pl.ds(rb, LRU_BLK), :], True)
        h = ab * hb + bb
        h_s[1, pl.ds(rb, LRU_BLK), :] = h
        hb = h[0:1, :]
        return hf, hb

    zero = jnp.zeros((1, WIDTH), F32)
    lax.fori_loop(0, n_blk, body, (zero, zero))
    first = SEQ_ALL - out_rows
    y_ref[...] = (h_s[0, first:, :] + h_s[1, first:, :]) * _gelu(gate_ref[first:, :])


def _lru(z, cw, cb, wr_bd, br, wi_bd, bi, lam, out_rows):
    nb = z.shape[0] // SEQ_ALL
    full = lambda shape: pl.BlockSpec(shape, lambda b: (0,) * len(shape))
    return pl.pallas_call(
        functools.partial(_lru_kernel, out_rows=out_rows),
        grid=(nb,),
        in_specs=[
            pl.BlockSpec((SEQ_ALL, WIDTH), lambda b: (b, COL_A_X)),
            pl.BlockSpec((SEQ_ALL, WIDTH), lambda b: (b, COL_A_G)),
            full((CONV_W, WIDTH)), full((1, WIDTH)),
            full((2, WIDTH, WIDTH)), full((2, WIDTH)),
            full((2, WIDTH, WIDTH)), full((2, WIDTH)),
            full((2, WIDTH)),
        ],
        out_specs=pl.BlockSpec((out_rows, WIDTH), lambda b: (b, 0)),
        out_shape=jax.ShapeDtypeStruct((nb * out_rows, WIDTH), F32),
        scratch_shapes=[
            pltpu.VMEM((2, SEQ_ALL, WIDTH), F32),
            pltpu.VMEM((2, SEQ_ALL, WIDTH), F32),
            pltpu.VMEM((2, SEQ_ALL, WIDTH), F32),
        ],
        compiler_params=_params("arbitrary"),
        name="rglru",
    )(z, z, cw, cb, wr_bd, br, wi_bd, bi, lam)


def _rope(x, cos, sin_signed):
    lane = lax.broadcasted_iota(jnp.int32, (1, WIDTH), 1)
    first_half = (lane % 16) < 8
    partner = jnp.where(first_half, pltpu.roll(x, WIDTH - 8, 1), pltpu.roll(x, 8, 1))
    return x * cos + partner * sin_signed


def _attn_kernel(lam_ref, q_ref, k_ref, v_ref, cos_ref, sin_ref, g_ref, y_ref, kt_s, v_s,
                 *, with_ctx, out_scale):
    j = pl.program_id(1)

    @pl.when(j == 0)
    def _():
        kr = _rope(k_ref[...], cos_ref[...], sin_ref[...])
        kt_s[...] = kr.T.astype(BF16)
        v = v_ref[...]
        for h in range(HEADS):
            v_s[h] = v[:, h * HEAD_DIM:(h + 1) * HEAD_DIM].astype(BF16)

    lam = lam_ref[0, 0]
    tile = j if with_ctx else j + 1
    row0 = pl.multiple_of(tile * ATT_TQ, ATT_TQ)
    q = _rope(q_ref[...], cos_ref[pl.ds(row0, ATT_TQ), :], sin_ref[pl.ds(row0, ATT_TQ), :])
    q = q * (DA_QD ** -0.5)

    def attend(n_keys):
        outs = []
        for h in range(HEADS):
            probs = []
            for n in range(2):
                c0 = h * HEAD_DIM + n * DA_QD
                s = jnp.dot(q[:, c0:c0 + DA_QD].astype(BF16), kt_s[c0:c0 + DA_QD, 0:n_keys],
                            preferred_element_type=F32)
                p = jnp.exp(s - jnp.max(s, axis=-1, keepdims=True))
                probs.append(p * (1.0 / jnp.sum(p, axis=-1, keepdims=True)))
            pd = (probs[0] - lam * probs[1]).astype(BF16)
            o = jnp.dot(pd, v_s[h, 0:n_keys, :], preferred_element_type=F32)
            o = o * lax.rsqrt(jnp.mean(o * o, axis=-1, keepdims=True) + EPS)
            outs.append(o)
        y_ref[...] = jnp.concatenate(outs, axis=-1) * g_ref[...] * out_scale

    if with_ctx:
        @pl.when(j == 0)
        def _():
            attend(CTX_LEN)

        @pl.when(j > 0)
        def _():
            attend(SEQ_ALL)
    else:
        attend(SEQ_ALL)


def _attn(z, lam, cos, sin_signed, sub_g4, with_ctx, out_scale):
    nb = z.shape[0] // SEQ_ALL
    tiles = SEQ_ALL // ATT_TQ
    nq = tiles if with_ctx else tiles - 1
    first = 0 if with_ctx else 1
    return pl.pallas_call(
        functools.partial(_attn_kernel, with_ctx=with_ctx, out_scale=out_scale),
        grid=(nb, nq),
        in_specs=[
            pl.BlockSpec(memory_space=pltpu.SMEM),
            pl.BlockSpec((ATT_TQ, WIDTH), lambda b, j: (b * tiles + j + first, COL_B_Q)),
            pl.BlockSpec((SEQ_ALL, WIDTH), lambda b, j: (b, COL_B_K)),
            pl.BlockSpec((SEQ_ALL, WIDTH), lambda b, j: (b, COL_B_V)),
            pl.BlockSpec((SEQ_ALL, WIDTH), lambda b, j: (0, 0)),
            pl.BlockSpec((SEQ_ALL, WIDTH), lambda b, j: (0, 0)),
            pl.BlockSpec((1, WIDTH), lambda b, j: (0, 0)),
        ],
        out_specs=pl.BlockSpec((ATT_TQ, WIDTH), lambda b, j: (b * nq + j, 0)),
        out_shape=jax.ShapeDtypeStruct((nb * nq * ATT_TQ, WIDTH), F32),
        scratch_shapes=[
            pltpu.VMEM((WIDTH, SEQ_ALL), BF16),
            pltpu.VMEM((HEADS, SEQ_ALL, HEAD_DIM), BF16),
        ],
        compiler_params=_params("arbitrary", "arbitrary"),
        name="diffattn",
    )(lam, z, z, z, cos, sin_signed, sub_g4)


def _sgu_kernel(u_ref, v_ref, g_ref, w_ref, b_ref, y_ref, *, out_rows):
    row_head = lax.broadcasted_iota(jnp.int32, (HEADS * SG_CHUNK, WIDTH), 0) // SG_CHUNK
    col_head = lax.broadcasted_iota(jnp.int32, (HEADS * SG_CHUNK, WIDTH), 1) // HEAD_DIM
    head_mask = row_head == col_head
    w = w_ref[...]
    bias = b_ref[...]
    g = g_ref[...]
    first = SEQ_ALL - out_rows
    for n in range(first // SG_CHUNK, SEQ_ALL // SG_CHUNK):
        rows = slice(n * SG_CHUNK, (n + 1) * SG_CHUNK)
        vn = _rms(_gelu(v_ref[rows, :]), g).astype(BF16)
        stacked = jnp.where(head_mask, jnp.concatenate([vn] * HEADS, axis=0), jnp.zeros((), BF16))
        vm = jnp.dot(w, stacked, preferred_element_type=F32) + bias
        y_ref[n * SG_CHUNK - first:(n + 1) * SG_CHUNK - first, :] = _gelu(u_ref[rows, :]) * vm


def _sgu(z, norm_g, w_cat, bias2d, out_rows):
    nb = z.shape[0] // SEQ_ALL
    return pl.pallas_call(
        functools.partial(_sgu_kernel, out_rows=out_rows),
        grid=(nb,),
        in_specs=[
            pl.BlockSpec((SEQ_ALL, WIDTH), lambda b: (b, COL_C_U)),
            pl.BlockSpec((SEQ_ALL, WIDTH), lambda b: (b, COL_C_V)),
            pl.BlockSpec((1, WIDTH), lambda b: (0, 0)),
            pl.BlockSpec((SG_CHUNK, HEADS * SG_CHUNK), lambda b: (0, 0)),
            pl.BlockSpec((SG_CHUNK, WIDTH), lambda b: (0, 0)),
        ],
        out_specs=pl.BlockSpec((out_rows, WIDTH), lambda b: (b, 0)),
        out_shape=jax.ShapeDtypeStruct((nb * out_rows, WIDTH), F32),
        compiler_params=_params("arbitrary"),
        name="sgu",
    )(z, z, norm_g, w_cat, bias2d)


def _hgrn_kernel(q_ref, ff_ref, fb_ref, i_ref, gate_ref, lb_ref, gn_ref, y_ref,
                 b_s, k_s, o_s, st_s, *, out_rows):
    c = HG_CHUNK
    pre = 128
    ones_bf = _head_ones(BF16)
    ones_f = _head_ones(F32)
    pr = lax.broadcasted_iota(jnp.int32, (pre, pre), 0)
    pc = lax.broadcasted_iota(jnp.int32, (pre, pre), 1)
    same = (pr // c) == (pc // c)
    tri = ((same & (pc <= pr)).astype(F32), (same & (pc >= pr)).astype(F32))

    for d, f_ref in enumerate((ff_ref, fb_ref)):
        lb = lb_ref[d:d + 1, :]
        for n in range(SEQ_ALL // pre):
            rows = slice(n * pre, (n + 1) * pre)
            f = lb + (1.0 - lb) * _sigmoid(f_ref[rows, :])
            k_s[d, rows, :] = 1.0 - f
            b_s[d, rows, :] = jnp.dot(tri[d], jnp.log(f), precision=HIGHEST,
                                      preferred_element_type=F32)
    st_s[...] = jnp.zeros_like(st_s)

    n_chunks = SEQ_ALL // c
    n_ctx = CTX_LEN // c
    sub = lax.broadcasted_iota(jnp.int32, (c, 1), 0)
    scale = HEAD_DIM ** -0.5

    def one_direction(d, chunk):
        r0 = pl.multiple_of(chunk * c, c)
        q = q_ref[pl.ds(r0, c), :] * scale
        v = i_ref[pl.ds(r0, c), :]
        b = b_s[d, pl.ds(r0, c), :]
        k = k_s[d, pl.ds(r0, c), :]
        tot = b[c - 1:c, :] if d == 0 else b[0:1, :]
        st = st_s[d]
        o = lax.dot_general((q * jnp.exp(b)).astype(BF16), st.astype(BF16),
                            (((1,), (1,)), ((), ())), preferred_element_type=F32)
        parts = []
        for s in range(c):
            ok = (sub >= s) if d == 0 else (sub <= s)
            diff = jnp.where(ok, b - b[s:s + 1, :], -jnp.inf)
            parts.append((q * k[s:s + 1, :] * jnp.exp(diff)).astype(BF16))
        att = jnp.dot(jnp.concatenate(parts, axis=0), ones_bf, preferred_element_type=F32)
        for s in range(c):
            o = o + att[s * c:(s + 1) * c, :] * v[s:s + 1, :]
        o_s[d, pl.ds(r0, c), :] = o
        ke = (k * jnp.exp(tot - b)).astype(BF16)
        upd = lax.dot_general(v.astype(BF16), ke, (((0,), (0,)), ((), ())),
                              preferred_element_type=F32)
        st_s[d] = st * jnp.exp(tot) + upd * ones_f

    def body(n, carry):
        one_direction(0, n)
        one_direction(1, jnp.where(n < n_ctx, n_ctx - 1 - n, n_chunks + n_ctx - 1 - n))
        return carry

    lax.fori_loop(0, n_chunks, body, 0)

    first = SEQ_ALL - out_rows
    o = o_s[0, first:, :] + o_s[1, first:, :]
    ms = jnp.dot(o * o, ones_f * (1.0 / HEAD_DIM), precision=HIGHEST, preferred_element_type=F32)
    y_ref[...] = o * lax.rsqrt(ms + EPS) * gn_ref[...] * _silu(gate_ref[first:, :])


def _hgrn(z, lb, gn, out_rows):
    nb = z.shape[0] // SEQ_ALL
    col = lambda cidx: pl.BlockSpec((SEQ_ALL, WIDTH), lambda b: (b, cidx))
    return pl.pallas_call(
        functools.partial(_hgrn_kernel, out_rows=out_rows),
        grid=(nb,),
        in_specs=[
            col(COL_D_Q), col(COL_D_FF), col(COL_D_FB), col(COL_D_I), col(COL_D_G),
            pl.BlockSpec((2, WIDTH), lambda b: (0, 0)),
            pl.BlockSpec((1, WIDTH), lambda b: (0, 0)),
        ],
        out_specs=pl.BlockSpec((out_rows, WIDTH), lambda b: (b, 0)),
        out_shape=jax.ShapeDtypeStruct((nb * out_rows, WIDTH), F32),
        scratch_shapes=[
            pltpu.VMEM((2, SEQ_ALL, WIDTH), F32),
            pltpu.VMEM((2, SEQ_ALL, WIDTH), F32),
            pltpu.VMEM((2, SEQ_ALL, WIDTH), F32),
            pltpu.VMEM((2, WIDTH, WIDTH), F32),
        ],
        compiler_params=_params("arbitrary"),
        name="hgrn2",
    )(z, z, z, z, z, lb, gn)


def _route(logits):
    rows = [logits[e:e + 1, :] for e in range(N_EXPERTS)]
    m = functools.reduce(jnp.maximum, rows)
    ex = [jnp.exp(r - m) for r in rows]
    inv = 1.0 / functools.reduce(jnp.add, ex)
    sc = [e * inv for e in ex]
    g_score = []
    for g in range(N_GROUPS):
        grp = sc[g * EXP_PER_GROUP:(g + 1) * EXP_PER_GROUP]
        pairs = [grp[a] + grp[b] for a in range(EXP_PER_GROUP) for b in range(a + 1, EXP_PER_GROUP)]
        g_score.append(functools.reduce(jnp.maximum, pairs))
    gate4 = [jnp.zeros_like(sc[0]) for _ in range(EXP_PER_GROUP)]
    onehot = []
    for g in range(N_GROUPS):
        g_ok = None
        for o in range(N_GROUPS):
            if o == g:
                continue
            t = (g_score[g] > g_score[o]) if o < g else (g_score[g] >= g_score[o])
            g_ok = t if g_ok is None else (g_ok & t)
        grp = sc[g * EXP_PER_GROUP:(g + 1) * EXP_PER_GROUP]
        picked = []
        for a in range(EXP_PER_GROUP):
            beaten = jnp.zeros_like(grp[a])
            for o in range(EXP_PER_GROUP):
                if o == a:
                    continue
                t = (grp[o] >= grp[a]) if o < a else (grp[o] > grp[a])
                beaten = beaten + jnp.where(t, 1.0, 0.0)
            picked.append(jnp.where((beaten < 1.5) & g_ok, grp[a], 0.0))
        denom = functools.reduce(jnp.add, picked)
        denom = jnp.where(g_ok, denom, 1.0)
        gate4 = [acc + p / denom for acc, p in zip(gate4, picked)]
        onehot.append(jnp.where(g_ok, 1.0, 0.0))
    return gate4, onehot


def _outproj_kernel(x_ref, ya_ref, yb_ref, ys_ref, yh_ref, w_ref, modl_ref, modc_ref, g_ref,
                    wr_ref, br_ref, xo_ref, hp_ref, pos_ref, cnt_ref, cnt_s, *, tm, has_ctx, cap):
    @pl.when(pl.program_id(0) == 0)
    def _():
        cnt_s[...] = jnp.zeros_like(cnt_s)

    pick = _mod_picker(modl_ref, modc_ref, tm, has_ctx)
    acc = jnp.zeros((tm, D_MODEL), F32)
    for kblk, y_ref in enumerate((ya_ref, yb_ref, ys_ref, yh_ref)):
        acc = acc + jnp.dot(y_ref[...].astype(BF16), w_ref[kblk * WIDTH:(kblk + 1) * WIDTH, :],
                            preferred_element_type=F32)
    x = x_ref[...] + pick(2) * acc
    xo_ref[...] = x
    h = _rms(x, g_ref[...]) * (1.0 + pick(4)) + pick(3)
    logits = lax.dot_general(wr_ref[...], h, (((1,), (1,)), ((), ())), precision=HIGHEST,
                             preferred_element_type=F32) + br_ref[...]
    gate4, onehot = _route(logits)
    hp_ref[:, 0:D_MODEL] = h
    gate_rows = jnp.concatenate(gate4 + [jnp.zeros((128 - EXP_PER_GROUP, tm), F32)], axis=0)
    hp_ref[:, D_MODEL:] = gate_rows.T
    sel = jnp.concatenate(onehot + [jnp.zeros((8 - N_GROUPS, tm), F32)], axis=0)
    before = (lax.broadcasted_iota(jnp.int32, (tm, tm), 0)
              <= lax.broadcasted_iota(jnp.int32, (tm, tm), 1)).astype(BF16)
    seen = jnp.dot(sel.astype(BF16), before, preferred_element_type=F32)
    carried = cnt_s[:, 0:1]
    base = lax.broadcasted_iota(jnp.int32, (8, 1), 0).astype(F32) * float(cap)
    slot = jnp.sum(sel * (base + carried + seen - 1.0), axis=0, keepdims=True)
    pos_ref[0] = slot.astype(jnp.int32)
    cnt_s[...] = cnt_s[...] + jnp.sum(sel, axis=1, keepdims=True)
    cnt_ref[...] = cnt_s[...].astype(jnp.int32)


def _outproj(x, ys, w_bf16, mod, g2, wr_t, br, has_ctx):
    n = ys[0].shape[0]
    if has_ctx:
        tm = DENSE_TM
        tpb = SEQ_ALL // tm
        x_spec = pl.BlockSpec((tm, D_MODEL), lambda i: (i, 0))
    else:
        tm = CTX_LEN
        tpb = SEQ // tm
        x_spec = pl.BlockSpec((tm, D_MODEL),
                              lambda i: ((i // tpb) * (SEQ_ALL // tm) + 1 + i % tpb, 0))
    tile = lambda w: pl.BlockSpec((tm, w), lambda i: (i, 0))
    return pl.pallas_call(
        functools.partial(_outproj_kernel, tm=tm, has_ctx=has_ctx, cap=n),
        grid=(n // tm,),
        in_specs=[
            x_spec, tile(WIDTH), tile(WIDTH), tile(WIDTH), tile(WIDTH),
            pl.BlockSpec((D_MODEL, D_MODEL), lambda i: (0, 0)),
            *_mod_specs(tpb),
            pl.BlockSpec((1, D_MODEL), lambda i: (0, 0)),
            pl.BlockSpec((N_EXPERTS, D_MODEL), lambda i: (0, 0)),
            pl.BlockSpec((N_EXPERTS, 1), lambda i: (0, 0)),
        ],
        out_specs=[
            tile(D_MODEL), tile(HP_COLS),
            pl.BlockSpec((1, 1, tm), lambda i: (i, 0, 0)),
            pl.BlockSpec((8, 128), lambda i: (0, 0)),
        ],
        out_shape=[
            jax.ShapeDtypeStruct((n, D_MODEL), F32),
            jax.ShapeDtypeStruct((n, HP_COLS), F32),
            jax.ShapeDtypeStruct((n // tm, 1, tm), jnp.int32),
            jax.ShapeDtypeStruct((8, 128), jnp.int32),
        ],
        scratch_shapes=[pltpu.VMEM((8, 128), F32)],
        compiler_params=_params("arbitrary"),
        name="outproj_router",
    )(x, *ys, w_bf16, mod, mod, g2.reshape(1, D_MODEL), wr_t, br)


def _row_copies_wait(src_rows, dst_rows, sem):
    pltpu.make_async_copy(src_rows, dst_rows, sem).wait()


def _dispatch_kernel(pos_ref, hp_hbm, hs_hbm, sem, *, tm):
    row0 = pl.program_id(0) * tm

    def body(t, carry):
        pltpu.make_async_copy(hp_hbm.at[pl.ds(row0 + t, 1)],
                              hs_hbm.at[pl.ds(pos_ref[0, 0, t], 1)], sem).start()
        return carry

    lax.fori_loop(0, tm, body, 0, unroll=8)
    _row_copies_wait(hp_hbm.at[pl.ds(0, tm)], hs_hbm.at[pl.ds(0, tm)], sem)


def _dispatch(pos, hp, tm):
    n = hp.shape[0]
    return pl.pallas_call(
        functools.partial(_dispatch_kernel, tm=tm),
        grid=(n // tm,),
        in_specs=[
            pl.BlockSpec((1, 1, tm), lambda i: (i, 0, 0), memory_space=pltpu.SMEM),
            pl.BlockSpec(memory_space=pl.ANY),
        ],
        out_specs=pl.BlockSpec(memory_space=pl.ANY),
        out_shape=jax.ShapeDtypeStruct((N_GROUPS * n, HP_COLS), F32),
        scratch_shapes=[pltpu.SemaphoreType.DMA],
        compiler_params=_params("arbitrary"),
        name="moe_dispatch",
    )(pos.reshape(n // tm, 1, tm), hp)


def _experts_kernel(row_ref, grp_ref, valid_ref, hs_ref, w1_ref, w3_ref, w2_ref, ys_ref):
    n_valid = valid_ref[pl.program_id(0)]

    @pl.when(n_valid > 0)
    def _():
        w = hs_ref[...]
        ok = lax.broadcasted_iota(jnp.int32, (MOE_R, 1), 0) < n_valid
        h = jnp.where(ok, w[:, 0:D_MODEL], 0.0).astype(BF16)
        gates = jnp.where(ok, w[:, D_MODEL:], 0.0)
        acc = jnp.zeros((MOE_R, D_MODEL), F32)
        for e in range(EXP_PER_GROUP):
            a = jnp.dot(h, w1_ref[e], preferred_element_type=F32)
            b = jnp.dot(h, w3_ref[e], preferred_element_type=F32)
            he = (_silu(a) * b * gates[:, e:e + 1]).astype(BF16)
            acc = acc + jnp.dot(he, w2_ref[e], preferred_element_type=F32)
        ys_ref[...] = acc


def _experts(hs, blk_row, blk_grp, blk_valid, w1, w3, w2):
    n_blk = blk_row.shape[0]
    grid_spec = pltpu.PrefetchScalarGridSpec(
        num_scalar_prefetch=3,
        grid=(n_blk,),
        in_specs=[
            pl.BlockSpec((MOE_R, HP_COLS), lambda k, row, grp, valid: (row[k], 0)),
            pl.BlockSpec((EXP_PER_GROUP, D_MODEL, D_EXPERT), lambda k, row, grp, valid: (grp[k], 0, 0)),
            pl.BlockSpec((EXP_PER_GROUP, D_MODEL, D_EXPERT), lambda k, row, grp, valid: (grp[k], 0, 0)),
            pl.BlockSpec((EXP_PER_GROUP, D_EXPERT, D_MODEL), lambda k, row, grp, valid: (grp[k], 0, 0)),
        ],
        out_specs=pl.BlockSpec((MOE_R, D_MODEL), lambda k, row, grp, valid: (row[k], 0)),
    )
    return pl.pallas_call(
        _experts_kernel,
        grid_spec=grid_spec,
        out_shape=jax.ShapeDtypeStruct((hs.shape[0], D_MODEL), F32),
        compiler_params=_params("arbitrary"),
        name="moe_experts",
    )(blk_row, blk_grp, blk_valid, hs, w1, w3, w2)


def _combine_kernel(pos_ref, x_ref, modl_ref, modc_ref, fg_ref, ys_hbm, o_ref, buf, sem,
                    *, tm, has_ctx, final_norm):
    def body(t, carry):
        pltpu.make_async_copy(ys_hbm.at[pl.ds(pos_ref[0, 0, t], 1)], buf.at[pl.ds(t, 1)],
                              sem).start()
        return carry

    lax.fori_loop(0, tm, body, 0, unroll=8)
    _row_copies_wait(ys_hbm.at[pl.ds(0, tm)], buf, sem)
    pick = _mod_picker(modl_ref, modc_ref, tm, has_ctx)
    x = x_ref[...] + pick(5) * buf[...]
    o_ref[...] = _rms(x, fg_ref[...]) if final_norm else x


def _combine(pos, x, ys, mod, final_g, has_ctx, final_norm):
    n = x.shape[0]
    tm = COMBINE_TM if has_ctx else COMBINE_TM_LATENT
    tpb = (SEQ_ALL if has_ctx else SEQ) // tm
    return pl.pallas_call(
        functools.partial(_combine_kernel, tm=tm, has_ctx=has_ctx, final_norm=final_norm),
        grid=(n // tm,),
        in_specs=[
            pl.BlockSpec((1, 1, tm), lambda i: (i, 0, 0), memory_space=pltpu.SMEM),
            pl.BlockSpec((tm, D_MODEL), lambda i: (i, 0)),
            *_mod_specs(tpb),
            pl.BlockSpec((1, D_MODEL), lambda i: (0, 0)),
            pl.BlockSpec(memory_space=pl.ANY),
        ],
        out_specs=pl.BlockSpec((tm, D_MODEL), lambda i: (i, 0)),
        out_shape=jax.ShapeDtypeStruct((n, D_MODEL), F32),
        scratch_shapes=[pltpu.VMEM((tm, D_MODEL), F32), pltpu.SemaphoreType.DMA],
        compiler_params=_params("arbitrary"),
        name="moe_combine",
    )(pos.reshape(n // tm, 1, tm), x, mod, mod, final_g.reshape(1, D_MODEL), ys)


def _block_table(counts, n):
    n_blk = n // MOE_R + N_GROUPS
    n_g = counts[:N_GROUPS, 0]
    nb = (n_g + MOE_R - 1) // MOE_R
    ends = jnp.cumsum(nb)
    starts = ends - nb
    k = jnp.arange(n_blk, dtype=jnp.int32)
    kk = jnp.minimum(k, ends[-1] - 1)
    g = jnp.sum((kk[:, None] >= ends[None, :]).astype(jnp.int32), axis=1)
    j = kk - starts[g]
    row = g * (n // MOE_R) + j
    valid = jnp.where(k < ends[-1], jnp.clip(n_g[g] - j * MOE_R, 0, MOE_R), 0)
    return row.astype(jnp.int32), g.astype(jnp.int32), valid.astype(jnp.int32)


def _moe(hp, pos, counts, x, w1, w3, w2, mod, final_g, has_ctx, final_norm):
    n = x.shape[0]
    assert n % MOE_R == 0
    pos = pos.reshape(n)
    hs = _dispatch(pos, hp, COMBINE_TM if has_ctx else COMBINE_TM_LATENT)
    ys = _experts(hs, *_block_table(counts, n), w1, w3, w2)
    return _combine(pos, x, ys, mod, final_g, has_ctx, final_norm)


def _block_diag(w):
    eye = jnp.eye(HEADS, dtype=w.dtype)
    return jnp.einsum('hij,hg->higj', w, eye).reshape(WIDTH, WIDTH)


def _rope_tables():
    rows = SEQ // GRID_W
    row_ids = jnp.repeat(jnp.arange(rows, dtype=F32), GRID_W)
    col_ids = jnp.tile(jnp.arange(GRID_W, dtype=F32), rows)
    n_freq = DA_QD // 4
    freqs = ROPE_THETA ** (-jnp.arange(n_freq, dtype=F32) / n_freq)
    ang = jnp.stack([row_ids[:, None] * freqs, col_ids[:, None] * freqs], axis=1)
    cos = jnp.cos(ang)
    sin = jnp.sin(ang)
    cos_l = jnp.broadcast_to(cos[:, None, None, :, None, :], (SEQ, HEADS, 2, 2, 2, n_freq))
    sin_l = jnp.broadcast_to(sin[:, None, None, :, None, :], (SEQ, HEADS, 2, 2, 2, n_freq))
    sign = jnp.array([-1.0, 1.0], F32)[None, None, None, None, :, None]
    cos_l = cos_l.reshape(SEQ, WIDTH)
    sin_l = (sin_l * sign).reshape(SEQ, WIDTH)
    cos_all = jnp.concatenate([jnp.ones((CTX_LEN, WIDTH), F32), cos_l], axis=0)
    sin_all = jnp.concatenate([jnp.zeros((CTX_LEN, WIDTH), F32), sin_l], axis=0)
    return cos_all, sin_all


def kernel(x, c, ctx, c_ctx, w_ada, b_ada, norm1_g, norm2_g, w_in, w_out, lru_conv_w, lru_conv_b,
           lru_wr, lru_br, lru_wi, lru_bi, lru_lam, da_lam, da_subln_g, sg_norm_g, sg_w, sg_b,
           hg_lb, hg_norm_g, router_w, router_b, moe_w1, moe_w3, moe_w2, final_norm_g):
    nb = x.shape[0]
    assert nb <= CTX_ROW and x.shape[1:] == (SEQ, D_MODEL) and ctx.shape[1:] == (CTX_LEN, D_MODEL)
    xs = jnp.concatenate([ctx, x], axis=1).reshape(nb * SEQ_ALL, D_MODEL)

    cvec = jnp.zeros((ADA_ROWS, D_MODEL), F32).at[:nb].set(c).at[CTX_ROW].set(c_ctx)
    mods = _ada(cvec, w_ada, b_ada).reshape(DEPTH, ADA_ROWS, 6, D_MODEL)

    cos_all, sin_all = _rope_tables()
    lb_cum = jnp.cumsum(jax.nn.softmax(hg_lb.astype(F32), axis=1), axis=1)
    lb_all = lb_cum - lb_cum[:, :1]
    wr_t = router_w.T
    br = router_b.reshape(N_EXPERTS, 1)

    for l in range(DEPTH):
        last = l == DEPTH - 1
        out_rows = SEQ if last else SEQ_ALL
        z = _inproj(xs, norm1_g[l], mods[l], w_in[l].astype(BF16))

        ya = _lru(z, lru_conv_w[l], lru_conv_b[l].reshape(1, WIDTH),
                  jax.vmap(_block_diag)(lru_wr[l]).astype(BF16), lru_br[l],
                  jax.vmap(_block_diag)(lru_wi[l]).astype(BF16), lru_bi[l], lru_lam[l], out_rows)

        lam_init = 0.8 - 0.6 * math.exp(-0.3 * l)
        lf = da_lam[l].astype(F32)
        lam = jnp.exp(jnp.sum(lf[0] * lf[1])) - jnp.exp(jnp.sum(lf[2] * lf[3])) + lam_init
        yb = _attn(z, lam.reshape(1, 1), cos_all, sin_all,
                   jnp.tile(da_subln_g[l], HEADS).reshape(1, WIDTH),
                   with_ctx=not last, out_scale=1.0 - lam_init)

        w_cat = jnp.transpose(sg_w[l], (1, 0, 2)).reshape(SG_CHUNK, HEADS * SG_CHUNK).astype(BF16)
        bias2d = jnp.repeat(sg_b[l].T, HEAD_DIM, axis=1)
        ys = _sgu(z, sg_norm_g[l].reshape(1, WIDTH), w_cat, bias2d, out_rows)

        yh = _hgrn(z, lb_all[:, l], hg_norm_g[l].reshape(1, WIDTH), out_rows)

        xs, hp, pos, counts = _outproj(xs, (ya, yb, ys, yh), w_out[l].astype(BF16), mods[l],
                                       norm2_g[l], wr_t, br, has_ctx=not last)
        xs = _moe(hp, pos, counts, xs, moe_w1[l].astype(BF16), moe_w3[l].astype(BF16),
                  moe_w2[l].astype(BF16), mods[l], final_norm_g, has_ctx=not last,
                  final_norm=last)

    return xs.reshape(nb, SEQ, D_MODEL)
```

```python
import functools
import math

import jax
import jax.numpy as jnp
from jax import lax
from jax.experimental import pallas as pl
from jax.experimental.pallas import tpu as pltpu

F32 = jnp.float32
BF16 = jnp.bfloat16

D_MODEL = 1024
SEQ = 2048
CTX_LEN = 256
SEQ_ALL = CTX_LEN + SEQ
DEPTH = 2
GRID_W = 64
EPS = 1e-6
WIDTH = 256
HEADS = 4
HEAD_DIM = 64
CONV_W = 4
RG_C = 8.0
DA_QD = 32
ROPE_THETA = 10000.0
SG_CHUNK = 128
N_EXPERTS = 16
N_GROUPS = 4
EXP_PER_GROUP = 4
D_EXPERT = 512
IN_COLS = 3072
COL_A_X, COL_A_G = 0, 1
COL_B_Q, COL_B_K, COL_B_V = 2, 3, 4
COL_C_U, COL_C_V = 5, 6
COL_D_Q, COL_D_FF, COL_D_FB, COL_D_I, COL_D_G = 7, 8, 9, 10, 11

ADA_ROWS = 16
CTX_ROW = 8
VMEM_LIMIT = 56 * 1024 * 1024

LRU_BLK = 8
HG_CHUNK = 16
ATT_TQ = 256
DENSE_TM = 768
MOE_R = 512
HP_COLS = D_MODEL + 128
DISPATCH_TM = MOE_R
COMBINE_TM = 1152
COMBINE_TM_LATENT = 1024


def _params(*sem):
    return pltpu.CompilerParams(dimension_semantics=sem, vmem_limit_bytes=VMEM_LIMIT)


def _rms(xf, g):
    return xf * lax.rsqrt(jnp.mean(xf * xf, axis=-1, keepdims=True) + EPS) * g


def _sigmoid(x):
    return 1.0 / (1.0 + jnp.exp(-x))


def _silu(x):
    return x * _sigmoid(x)


def _gelu(x):
    return jax.nn.gelu(x)


def _split3(x):
    hi = x.astype(BF16)
    r = x - hi.astype(F32)
    mid = r.astype(BF16)
    lo = (r - mid.astype(F32)).astype(BF16)
    return hi, mid, lo


def _dot_f32_rhs(m_bf16, x):
    return functools.reduce(jnp.add, [jnp.dot(m_bf16, p, preferred_element_type=F32)
                                      for p in _split3(x)])


def _dot_f32_lhs(x, m_bf16):
    return functools.reduce(jnp.add, [jnp.dot(p, m_bf16, preferred_element_type=F32)
                                      for p in _split3(x)])


def _head_ones(dtype):
    r = lax.broadcasted_iota(jnp.int32, (WIDTH, WIDTH), 0) // HEAD_DIM
    c = lax.broadcasted_iota(jnp.int32, (WIDTH, WIDTH), 1) // HEAD_DIM
    return (r == c).astype(dtype)


def _mod_picker(modl_ref, modc_ref, tm, has_ctx):
    ml = modl_ref[0]
    if not has_ctx:
        return lambda r: ml[r:r + 1]
    mc = modc_ref[0]
    row0 = (pl.program_id(0) % (SEQ_ALL // tm)) * tm
    is_ctx = row0 + lax.broadcasted_iota(jnp.int32, (tm, 1), 0) < CTX_LEN
    return lambda r: jnp.where(is_ctx, mc[r:r + 1], ml[r:r + 1])


def _ada_kernel(c_ref, w_ref, b_ref, o_ref):
    s = _silu(c_ref[...]).astype(BF16)
    o_ref[...] = jnp.dot(s, w_ref[...].astype(BF16), preferred_element_type=F32) + b_ref[...]


def _ada(cvec, w_ada, b_ada):
    tn = 1536
    return pl.pallas_call(
        _ada_kernel,
        grid=(DEPTH, 6 * D_MODEL // tn),
        in_specs=[
            pl.BlockSpec((ADA_ROWS, D_MODEL), lambda l, j: (0, 0)),
            pl.BlockSpec((None, D_MODEL, tn), lambda l, j: (l, 0, j)),
            pl.BlockSpec((None, 1, tn), lambda l, j: (l, 0, j)),
        ],
        out_specs=pl.BlockSpec((None, ADA_ROWS, tn), lambda l, j: (l, 0, j)),
        out_shape=jax.ShapeDtypeStruct((DEPTH, ADA_ROWS, 6 * D_MODEL), F32),
        compiler_params=_params("arbitrary", "arbitrary"),
        name="ada",
    )(cvec, w_ada, b_ada.reshape(DEPTH, 1, 6 * D_MODEL))


def _inproj_kernel(x_ref, g_ref, modl_ref, modc_ref, w_ref, z_ref, *, tm):
    pick = _mod_picker(modl_ref, modc_ref, tm, True)
    h = _rms(x_ref[...], g_ref[...]) * (1.0 + pick(1)) + pick(0)
    z_ref[...] = jnp.dot(h.astype(BF16), w_ref[...], preferred_element_type=F32)


def _mod_specs(tiles_per_batch):
    return [
        pl.BlockSpec((1, 6, D_MODEL), lambda i, *_: (i // tiles_per_batch, 0, 0)),
        pl.BlockSpec((1, 6, D_MODEL), lambda i, *_: (CTX_ROW, 0, 0)),
    ]


def _inproj(x, g, mod, w_bf16):
    n = x.shape[0]
    tm = DENSE_TM
    return pl.pallas_call(
        functools.partial(_inproj_kernel, tm=tm),
        grid=(n // tm,),
        in_specs=[
            pl.BlockSpec((tm, D_MODEL), lambda i: (i, 0)),
            pl.BlockSpec((1, D_MODEL), lambda i: (0, 0)),
            *_mod_specs(SEQ_ALL // tm),
            pl.BlockSpec((D_MODEL, IN_COLS), lambda i: (0, 0)),
        ],
        out_specs=pl.BlockSpec((tm, IN_COLS), lambda i: (i, 0)),
        out_shape=jax.ShapeDtypeStruct((n, IN_COLS), F32),
        compiler_params=_params("arbitrary"),
        name="inproj",
    )(x, g.reshape(1, D_MODEL), mod, mod, w_bf16)


def _lru_kernel(x_ref, gate_ref, cw_ref, cb_ref, wr_ref, br_ref, wi_ref, bi_ref, lam_ref,
                y_ref, a_s, b_s, h_s, *, out_rows):
    x = x_ref[...]
    rows = lax.broadcasted_iota(jnp.int32, (SEQ_ALL, 1), 0)
    seg = rows < CTX_LEN
    u = jnp.zeros_like(x)
    for j in range(CONV_W):
        off = j - CONV_W // 2
        xs = x if off == 0 else pltpu.roll(x, (-off) % SEQ_ALL, 0)
        src = rows + off
        ok = (src >= 0) & (src < SEQ_ALL) & ((src < CTX_LEN) == seg)
        u = u + jnp.where(ok, xs, 0.0) * cw_ref[j:j + 1, :]
    u = u + cb_ref[...]
    ub = u.astype(BF16)
    for d in range(2):
        r = _sigmoid(jnp.dot(ub, wr_ref[d], preferred_element_type=F32) + br_ref[d:d + 1, :])
        i = _sigmoid(jnp.dot(ub, wi_ref[d], preferred_element_type=F32) + bi_ref[d:d + 1, :])
        nl = -lam_ref[d:d + 1, :]
        softplus = jnp.maximum(nl, 0.0) + jnp.log(1.0 + jnp.exp(-jnp.abs(nl)))
        log_a = -RG_C * r * softplus
        a_s[d] = jnp.exp(log_a)
        b_s[d] = jnp.sqrt(1.0 - jnp.exp(2.0 * log_a)) * i * u

    n_blk = SEQ_ALL // LRU_BLK
    n_ctx_blk = CTX_LEN // LRU_BLK
    sub = lax.broadcasted_iota(jnp.int32, (LRU_BLK, 1), 0)

    def block_scan(a, b, reverse):
        s = 1
        while s < LRU_BLK:
            if reverse:
                a_sh = pltpu.roll(a, LRU_BLK - s, 0)
                b_sh = pltpu.roll(b, LRU_BLK - s, 0)
                ok = sub < LRU_BLK - s
            else:
                a_sh = pltpu.roll(a, s, 0)
                b_sh = pltpu.roll(b, s, 0)
                ok = sub >= s
            b = jnp.where(ok, a * b_sh + b, b)
            a = jnp.where(ok, a * a_sh, a)
            s *= 2
        return a, b

    def body(n, carry):
        hf, hb = carry
        rf = pl.multiple_of(n * LRU_BLK, LRU_BLK)
        af, bf = block_scan(a_s[0, pl.ds(rf, LRU_BLK), :], b_s[0, pl.ds(rf, LRU_BLK), :], False)
        h = af * hf + bf
        h_s[0, pl.ds(rf, LRU_BLK), :] = h
        hf = h[LRU_BLK - 1:LRU_BLK, :]
        nb = jnp.where(n < n_ctx_blk, n_ctx_blk - 1 - n, n_blk + n_ctx_blk - 1 - n)
        rb = pl.multiple_of(nb * LRU_BLK, LRU_BLK)
        ab, bb = block_scan(a_s[1, pl.ds(rb, LRU_BLK), :], b_s[1, pl.ds(rb, LRU_BLK), :], True)
        h = ab * hb + bb
        h_s[1, pl.ds(rb, LRU_BLK), :] = h
        hb = h[0:1, :]
        return hf, hb

    zero = jnp.zeros((1, WIDTH), F32)
    lax.fori_loop(0, n_blk, body, (zero, zero))
    first = SEQ_ALL - out_rows
    y_ref[...] = (h_s[0, first:, :] + h_s[1, first:, :]) * _gelu(gate_ref[first:, :])


def _lru(z, cw, cb, wr_bd, br, wi_bd, bi, lam, out_rows):
    nb = z.shape[0] // SEQ_ALL
    full = lambda shape: pl.BlockSpec(shape, lambda b: (0,) * len(shape))
    return pl.pallas_call(
        functools.partial(_lru_kernel, out_rows=out_rows),
        grid=(nb,),
        in_specs=[
            pl.BlockSpec((SEQ_ALL, WIDTH), lambda b: (b, COL_A_X)),
            pl.BlockSpec((SEQ_ALL, WIDTH), lambda b: (b, COL_A_G)),
            full((CONV_W, WIDTH)), full((1, WIDTH)),
            full((2, WIDTH, WIDTH)), full((2, WIDTH)),
            full((2, WIDTH, WIDTH)), full((2, WIDTH)),
            full((2, WIDTH)),
        ],
        out_specs=pl.BlockSpec((out_rows, WIDTH), lambda b: (b, 0)),
        out_shape=jax.ShapeDtypeStruct((nb * out_rows, WIDTH), F32),
        scratch_shapes=[
            pltpu.VMEM((2, SEQ_ALL, WIDTH), F32),
            pltpu.VMEM((2, SEQ_ALL, WIDTH), F32),
            pltpu.VMEM((2, SEQ_ALL, WIDTH), F32),
        ],
        compiler_params=_params("arbitrary"),
        name="rglru",
    )(z, z, cw, cb, wr_bd, br, wi_bd, bi, lam)


def _rope(x, cos, sin_signed):
    lane = lax.broadcasted_iota(jnp.int32, (1, WIDTH), 1)
    first_half = (lane % 16) < 8
    partner = jnp.where(first_half, pltpu.roll(x, WIDTH - 8, 1), pltpu.roll(x, 8, 1))
    return x * cos + partner * sin_signed


def _attn_kernel(lam_ref, q_ref, k_ref, v_ref, cos_ref, sin_ref, g_ref, y_ref, kt_s, v_s,
                 *, with_ctx, out_scale):
    j = pl.program_id(1)

    @pl.when(j == 0)
    def _():
        kr = _rope(k_ref[...], cos_ref[...], sin_ref[...])
        kt_s[...] = kr.T.astype(BF16)
        v = v_ref[...]
        ones = jnp.ones((SEQ_ALL, HEAD_DIM), F32)
        for h in range(HEADS):
            v_s[h] = jnp.concatenate([v[:, h * HEAD_DIM:(h + 1) * HEAD_DIM], ones],
                                     axis=1).astype(BF16)

    lam = lam_ref[0, 0]
    tile = j if with_ctx else j + 1
    row0 = pl.multiple_of(tile * ATT_TQ, ATT_TQ)
    q = _rope(q_ref[...], cos_ref[pl.ds(row0, ATT_TQ), :], sin_ref[pl.ds(row0, ATT_TQ), :])
    q = q * (DA_QD ** -0.5)

    def attend(n_keys):
        outs = []
        for h in range(HEADS):
            branch = []
            for n in range(2):
                c0 = h * HEAD_DIM + n * DA_QD
                s = jnp.dot(q[:, c0:c0 + DA_QD].astype(BF16), kt_s[c0:c0 + DA_QD, 0:n_keys],
                            preferred_element_type=F32)
                p = jnp.exp(s - jnp.max(s, axis=-1, keepdims=True)).astype(BF16)
                ov = jnp.dot(p, v_s[h, 0:n_keys, :], preferred_element_type=F32)
                branch.append(ov[:, 0:HEAD_DIM] * (1.0 / ov[:, HEAD_DIM:HEAD_DIM + 1]))
            o = branch[0] - lam * branch[1]
            o = o * lax.rsqrt(jnp.mean(o * o, axis=-1, keepdims=True) + EPS)
            outs.append(o)
        y_ref[...] = jnp.concatenate(outs, axis=-1) * g_ref[...] * out_scale

    if with_ctx:
        @pl.when(j == 0)
        def _():
            attend(CTX_LEN)

        @pl.when(j > 0)
        def _():
            attend(SEQ_ALL)
    else:
        attend(SEQ_ALL)


def _attn(z, lam, cos, sin_signed, sub_g4, with_ctx, out_scale):
    nb = z.shape[0] // SEQ_ALL
    tiles = SEQ_ALL // ATT_TQ
    nq = tiles if with_ctx else tiles - 1
    first = 0 if with_ctx else 1
    return pl.pallas_call(
        functools.partial(_attn_kernel, with_ctx=with_ctx, out_scale=out_scale),
        grid=(nb, nq),
        in_specs=[
            pl.BlockSpec(memory_space=pltpu.SMEM),
            pl.BlockSpec((ATT_TQ, WIDTH), lambda b, j: (b * tiles + j + first, COL_B_Q)),
            pl.BlockSpec((SEQ_ALL, WIDTH), lambda b, j: (b, COL_B_K)),
            pl.BlockSpec((SEQ_ALL, WIDTH), lambda b, j: (b, COL_B_V)),
            pl.BlockSpec((SEQ_ALL, WIDTH), lambda b, j: (0, 0)),
            pl.BlockSpec((SEQ_ALL, WIDTH), lambda b, j: (0, 0)),
            pl.BlockSpec((1, WIDTH), lambda b, j: (0, 0)),
        ],
        out_specs=pl.BlockSpec((ATT_TQ, WIDTH), lambda b, j: (b * nq + j, 0)),
        out_shape=jax.ShapeDtypeStruct((nb * nq * ATT_TQ, WIDTH), F32),
        scratch_shapes=[
            pltpu.VMEM((WIDTH, SEQ_ALL), BF16),
            pltpu.VMEM((HEADS, SEQ_ALL, 2 * HEAD_DIM), BF16),
        ],
        compiler_params=_params("arbitrary", "arbitrary"),
        name="diffattn",
    )(lam, z, z, z, cos, sin_signed, sub_g4)


def _sgu_kernel(u_ref, v_ref, g_ref, w_ref, b_ref, y_ref, *, out_rows):
    row_head = lax.broadcasted_iota(jnp.int32, (HEADS * SG_CHUNK, WIDTH), 0) // SG_CHUNK
    col_head = lax.broadcasted_iota(jnp.int32, (HEADS * SG_CHUNK, WIDTH), 1) // HEAD_DIM
    head_mask = row_head == col_head
    w = w_ref[...]
    bias = b_ref[...]
    g = g_ref[...]
    first = SEQ_ALL - out_rows
    for n in range(first // SG_CHUNK, SEQ_ALL // SG_CHUNK):
        rows = slice(n * SG_CHUNK, (n + 1) * SG_CHUNK)
        vn = _rms(_gelu(v_ref[rows, :]), g).astype(BF16)
        stacked = jnp.where(head_mask, jnp.concatenate([vn] * HEADS, axis=0), jnp.zeros((), BF16))
        vm = jnp.dot(w, stacked, preferred_element_type=F32) + bias
        y_ref[n * SG_CHUNK - first:(n + 1) * SG_CHUNK - first, :] = _gelu(u_ref[rows, :]) * vm


def _sgu(z, norm_g, w_cat, bias2d, out_rows):
    nb = z.shape[0] // SEQ_ALL
    return pl.pallas_call(
        functools.partial(_sgu_kernel, out_rows=out_rows),
        grid=(nb,),
        in_specs=[
            pl.BlockSpec((SEQ_ALL, WIDTH), lambda b: (b, COL_C_U)),
            pl.BlockSpec((SEQ_ALL, WIDTH), lambda b: (b, COL_C_V)),
            pl.BlockSpec((1, WIDTH), lambda b: (0, 0)),
            pl.BlockSpec((SG_CHUNK, HEADS * SG_CHUNK), lambda b: (0, 0)),
            pl.BlockSpec((SG_CHUNK, WIDTH), lambda b: (0, 0)),
        ],
        out_specs=pl.BlockSpec((out_rows, WIDTH), lambda b: (b, 0)),
        out_shape=jax.ShapeDtypeStruct((nb * out_rows, WIDTH), F32),
        compiler_params=_params("arbitrary"),
        name="sgu",
    )(z, z, norm_g, w_cat, bias2d)


def _hgrn_kernel(q_ref, ff_ref, fb_ref, i_ref, gate_ref, lb_ref, gn_ref, y_ref,
                 b_s, k_s, o_s, st_s, stb_s, *, out_rows):
    c = HG_CHUNK
    pre = 128
    ones_bf = _head_ones(BF16)
    pr = lax.broadcasted_iota(jnp.int32, (pre, pre), 0)
    pc = lax.broadcasted_iota(jnp.int32, (pre, pre), 1)
    same = (pr // c) == (pc // c)
    tri = ((same & (pc <= pr)).astype(BF16), (same & (pc >= pr)).astype(BF16))

    for d, f_ref in enumerate((ff_ref, fb_ref)):
        lb = lb_ref[d:d + 1, :]
        for n in range(SEQ_ALL // pre):
            rows = slice(n * pre, (n + 1) * pre)
            f = lb + (1.0 - lb) * _sigmoid(f_ref[rows, :])
            k_s[d, rows, :] = 1.0 - f
            b_s[d, rows, :] = _dot_f32_rhs(tri[d], jnp.log(f))
    st_s[...] = jnp.zeros_like(st_s)
    stb_s[...] = jnp.zeros_like(stb_s)

    n_chunks = SEQ_ALL // c
    n_ctx = CTX_LEN // c
    half = c // 2
    sub = lax.broadcasted_iota(jnp.int32, (half, 1), 0)
    lane_head = lax.broadcasted_iota(jnp.int32, (1, 128), 1) // HEAD_DIM
    scale = HEAD_DIM ** -0.5

    def one_direction(d, chunk):
        r0 = pl.multiple_of(chunk * c, c)
        q = q_ref[pl.ds(r0, c), :] * scale
        v = i_ref[pl.ds(r0, c), :]
        b = b_s[d, pl.ds(r0, c), :]
        k = k_s[d, pl.ds(r0, c), :]
        tot = b[c - 1:c, :] if d == 0 else b[0:1, :]
        o = lax.dot_general((q * jnp.exp(b)).astype(BF16), stb_s[d],
                            (((1,), (1,)), ((), ())), preferred_element_type=F32)
        parts = []
        for s in range(c):
            for tile in range(2):
                lo_row, hi_row = tile * half, (tile + 1) * half
                if (d == 0 and hi_row <= s) or (d == 1 and lo_row > s):
                    parts.append(jnp.zeros((half, WIDTH), F32))
                    continue
                diff = b[lo_row:hi_row, :] - b[s:s + 1, :]
                if d == 0 and lo_row < s:
                    diff = jnp.where(sub + lo_row >= s, diff, -jnp.inf)
                elif d == 1 and hi_row - 1 > s:
                    diff = jnp.where(sub + lo_row <= s, diff, -jnp.inf)
                parts.append(q[lo_row:hi_row, :] * k[s:s + 1, :] * jnp.exp(diff))
        att = jnp.dot(jnp.concatenate(parts, axis=0).astype(BF16), ones_bf,
                      preferred_element_type=F32)
        for s in range(c):
            o = o + att[s * c:(s + 1) * c, :] * v[s:s + 1, :]
        o_s[d, pl.ds(r0, c), :] = o
        ke = (k * jnp.exp(tot - b)).astype(BF16)
        upd = lax.dot_general(v.astype(BF16), ke, (((0,), (0,)), ((), ())),
                              preferred_element_type=F32)
        decay = jnp.exp(tot)
        for h in range(HEADS):
            r = slice(h * HEAD_DIM, (h + 1) * HEAD_DIM)
            ct = slice((h // 2) * 128, (h // 2 + 1) * 128)
            new = st_s[d, r, ct] * decay[:, ct] + jnp.where(lane_head == h % 2, upd[r, ct], 0.0)
            st_s[d, r, ct] = new
            stb_s[d, r, ct] = new.astype(BF16)

    def body(n, carry):
        one_direction(0, n)
        one_direction(1, jnp.where(n < n_ctx, n_ctx - 1 - n, n_chunks + n_ctx - 1 - n))
        return carry

    lax.fori_loop(0, n_chunks, body, 0)

    first = SEQ_ALL - out_rows
    o = o_s[0, first:, :] + o_s[1, first:, :]
    ms = _dot_f32_lhs(o * o, ones_bf) * (1.0 / HEAD_DIM)
    y_ref[...] = o * lax.rsqrt(ms + EPS) * gn_ref[...] * _silu(gate_ref[first:, :])


def _hgrn(z, lb, gn, out_rows):
    nb = z.shape[0] // SEQ_ALL
    col = lambda cidx: pl.BlockSpec((SEQ_ALL, WIDTH), lambda b: (b, cidx))
    return pl.pallas_call(
        functools.partial(_hgrn_kernel, out_rows=out_rows),
        grid=(nb,),
        in_specs=[
            col(COL_D_Q), col(COL_D_FF), col(COL_D_FB), col(COL_D_I), col(COL_D_G),
            pl.BlockSpec((2, WIDTH), lambda b: (0, 0)),
            pl.BlockSpec((1, WIDTH), lambda b: (0, 0)),
        ],
        out_specs=pl.BlockSpec((out_rows, WIDTH), lambda b: (b, 0)),
        out_shape=jax.ShapeDtypeStruct((nb * out_rows, WIDTH), F32),
        scratch_shapes=[
            pltpu.VMEM((2, SEQ_ALL, WIDTH), F32),
            pltpu.VMEM((2, SEQ_ALL, WIDTH), F32),
            pltpu.VMEM((2, SEQ_ALL, WIDTH), F32),
            pltpu.VMEM((2, WIDTH, WIDTH), F32),
            pltpu.VMEM((2, WIDTH, WIDTH), BF16),
        ],
        compiler_params=_params("arbitrary"),
        name="hgrn2",
    )(z, z, z, z, z, lb, gn)


def _route(logits):
    rows = [logits[e:e + 1, :] for e in range(N_EXPERTS)]
    m = functools.reduce(jnp.maximum, rows)
    ex = [jnp.exp(r - m) for r in rows]
    inv = 1.0 / functools.reduce(jnp.add, ex)
    sc = [e * inv for e in ex]
    g_score = []
    for g in range(N_GROUPS):
        grp = sc[g * EXP_PER_GROUP:(g + 1) * EXP_PER_GROUP]
        pairs = [grp[a] + grp[b] for a in range(EXP_PER_GROUP) for b in range(a + 1, EXP_PER_GROUP)]
        g_score.append(functools.reduce(jnp.maximum, pairs))
    gate4 = [jnp.zeros_like(sc[0]) for _ in range(EXP_PER_GROUP)]
    onehot = []
    for g in range(N_GROUPS):
        g_ok = None
        for o in range(N_GROUPS):
            if o == g:
                continue
            t = (g_score[g] > g_score[o]) if o < g else (g_score[g] >= g_score[o])
            g_ok = t if g_ok is None else (g_ok & t)
        grp = sc[g * EXP_PER_GROUP:(g + 1) * EXP_PER_GROUP]
        picked = []
        for a in range(EXP_PER_GROUP):
            beaten = jnp.zeros_like(grp[a])
            for o in range(EXP_PER_GROUP):
                if o == a:
                    continue
                t = (grp[o] >= grp[a]) if o < a else (grp[o] > grp[a])
                beaten = beaten + jnp.where(t, 1.0, 0.0)
            picked.append(jnp.where((beaten < 1.5) & g_ok, grp[a], 0.0))
        denom = functools.reduce(jnp.add, picked)
        denom = jnp.where(g_ok, denom, 1.0)
        gate4 = [acc + p / denom for acc, p in zip(gate4, picked)]
        onehot.append(jnp.where(g_ok, 1.0, 0.0))
    return gate4, onehot


def _outproj_kernel(x_ref, ya_ref, yb_ref, ys_ref, yh_ref, w_ref, modl_ref, modc_ref, g_ref,
                    wr_ref, br_ref, xo_ref, hp_ref, pos_ref, cnt_ref, cnt_s, *, tm, has_ctx, cap):
    @pl.when(pl.program_id(0) == 0)
    def _():
        cnt_s[...] = jnp.zeros_like(cnt_s)

    pick = _mod_picker(modl_ref, modc_ref, tm, has_ctx)
    acc = jnp.zeros((tm, D_MODEL), F32)
    for kblk, y_ref in enumerate((ya_ref, yb_ref, ys_ref, yh_ref)):
        acc = acc + jnp.dot(y_ref[...].astype(BF16), w_ref[kblk * WIDTH:(kblk + 1) * WIDTH, :],
                            preferred_element_type=F32)
    x = x_ref[...] + pick(2) * acc
    xo_ref[...] = x
    h = _rms(x, g_ref[...]) * (1.0 + pick(4)) + pick(3)
    nt = (((1,), (1,)), ((), ()))
    wr = wr_ref[...]
    wr_hi = wr.astype(BF16)
    wr_lo = (wr - wr_hi.astype(F32)).astype(BF16)
    h_hi = h.astype(BF16)
    h_lo = (h - h_hi.astype(F32)).astype(BF16)
    logits = (lax.dot_general(wr_hi, h_hi, nt, preferred_element_type=F32)
              + lax.dot_general(wr_hi, h_lo, nt, preferred_element_type=F32)
              + lax.dot_general(wr_lo, h_hi, nt, preferred_element_type=F32)) + br_ref[...]
    gate4, onehot = _route(logits)
    hp_ref[:, 0:D_MODEL] = h
    gate_rows = jnp.concatenate(gate4 + [jnp.zeros((128 - EXP_PER_GROUP, tm), F32)], axis=0)
    hp_ref[:, D_MODEL:] = gate_rows.T
    sel = jnp.concatenate(onehot + [jnp.zeros((8 - N_GROUPS, tm), F32)], axis=0)
    before = (lax.broadcasted_iota(jnp.int32, (tm, tm), 0)
              <= lax.broadcasted_iota(jnp.int32, (tm, tm), 1)).astype(BF16)
    seen = jnp.dot(sel.astype(BF16), before, preferred_element_type=F32)
    carried = cnt_s[:, 0:1]
    base = lax.broadcasted_iota(jnp.int32, (8, 1), 0).astype(F32) * float(cap)
    slot = jnp.sum(sel * (base + carried + seen - 1.0), axis=0, keepdims=True)
    pos_ref[0] = slot.astype(jnp.int32)
    cnt_s[...] = cnt_s[...] + jnp.sum(sel, axis=1, keepdims=True)
    cnt_ref[...] = cnt_s[...].astype(jnp.int32)


def _outproj(x, ys, w_bf16, mod, g2, wr_t, br, has_ctx):
    n = ys[0].shape[0]
    if has_ctx:
        tm = DENSE_TM
        tpb = SEQ_ALL // tm
        x_spec = pl.BlockSpec((tm, D_MODEL), lambda i: (i, 0))
    else:
        tm = CTX_LEN
        tpb = SEQ // tm
        x_spec = pl.BlockSpec((tm, D_MODEL),
                              lambda i: ((i // tpb) * (SEQ_ALL // tm) + 1 + i % tpb, 0))
    tile = lambda w: pl.BlockSpec((tm, w), lambda i: (i, 0))
    return pl.pallas_call(
        functools.partial(_outproj_kernel, tm=tm, has_ctx=has_ctx, cap=n),
        grid=(n // tm,),
        in_specs=[
            x_spec, tile(WIDTH), tile(WIDTH), tile(WIDTH), tile(WIDTH),
            pl.BlockSpec((D_MODEL, D_MODEL), lambda i: (0, 0)),
            *_mod_specs(tpb),
            pl.BlockSpec((1, D_MODEL), lambda i: (0, 0)),
            pl.BlockSpec((N_EXPERTS, D_MODEL), lambda i: (0, 0)),
            pl.BlockSpec((N_EXPERTS, 1), lambda i: (0, 0)),
        ],
        out_specs=[
            tile(D_MODEL), tile(HP_COLS),
            pl.BlockSpec((1, 1, tm), lambda i: (i, 0, 0)),
            pl.BlockSpec((8, 128), lambda i: (0, 0)),
        ],
        out_shape=[
            jax.ShapeDtypeStruct((n, D_MODEL), F32),
            jax.ShapeDtypeStruct((n, HP_COLS), F32),
            jax.ShapeDtypeStruct((n // tm, 1, tm), jnp.int32),
            jax.ShapeDtypeStruct((8, 128), jnp.int32),
        ],
        scratch_shapes=[pltpu.VMEM((8, 128), F32)],
        compiler_params=_params("arbitrary"),
        name="outproj_router",
    )(x, *ys, w_bf16, mod, mod, g2.reshape(1, D_MODEL), wr_t, br)


def _row_copies_wait(src_rows, dst_rows, sem):
    pltpu.make_async_copy(src_rows, dst_rows, sem).wait()


def _invert_kernel(pos_ref, tok_ref):
    def clear(p, carry):
        tok_ref[p] = 0
        return carry

    def put(t, carry):
        tok_ref[pos_ref[t]] = t
        return carry

    lax.fori_loop(0, tok_ref.shape[0], clear, 0, unroll=8)
    lax.fori_loop(0, pos_ref.shape[0], put, 0, unroll=8)


def _invert(pos, n_slots):
    return pl.pallas_call(
        _invert_kernel,
        in_specs=[pl.BlockSpec(memory_space=pltpu.SMEM)],
        out_specs=pl.BlockSpec(memory_space=pltpu.SMEM),
        out_shape=jax.ShapeDtypeStruct((n_slots,), jnp.int32),
        name="moe_invert",
    )(pos)


def _dispatch_kernel(tok_ref, hp_hbm, hs_ref, sem, *, tm):
    def body(p, carry):
        pltpu.make_async_copy(hp_hbm.at[pl.ds(tok_ref[0, 0, p], 1)], hs_ref.at[pl.ds(p, 1)],
                              sem).start()
        return carry

    lax.fori_loop(0, tm, body, 0, unroll=8)
    _row_copies_wait(hp_hbm.at[pl.ds(0, tm)], hs_ref, sem)


def _dispatch(tok, hp):
    n_slots = tok.shape[0]
    tm = DISPATCH_TM
    return pl.pallas_call(
        functools.partial(_dispatch_kernel, tm=tm),
        grid=(n_slots // tm,),
        in_specs=[
            pl.BlockSpec((1, 1, tm), lambda i: (i, 0, 0), memory_space=pltpu.SMEM),
            pl.BlockSpec(memory_space=pl.ANY),
        ],
        out_specs=pl.BlockSpec((tm, HP_COLS), lambda i: (i, 0)),
        out_shape=jax.ShapeDtypeStruct((n_slots, HP_COLS), F32),
        scratch_shapes=[pltpu.SemaphoreType.DMA],
        compiler_params=_params("arbitrary"),
        name="moe_dispatch",
    )(tok.reshape(n_slots // tm, 1, tm), hp)


def _experts_kernel(grp_ref, valid_ref, hs_ref, w1_ref, w3_ref, w2_ref, ys_ref):
    n_valid = valid_ref[pl.program_id(0)]

    @pl.when(n_valid > 0)
    def _():
        w = hs_ref[...]
        ok = lax.broadcasted_iota(jnp.int32, (MOE_R, 1), 0) < n_valid
        h = jnp.where(ok, w[:, 0:D_MODEL], 0.0).astype(BF16)
        gates = jnp.where(ok, w[:, D_MODEL:], 0.0)
        acc = jnp.zeros((MOE_R, D_MODEL), F32)
        for e in range(EXP_PER_GROUP):
            a = jnp.dot(h, w1_ref[e], preferred_element_type=F32)
            b = jnp.dot(h, w3_ref[e], preferred_element_type=F32)
            he = (_silu(a) * b * gates[:, e:e + 1]).astype(BF16)
            acc = acc + jnp.dot(he, w2_ref[e], preferred_element_type=F32)
        ys_ref[...] = acc

    @pl.when(n_valid == 0)
    def _():
        ys_ref[...] = jnp.zeros_like(ys_ref)


def _experts(hs, blk_grp, blk_valid, w1, w3, w2):
    grid_spec = pltpu.PrefetchScalarGridSpec(
        num_scalar_prefetch=2,
        grid=(hs.shape[0] // MOE_R,),
        in_specs=[
            pl.BlockSpec((MOE_R, HP_COLS), lambda k, grp, valid: (k, 0)),
            pl.BlockSpec((EXP_PER_GROUP, D_MODEL, D_EXPERT), lambda k, grp, valid: (grp[k], 0, 0)),
            pl.BlockSpec((EXP_PER_GROUP, D_MODEL, D_EXPERT), lambda k, grp, valid: (grp[k], 0, 0)),
            pl.BlockSpec((EXP_PER_GROUP, D_EXPERT, D_MODEL), lambda k, grp, valid: (grp[k], 0, 0)),
        ],
        out_specs=pl.BlockSpec((MOE_R, D_MODEL), lambda k, grp, valid: (k, 0)),
    )
    return pl.pallas_call(
        _experts_kernel,
        grid_spec=grid_spec,
        out_shape=jax.ShapeDtypeStruct((hs.shape[0], D_MODEL), F32),
        compiler_params=_params("arbitrary"),
        name="moe_experts",
    )(blk_grp, blk_valid, hs, w1, w3, w2)


def _combine_kernel(pos_ref, x_ref, modl_ref, modc_ref, fg_ref, ys_hbm, o_ref, buf, sem,
                    *, tm, has_ctx, final_norm):
    def body(t, carry):
        pltpu.make_async_copy(ys_hbm.at[pl.ds(pos_ref[0, 0, t], 1)], buf.at[pl.ds(t, 1)],
                              sem).start()
        return carry

    lax.fori_loop(0, tm, body, 0, unroll=8)
    _row_copies_wait(ys_hbm.at[pl.ds(0, tm)], buf, sem)
    pick = _mod_picker(modl_ref, modc_ref, tm, has_ctx)
    x = x_ref[...] + pick(5) * buf[...]
    o_ref[...] = _rms(x, fg_ref[...]) if final_norm else x


def _combine(pos, x, ys, mod, final_g, has_ctx, final_norm):
    n = x.shape[0]
    tm = COMBINE_TM if has_ctx else COMBINE_TM_LATENT
    tpb = (SEQ_ALL if has_ctx else SEQ) // tm
    return pl.pallas_call(
        functools.partial(_combine_kernel, tm=tm, has_ctx=has_ctx, final_norm=final_norm),
        grid=(n // tm,),
        in_specs=[
            pl.BlockSpec((1, 1, tm), lambda i: (i, 0, 0), memory_space=pltpu.SMEM),
            pl.BlockSpec((tm, D_MODEL), lambda i: (i, 0)),
            *_mod_specs(tpb),
            pl.BlockSpec((1, D_MODEL), lambda i: (0, 0)),
            pl.BlockSpec(memory_space=pl.ANY),
        ],
        out_specs=pl.BlockSpec((tm, D_MODEL), lambda i: (i, 0)),
        out_shape=jax.ShapeDtypeStruct((n, D_MODEL), F32),
        scratch_shapes=[pltpu.VMEM((tm, D_MODEL), F32), pltpu.SemaphoreType.DMA],
        compiler_params=_params("arbitrary"),
        name="moe_combine",
    )(pos.reshape(n // tm, 1, tm), x, mod, mod, final_g.reshape(1, D_MODEL), ys)


def _routing_tables(slot, counts, n):
    n_blk = n // MOE_R + N_GROUPS
    n_g = counts[:N_GROUPS, 0]
    nb = (n_g + MOE_R - 1) // MOE_R
    ends = jnp.cumsum(nb)
    starts = ends - nb
    g_tok = slot // n
    pos = slot - g_tok * n + starts[g_tok] * MOE_R
    k = jnp.arange(n_blk, dtype=jnp.int32)
    g = jnp.minimum(jnp.sum((k[:, None] >= ends[None, :]).astype(jnp.int32), axis=1), N_GROUPS - 1)
    valid = jnp.where(k < ends[-1], jnp.clip(n_g[g] - (k - starts[g]) * MOE_R, 0, MOE_R), 0)
    return pos.astype(jnp.int32), g.astype(jnp.int32), valid.astype(jnp.int32)


def _moe(hp, slot, counts, x, w1, w3, w2, mod, final_g, has_ctx, final_norm):
    n = x.shape[0]
    assert n % MOE_R == 0
    pos, blk_grp, blk_valid = _routing_tables(slot.reshape(n), counts, n)
    tok = _invert(pos, n + N_GROUPS * MOE_R)
    hs = _dispatch(tok, hp)
    ys = _experts(hs, blk_grp, blk_valid, w1, w3, w2)
    return _combine(pos, x, ys, mod, final_g, has_ctx, final_norm)


def _block_diag(w):
    eye = jnp.eye(HEADS, dtype=w.dtype)
    return jnp.einsum('hij,hg->higj', w, eye).reshape(WIDTH, WIDTH)


def _rope_tables():
    rows = SEQ // GRID_W
    row_ids = jnp.repeat(jnp.arange(rows, dtype=F32), GRID_W)
    col_ids = jnp.tile(jnp.arange(GRID_W, dtype=F32), rows)
    n_freq = DA_QD // 4
    freqs = ROPE_THETA ** (-jnp.arange(n_freq, dtype=F32) / n_freq)
    ang = jnp.stack([row_ids[:, None] * freqs, col_ids[:, None] * freqs], axis=1)
    cos = jnp.cos(ang)
    sin = jnp.sin(ang)
    cos_l = jnp.broadcast_to(cos[:, None, None, :, None, :], (SEQ, HEADS, 2, 2, 2, n_freq))
    sin_l = jnp.broadcast_to(sin[:, None, None, :, None, :], (SEQ, HEADS, 2, 2, 2, n_freq))
    sign = jnp.array([-1.0, 1.0], F32)[None, None, None, None, :, None]
    cos_l = cos_l.reshape(SEQ, WIDTH)
    sin_l = (sin_l * sign).reshape(SEQ, WIDTH)
    cos_all = jnp.concatenate([jnp.ones((CTX_LEN, WIDTH), F32), cos_l], axis=0)
    sin_all = jnp.concatenate([jnp.zeros((CTX_LEN, WIDTH), F32), sin_l], axis=0)
    return cos_all, sin_all


def kernel(x, c, ctx, c_ctx, w_ada, b_ada, norm1_g, norm2_g, w_in, w_out, lru_conv_w, lru_conv_b,
           lru_wr, lru_br, lru_wi, lru_bi, lru_lam, da_lam, da_subln_g, sg_norm_g, sg_w, sg_b,
           hg_lb, hg_norm_g, router_w, router_b, moe_w1, moe_w3, moe_w2, final_norm_g):
    nb = x.shape[0]
    assert nb <= CTX_ROW and x.shape[1:] == (SEQ, D_MODEL) and ctx.shape[1:] == (CTX_LEN, D_MODEL)
    xs = jnp.concatenate([ctx, x], axis=1).reshape(nb * SEQ_ALL, D_MODEL)

    cvec = jnp.zeros((ADA_ROWS, D_MODEL), F32).at[:nb].set(c).at[CTX_ROW].set(c_ctx)
    mods = _ada(cvec, w_ada, b_ada).reshape(DEPTH, ADA_ROWS, 6, D_MODEL)

    cos_all, sin_all = _rope_tables()
    lb_cum = jnp.cumsum(jax.nn.softmax(hg_lb.astype(F32), axis=1), axis=1)
    lb_all = lb_cum - lb_cum[:, :1]
    wr_t = router_w.T
    br = router_b.reshape(N_EXPERTS, 1)

    for l in range(DEPTH):
        last = l == DEPTH - 1
        out_rows = SEQ if last else SEQ_ALL
        z = _inproj(xs, norm1_g[l], mods[l], w_in[l].astype(BF16))

        ya = _lru(z, lru_conv_w[l], lru_conv_b[l].reshape(1, WIDTH),
                  jax.vmap(_block_diag)(lru_wr[l]).astype(BF16), lru_br[l],
                  jax.vmap(_block_diag)(lru_wi[l]).astype(BF16), lru_bi[l], lru_lam[l], out_rows)

        lam_init = 0.8 - 0.6 * math.exp(-0.3 * l)
        lf = da_lam[l].astype(F32)
        lam = jnp.exp(jnp.sum(lf[0] * lf[1])) - jnp.exp(jnp.sum(lf[2] * lf[3])) + lam_init
        yb = _attn(z, lam.reshape(1, 1), cos_all, sin_all,
                   jnp.tile(da_subln_g[l], HEADS).reshape(1, WIDTH),
                   with_ctx=not last, out_scale=1.0 - lam_init)

        w_cat = jnp.transpose(sg_w[l], (1, 0, 2)).reshape(SG_CHUNK, HEADS * SG_CHUNK).astype(BF16)
        bias2d = jnp.repeat(sg_b[l].T, HEAD_DIM, axis=1)
        ys = _sgu(z, sg_norm_g[l].reshape(1, WIDTH), w_cat, bias2d, out_rows)

        yh = _hgrn(z, lb_all[:, l], hg_norm_g[l].reshape(1, WIDTH), out_rows)

        xs, hp, pos, counts = _outproj(xs, (ya, yb, ys, yh), w_out[l].astype(BF16), mods[l],
                                       norm2_g[l], wr_t, br, has_ctx=not last)
        xs = _moe(hp, pos, counts, xs, moe_w1[l].astype(BF16), moe_w3[l].astype(BF16),
                  moe_w2[l].astype(BF16), mods[l], final_norm_g, has_ctx=not last,
                  final_norm=last)

    return xs.reshape(nb, SEQ, D_MODEL)
```

```python
import functools
import math

import jax
import jax.numpy as jnp
from jax import lax
from jax.experimental import pallas as pl
from jax.experimental.pallas import tpu as pltpu

F32 = jnp.float32
BF16 = jnp.bfloat16

D_MODEL = 1024
SEQ = 2048
CTX_LEN = 256
SEQ_ALL = CTX_LEN + SEQ
DEPTH = 2
GRID_W = 64
EPS = 1e-6
WIDTH = 256
HEADS = 4
HEAD_DIM = 64
CONV_W = 4
RG_C = 8.0
DA_QD = 32
ROPE_THETA = 10000.0
SG_CHUNK = 128
N_EXPERTS = 16
N_GROUPS = 4
EXP_PER_GROUP = 4
D_EXPERT = 512
IN_COLS = 3072
COL_A_X, COL_A_G = 0, 1
COL_B_Q, COL_B_K, COL_B_V = 2, 3, 4
COL_C_U, COL_C_V = 5, 6
COL_D_Q, COL_D_FF, COL_D_FB, COL_D_I, COL_D_G = 7, 8, 9, 10, 11

ADA_ROWS = 16
CTX_ROW = 8
VMEM_LIMIT = 56 * 1024 * 1024

LRU_BLK = 8
HG_CHUNK = 16
ATT_TQ = 256
DENSE_TM = 768
MOE_R = 512
HP_COLS = D_MODEL + 128
DISPATCH_TM = 2048
COMBINE_TM = 1152
COMBINE_TM_LATENT = 1024


def _params(*sem):
    return pltpu.CompilerParams(dimension_semantics=sem, vmem_limit_bytes=VMEM_LIMIT)


def _rms(xf, g):
    return xf * lax.rsqrt(jnp.mean(xf * xf, axis=-1, keepdims=True) + EPS) * g


def _sigmoid(x):
    return 1.0 / (1.0 + jnp.exp(-x))


def _silu(x):
    return x * _sigmoid(x)


def _gelu(x):
    return jax.nn.gelu(x)


def _split3(x):
    hi = x.astype(BF16)
    r = x - hi.astype(F32)
    mid = r.astype(BF16)
    lo = (r - mid.astype(F32)).astype(BF16)
    return hi, mid, lo


def _dot_f32_rhs(m_bf16, x):
    return functools.reduce(jnp.add, [jnp.dot(m_bf16, p, preferred_element_type=F32)
                                      for p in _split3(x)])


def _dot_f32_lhs(x, m_bf16):
    return functools.reduce(jnp.add, [jnp.dot(p, m_bf16, preferred_element_type=F32)
                                      for p in _split3(x)])


def _head_ones(dtype):
    r = lax.broadcasted_iota(jnp.int32, (WIDTH, WIDTH), 0) // HEAD_DIM
    c = lax.broadcasted_iota(jnp.int32, (WIDTH, WIDTH), 1) // HEAD_DIM
    return (r == c).astype(dtype)


def _mod_picker(modl_ref, modc_ref, tm, has_ctx):
    ml = modl_ref[0]
    if not has_ctx:
        return lambda r: ml[r:r + 1]
    mc = modc_ref[0]
    row0 = (pl.program_id(0) % (SEQ_ALL // tm)) * tm
    is_ctx = row0 + lax.broadcasted_iota(jnp.int32, (tm, 1), 0) < CTX_LEN
    return lambda r: jnp.where(is_ctx, mc[r:r + 1], ml[r:r + 1])


def _ada_kernel(c_ref, w_ref, b_ref, o_ref):
    s = _silu(c_ref[...]).astype(BF16)
    o_ref[...] = jnp.dot(s, w_ref[...].astype(BF16), preferred_element_type=F32) + b_ref[...]


def _ada(cvec, w_ada, b_ada):
    tn = 1536
    return pl.pallas_call(
        _ada_kernel,
        grid=(DEPTH, 6 * D_MODEL // tn),
        in_specs=[
            pl.BlockSpec((ADA_ROWS, D_MODEL), lambda l, j: (0, 0)),
            pl.BlockSpec((None, D_MODEL, tn), lambda l, j: (l, 0, j)),
            pl.BlockSpec((None, 1, tn), lambda l, j: (l, 0, j)),
        ],
        out_specs=pl.BlockSpec((None, ADA_ROWS, tn), lambda l, j: (l, 0, j)),
        out_shape=jax.ShapeDtypeStruct((DEPTH, ADA_ROWS, 6 * D_MODEL), F32),
        compiler_params=_params("arbitrary", "arbitrary"),
        name="ada",
    )(cvec, w_ada, b_ada.reshape(DEPTH, 1, 6 * D_MODEL))


def _inproj_kernel(x_ref, g_ref, modl_ref, modc_ref, w_ref, z_ref, *, tm):
    pick = _mod_picker(modl_ref, modc_ref, tm, True)
    h = _rms(x_ref[...], g_ref[...]) * (1.0 + pick(1)) + pick(0)
    z_ref[...] = jnp.dot(h.astype(BF16), w_ref[...], preferred_element_type=F32)


def _mod_specs(tiles_per_batch):
    return [
        pl.BlockSpec((1, 6, D_MODEL), lambda i, *_: (i // tiles_per_batch, 0, 0)),
        pl.BlockSpec((1, 6, D_MODEL), lambda i, *_: (CTX_ROW, 0, 0)),
    ]


def _inproj(x, g, mod, w_bf16):
    n = x.shape[0]
    tm = DENSE_TM
    return pl.pallas_call(
        functools.partial(_inproj_kernel, tm=tm),
        grid=(n // tm,),
        in_specs=[
            pl.BlockSpec((tm, D_MODEL), lambda i: (i, 0)),
            pl.BlockSpec((1, D_MODEL), lambda i: (0, 0)),
            *_mod_specs(SEQ_ALL // tm),
            pl.BlockSpec((D_MODEL, IN_COLS), lambda i: (0, 0)),
        ],
        out_specs=pl.BlockSpec((tm, IN_COLS), lambda i: (i, 0)),
        out_shape=jax.ShapeDtypeStruct((n, IN_COLS), F32),
        compiler_params=_params("arbitrary"),
        name="inproj",
    )(x, g.reshape(1, D_MODEL), mod, mod, w_bf16)


def _lru_kernel(x_ref, gate_ref, cw_ref, cb_ref, wr_ref, br_ref, wi_ref, bi_ref, lam_ref,
                y_ref, a_s, b_s, h_s, *, out_rows):
    x = x_ref[...]
    rows = lax.broadcasted_iota(jnp.int32, (SEQ_ALL, 1), 0)
    seg = rows < CTX_LEN
    u = jnp.zeros_like(x)
    for j in range(CONV_W):
        off = j - CONV_W // 2
        xs = x if off == 0 else pltpu.roll(x, (-off) % SEQ_ALL, 0)
        src = rows + off
        ok = (src >= 0) & (src < SEQ_ALL) & ((src < CTX_LEN) == seg)
        u = u + jnp.where(ok, xs, 0.0) * cw_ref[j:j + 1, :]
    u = u + cb_ref[...]
    ub = u.astype(BF16)
    for d in range(2):
        r = _sigmoid(jnp.dot(ub, wr_ref[d], preferred_element_type=F32) + br_ref[d:d + 1, :])
        i = _sigmoid(jnp.dot(ub, wi_ref[d], preferred_element_type=F32) + bi_ref[d:d + 1, :])
        nl = -lam_ref[d:d + 1, :]
        softplus = jnp.maximum(nl, 0.0) + jnp.log(1.0 + jnp.exp(-jnp.abs(nl)))
        log_a = -RG_C * r * softplus
        a_s[d] = jnp.exp(log_a)
        b_s[d] = jnp.sqrt(1.0 - jnp.exp(2.0 * log_a)) * i * u

    n_blk = SEQ_ALL // LRU_BLK
    n_ctx_blk = CTX_LEN // LRU_BLK
    sub = lax.broadcasted_iota(jnp.int32, (LRU_BLK, 1), 0)

    def block_scan(a, b, reverse):
        s = 1
        while s < LRU_BLK:
            if reverse:
                a_sh = pltpu.roll(a, LRU_BLK - s, 0)
                b_sh = pltpu.roll(b, LRU_BLK - s, 0)
                ok = sub < LRU_BLK - s
            else:
                a_sh = pltpu.roll(a, s, 0)
                b_sh = pltpu.roll(b, s, 0)
                ok = sub >= s
            b = jnp.where(ok, a * b_sh + b, b)
            a = jnp.where(ok, a * a_sh, a)
            s *= 2
        return a, b

    def body(n, carry):
        hf, hb = carry
        rf = pl.multiple_of(n * LRU_BLK, LRU_BLK)
        af, bf = block_scan(a_s[0, pl.ds(rf, LRU_BLK), :], b_s[0, pl.ds(rf, LRU_BLK), :], False)
        h = af * hf + bf
        h_s[0, pl.ds(rf, LRU_BLK), :] = h
        hf = h[LRU_BLK - 1:LRU_BLK, :]
        nb = jnp.where(n < n_ctx_blk, n_ctx_blk - 1 - n, n_blk + n_ctx_blk - 1 - n)
        rb = pl.multiple_of(nb * LRU_BLK, LRU_BLK)
        ab, bb = block_scan(a_s[1, pl.ds(rb, LRU_BLK), :], b_s[1, pl.ds(rb, LRU_BLK), :], True)
        h = ab * hb + bb
        h_s[1, pl.ds(rb, LRU_BLK), :] = h
        hb = h[0:1, :]
        return hf, hb

    zero = jnp.zeros((1, WIDTH), F32)
    lax.fori_loop(0, n_blk, body, (zero, zero))
    first = SEQ_ALL - out_rows
    y_ref[...] = (h_s[0, first:, :] + h_s[1, first:, :]) * _gelu(gate_ref[first:, :])


def _lru(z, cw, cb, wr_bd, br, wi_bd, bi, lam, out_rows):
    nb = z.shape[0] // SEQ_ALL
    full = lambda shape: pl.BlockSpec(shape, lambda b: (0,) * len(shape))
    return pl.pallas_call(
        functools.partial(_lru_kernel, out_rows=out_rows),
        grid=(nb,),
        in_specs=[
            pl.BlockSpec((SEQ_ALL, WIDTH), lambda b: (b, COL_A_X)),
            pl.BlockSpec((SEQ_ALL, WIDTH), lambda b: (b, COL_A_G)),
            full((CONV_W, WIDTH)), full((1, WIDTH)),
            full((2, WIDTH, WIDTH)), full((2, WIDTH)),
            full((2, WIDTH, WIDTH)), full((2, WIDTH)),
            full((2, WIDTH)),
        ],
        out_specs=pl.BlockSpec((out_rows, WIDTH), lambda b: (b, 0)),
        out_shape=jax.ShapeDtypeStruct((nb * out_rows, WIDTH), F32),
        scratch_shapes=[
            pltpu.VMEM((2, SEQ_ALL, WIDTH), F32),
            pltpu.VMEM((2, SEQ_ALL, WIDTH), F32),
            pltpu.VMEM((2, SEQ_ALL, WIDTH), F32),
        ],
        compiler_params=_params("arbitrary"),
        name="rglru",
    )(z, z, cw, cb, wr_bd, br, wi_bd, bi, lam)


def _rope(x, cos, sin_signed):
    lane = lax.broadcasted_iota(jnp.int32, (1, WIDTH), 1)
    first_half = (lane % 16) < 8
    partner = jnp.where(first_half, pltpu.roll(x, WIDTH - 8, 1), pltpu.roll(x, 8, 1))
    return x * cos + partner * sin_signed


def _attn_kernel(lam_ref, q_ref, k_ref, v_ref, cos_ref, sin_ref, g_ref, y_ref, kt_s, v_s,
                 *, with_ctx, out_scale):
    j = pl.program_id(1)

    @pl.when(j == 0)
    def _():
        kr = _rope(k_ref[...], cos_ref[...], sin_ref[...])
        kt_s[...] = kr.T.astype(BF16)
        v = v_ref[...]
        ones = jnp.ones((SEQ_ALL, HEAD_DIM), F32)
        for h in range(HEADS):
            v_s[h] = jnp.concatenate([v[:, h * HEAD_DIM:(h + 1) * HEAD_DIM], ones],
                                     axis=1).astype(BF16)

    lam = lam_ref[0, 0]
    tile = j if with_ctx else j + 1
    row0 = pl.multiple_of(tile * ATT_TQ, ATT_TQ)
    q = _rope(q_ref[...], cos_ref[pl.ds(row0, ATT_TQ), :], sin_ref[pl.ds(row0, ATT_TQ), :])
    q = q * (DA_QD ** -0.5)

    def attend(n_keys):
        outs = []
        for h in range(HEADS):
            branch = []
            for n in range(2):
                c0 = h * HEAD_DIM + n * DA_QD
                s = jnp.dot(q[:, c0:c0 + DA_QD].astype(BF16), kt_s[c0:c0 + DA_QD, 0:n_keys],
                            preferred_element_type=F32)
                p = jnp.exp(s - jnp.max(s, axis=-1, keepdims=True)).astype(BF16)
                ov = jnp.dot(p, v_s[h, 0:n_keys, :], preferred_element_type=F32)
                branch.append(ov[:, 0:HEAD_DIM] * (1.0 / ov[:, HEAD_DIM:HEAD_DIM + 1]))
            o = branch[0] - lam * branch[1]
            o = o * lax.rsqrt(jnp.mean(o * o, axis=-1, keepdims=True) + EPS)
            outs.append(o)
        y_ref[...] = jnp.concatenate(outs, axis=-1) * g_ref[...] * out_scale

    if with_ctx:
        @pl.when(j == 0)
        def _():
            attend(CTX_LEN)

        @pl.when(j > 0)
        def _():
            attend(SEQ_ALL)
    else:
        attend(SEQ_ALL)


def _attn(z, lam, cos, sin_signed, sub_g4, with_ctx, out_scale):
    nb = z.shape[0] // SEQ_ALL
    tiles = SEQ_ALL // ATT_TQ
    nq = tiles if with_ctx else tiles - 1
    first = 0 if with_ctx else 1
    return pl.pallas_call(
        functools.partial(_attn_kernel, with_ctx=with_ctx, out_scale=out_scale),
        grid=(nb, nq),
        in_specs=[
            pl.BlockSpec(memory_space=pltpu.SMEM),
            pl.BlockSpec((ATT_TQ, WIDTH), lambda b, j: (b * tiles + j + first, COL_B_Q)),
            pl.BlockSpec((SEQ_ALL, WIDTH), lambda b, j: (b, COL_B_K)),
            pl.BlockSpec((SEQ_ALL, WIDTH), lambda b, j: (b, COL_B_V)),
            pl.BlockSpec((SEQ_ALL, WIDTH), lambda b, j: (0, 0)),
            pl.BlockSpec((SEQ_ALL, WIDTH), lambda b, j: (0, 0)),
            pl.BlockSpec((1, WIDTH), lambda b, j: (0, 0)),
        ],
        out_specs=pl.BlockSpec((ATT_TQ, WIDTH), lambda b, j: (b * nq + j, 0)),
        out_shape=jax.ShapeDtypeStruct((nb * nq * ATT_TQ, WIDTH), F32),
        scratch_shapes=[
            pltpu.VMEM((WIDTH, SEQ_ALL), BF16),
            pltpu.VMEM((HEADS, SEQ_ALL, 2 * HEAD_DIM), BF16),
        ],
        compiler_params=_params("arbitrary", "arbitrary"),
        name="diffattn",
    )(lam, z, z, z, cos, sin_signed, sub_g4)


def _sgu_kernel(u_ref, v_ref, g_ref, w_ref, b_ref, y_ref, *, out_rows):
    row_head = lax.broadcasted_iota(jnp.int32, (HEADS * SG_CHUNK, WIDTH), 0) // SG_CHUNK
    col_head = lax.broadcasted_iota(jnp.int32, (HEADS * SG_CHUNK, WIDTH), 1) // HEAD_DIM
    head_mask = row_head == col_head
    w = w_ref[...]
    bias = b_ref[...]
    g = g_ref[...]
    first = SEQ_ALL - out_rows
    for n in range(first // SG_CHUNK, SEQ_ALL // SG_CHUNK):
        rows = slice(n * SG_CHUNK, (n + 1) * SG_CHUNK)
        vn = _rms(_gelu(v_ref[rows, :]), g).astype(BF16)
        stacked = jnp.where(head_mask, jnp.concatenate([vn] * HEADS, axis=0), jnp.zeros((), BF16))
        vm = jnp.dot(w, stacked, preferred_element_type=F32) + bias
        y_ref[n * SG_CHUNK - first:(n + 1) * SG_CHUNK - first, :] = _gelu(u_ref[rows, :]) * vm


def _sgu(z, norm_g, w_cat, bias2d, out_rows):
    nb = z.shape[0] // SEQ_ALL
    return pl.pallas_call(
        functools.partial(_sgu_kernel, out_rows=out_rows),
        grid=(nb,),
        in_specs=[
            pl.BlockSpec((SEQ_ALL, WIDTH), lambda b: (b, COL_C_U)),
            pl.BlockSpec((SEQ_ALL, WIDTH), lambda b: (b, COL_C_V)),
            pl.BlockSpec((1, WIDTH), lambda b: (0, 0)),
            pl.BlockSpec((SG_CHUNK, HEADS * SG_CHUNK), lambda b: (0, 0)),
            pl.BlockSpec((SG_CHUNK, WIDTH), lambda b: (0, 0)),
        ],
        out_specs=pl.BlockSpec((out_rows, WIDTH), lambda b: (b, 0)),
        out_shape=jax.ShapeDtypeStruct((nb * out_rows, WIDTH), F32),
        compiler_params=_params("arbitrary"),
        name="sgu",
    )(z, z, norm_g, w_cat, bias2d)


def _hgrn_kernel(q_ref, ff_ref, fb_ref, i_ref, gate_ref, lb_ref, gn_ref, y_ref,
                 b_s, k_s, o_s, st_s, stb_s, *, out_rows):
    c = HG_CHUNK
    pre = 128
    ones_bf = _head_ones(BF16)
    pr = lax.broadcasted_iota(jnp.int32, (pre, pre), 0)
    pc = lax.broadcasted_iota(jnp.int32, (pre, pre), 1)
    same = (pr // c) == (pc // c)
    tri = ((same & (pc <= pr)).astype(BF16), (same & (pc >= pr)).astype(BF16))

    for d, f_ref in enumerate((ff_ref, fb_ref)):
        lb = lb_ref[d:d + 1, :]
        for n in range(SEQ_ALL // pre):
            rows = slice(n * pre, (n + 1) * pre)
            f = lb + (1.0 - lb) * _sigmoid(f_ref[rows, :])
            k_s[d, rows, :] = 1.0 - f
            b_s[d, rows, :] = _dot_f32_rhs(tri[d], jnp.log(f))
    st_s[...] = jnp.zeros_like(st_s)
    stb_s[...] = jnp.zeros_like(stb_s)

    n_chunks = SEQ_ALL // c
    n_ctx = CTX_LEN // c
    half = c // 2
    sub = lax.broadcasted_iota(jnp.int32, (half, 1), 0)
    lane_head = lax.broadcasted_iota(jnp.int32, (1, 128), 1) // HEAD_DIM
    scale = HEAD_DIM ** -0.5

    def one_direction(d, chunk):
        r0 = pl.multiple_of(chunk * c, c)
        q = q_ref[pl.ds(r0, c), :] * scale
        v = i_ref[pl.ds(r0, c), :]
        b = b_s[d, pl.ds(r0, c), :]
        k = k_s[d, pl.ds(r0, c), :]
        tot = b[c - 1:c, :] if d == 0 else b[0:1, :]
        o = lax.dot_general((q * jnp.exp(b)).astype(BF16), stb_s[d],
                            (((1,), (1,)), ((), ())), preferred_element_type=F32)
        parts = []
        for s in range(c):
            for tile in range(2):
                lo_row, hi_row = tile * half, (tile + 1) * half
                if (d == 0 and hi_row <= s) or (d == 1 and lo_row > s):
                    parts.append(jnp.zeros((half, WIDTH), F32))
                    continue
                diff = b[lo_row:hi_row, :] - b[s:s + 1, :]
                if d == 0 and lo_row < s:
                    diff = jnp.where(sub + lo_row >= s, diff, -jnp.inf)
                elif d == 1 and hi_row - 1 > s:
                    diff = jnp.where(sub + lo_row <= s, diff, -jnp.inf)
                parts.append(q[lo_row:hi_row, :] * k[s:s + 1, :] * jnp.exp(diff))
        att = jnp.dot(jnp.concatenate(parts, axis=0).astype(BF16), ones_bf,
                      preferred_element_type=F32)
        for s in range(c):
            o = o + att[s * c:(s + 1) * c, :] * v[s:s + 1, :]
        o_s[d, pl.ds(r0, c), :] = o
        ke = (k * jnp.exp(tot - b)).astype(BF16)
        upd = lax.dot_general(v.astype(BF16), ke, (((0,), (0,)), ((), ())),
                              preferred_element_type=F32)
        decay = jnp.exp(tot)
        for h in range(HEADS):
            r = slice(h * HEAD_DIM, (h + 1) * HEAD_DIM)
            ct = slice((h // 2) * 128, (h // 2 + 1) * 128)
            new = st_s[d, r, ct] * decay[:, ct] + jnp.where(lane_head == h % 2, upd[r, ct], 0.0)
            st_s[d, r, ct] = new
            stb_s[d, r, ct] = new.astype(BF16)

    def body(n, carry):
        one_direction(0, n)
        one_direction(1, jnp.where(n < n_ctx, n_ctx - 1 - n, n_chunks + n_ctx - 1 - n))
        return carry

    lax.fori_loop(0, n_chunks, body, 0, unroll=4)

    first = SEQ_ALL - out_rows
    o = o_s[0, first:, :] + o_s[1, first:, :]
    ms = _dot_f32_lhs(o * o, ones_bf) * (1.0 / HEAD_DIM)
    y_ref[...] = o * lax.rsqrt(ms + EPS) * gn_ref[...] * _silu(gate_ref[first:, :])


def _hgrn(z, lb, gn, out_rows):
    nb = z.shape[0] // SEQ_ALL
    col = lambda cidx: pl.BlockSpec((SEQ_ALL, WIDTH), lambda b: (b, cidx))
    return pl.pallas_call(
        functools.partial(_hgrn_kernel, out_rows=out_rows),
        grid=(nb,),
        in_specs=[
            col(COL_D_Q), col(COL_D_FF), col(COL_D_FB), col(COL_D_I), col(COL_D_G),
            pl.BlockSpec((2, WIDTH), lambda b: (0, 0)),
            pl.BlockSpec((1, WIDTH), lambda b: (0, 0)),
        ],
        out_specs=pl.BlockSpec((out_rows, WIDTH), lambda b: (b, 0)),
        out_shape=jax.ShapeDtypeStruct((nb * out_rows, WIDTH), F32),
        scratch_shapes=[
            pltpu.VMEM((2, SEQ_ALL, WIDTH), F32),
            pltpu.VMEM((2, SEQ_ALL, WIDTH), F32),
            pltpu.VMEM((2, SEQ_ALL, WIDTH), F32),
            pltpu.VMEM((2, WIDTH, WIDTH), F32),
            pltpu.VMEM((2, WIDTH, WIDTH), BF16),
        ],
        compiler_params=_params("arbitrary"),
        name="hgrn2",
    )(z, z, z, z, z, lb, gn)


def _route(logits):
    rows = [logits[e:e + 1, :] for e in range(N_EXPERTS)]
    m = functools.reduce(jnp.maximum, rows)
    ex = [jnp.exp(r - m) for r in rows]
    inv = 1.0 / functools.reduce(jnp.add, ex)
    sc = [e * inv for e in ex]
    g_score = []
    for g in range(N_GROUPS):
        grp = sc[g * EXP_PER_GROUP:(g + 1) * EXP_PER_GROUP]
        pairs = [grp[a] + grp[b] for a in range(EXP_PER_GROUP) for b in range(a + 1, EXP_PER_GROUP)]
        g_score.append(functools.reduce(jnp.maximum, pairs))
    gate4 = [jnp.zeros_like(sc[0]) for _ in range(EXP_PER_GROUP)]
    onehot = []
    for g in range(N_GROUPS):
        g_ok = None
        for o in range(N_GROUPS):
            if o == g:
                continue
            t = (g_score[g] > g_score[o]) if o < g else (g_score[g] >= g_score[o])
            g_ok = t if g_ok is None else (g_ok & t)
        grp = sc[g * EXP_PER_GROUP:(g + 1) * EXP_PER_GROUP]
        picked = []
        for a in range(EXP_PER_GROUP):
            beaten = jnp.zeros_like(grp[a])
            for o in range(EXP_PER_GROUP):
                if o == a:
                    continue
                t = (grp[o] >= grp[a]) if o < a else (grp[o] > grp[a])
                beaten = beaten + jnp.where(t, 1.0, 0.0)
            picked.append(jnp.where((beaten < 1.5) & g_ok, grp[a], 0.0))
        denom = functools.reduce(jnp.add, picked)
        denom = jnp.where(g_ok, denom, 1.0)
        gate4 = [acc + p / denom for acc, p in zip(gate4, picked)]
        onehot.append(jnp.where(g_ok, 1.0, 0.0))
    return gate4, onehot


def _outproj_kernel(x_ref, ya_ref, yb_ref, ys_ref, yh_ref, w_ref, modl_ref, modc_ref, g_ref,
                    wr_ref, br_ref, xo_ref, hp_ref, pos_ref, cnt_ref, cnt_s, *, tm, has_ctx, cap):
    @pl.when(pl.program_id(0) == 0)
    def _():
        cnt_s[...] = jnp.zeros_like(cnt_s)

    pick = _mod_picker(modl_ref, modc_ref, tm, has_ctx)
    acc = jnp.zeros((tm, D_MODEL), F32)
    for kblk, y_ref in enumerate((ya_ref, yb_ref, ys_ref, yh_ref)):
        acc = acc + jnp.dot(y_ref[...].astype(BF16), w_ref[kblk * WIDTH:(kblk + 1) * WIDTH, :],
                            preferred_element_type=F32)
    x = x_ref[...] + pick(2) * acc
    xo_ref[...] = x
    h = _rms(x, g_ref[...]) * (1.0 + pick(4)) + pick(3)
    nt = (((1,), (1,)), ((), ()))
    wr = wr_ref[...]
    wr_hi = wr.astype(BF16)
    wr_lo = (wr - wr_hi.astype(F32)).astype(BF16)
    h_hi = h.astype(BF16)
    h_lo = (h - h_hi.astype(F32)).astype(BF16)
    logits = (lax.dot_general(wr_hi, h_hi, nt, preferred_element_type=F32)
              + lax.dot_general(wr_hi, h_lo, nt, preferred_element_type=F32)
              + lax.dot_general(wr_lo, h_hi, nt, preferred_element_type=F32)) + br_ref[...]
    gate4, onehot = _route(logits)
    hp_ref[:, 0:D_MODEL] = h
    gate_rows = jnp.concatenate(gate4 + [jnp.zeros((128 - EXP_PER_GROUP, tm), F32)], axis=0)
    hp_ref[:, D_MODEL:] = gate_rows.T
    sel = jnp.concatenate(onehot + [jnp.zeros((8 - N_GROUPS, tm), F32)], axis=0)
    before = (lax.broadcasted_iota(jnp.int32, (tm, tm), 0)
              <= lax.broadcasted_iota(jnp.int32, (tm, tm), 1)).astype(BF16)
    seen = jnp.dot(sel.astype(BF16), before, preferred_element_type=F32)
    carried = cnt_s[:, 0:1]
    base = lax.broadcasted_iota(jnp.int32, (8, 1), 0).astype(F32) * float(cap)
    slot = jnp.sum(sel * (base + carried + seen - 1.0), axis=0, keepdims=True)
    pos_ref[0] = slot.astype(jnp.int32)
    cnt_s[...] = cnt_s[...] + jnp.sum(sel, axis=1, keepdims=True)
    cnt_ref[...] = cnt_s[...].astype(jnp.int32)


def _outproj(x, ys, w_bf16, mod, g2, wr_t, br, has_ctx):
    n = ys[0].shape[0]
    if has_ctx:
        tm = DENSE_TM
        tpb = SEQ_ALL // tm
        x_spec = pl.BlockSpec((tm, D_MODEL), lambda i: (i, 0))
    else:
        tm = CTX_LEN
        tpb = SEQ // tm
        x_spec = pl.BlockSpec((tm, D_MODEL),
                              lambda i: ((i // tpb) * (SEQ_ALL // tm) + 1 + i % tpb, 0))
    tile = lambda w: pl.BlockSpec((tm, w), lambda i: (i, 0))
    return pl.pallas_call(
        functools.partial(_outproj_kernel, tm=tm, has_ctx=has_ctx, cap=n),
        grid=(n // tm,),
        in_specs=[
            x_spec, tile(WIDTH), tile(WIDTH), tile(WIDTH), tile(WIDTH),
            pl.BlockSpec((D_MODEL, D_MODEL), lambda i: (0, 0)),
            *_mod_specs(tpb),
            pl.BlockSpec((1, D_MODEL), lambda i: (0, 0)),
            pl.BlockSpec((N_EXPERTS, D_MODEL), lambda i: (0, 0)),
            pl.BlockSpec((N_EXPERTS, 1), lambda i: (0, 0)),
        ],
        out_specs=[
            tile(D_MODEL), tile(HP_COLS),
            pl.BlockSpec((1, 1, tm), lambda i: (i, 0, 0)),
            pl.BlockSpec((8, 128), lambda i: (0, 0)),
        ],
        out_shape=[
            jax.ShapeDtypeStruct((n, D_MODEL), F32),
            jax.ShapeDtypeStruct((n, HP_COLS), F32),
            jax.ShapeDtypeStruct((n // tm, 1, tm), jnp.int32),
            jax.ShapeDtypeStruct((8, 128), jnp.int32),
        ],
        scratch_shapes=[pltpu.VMEM((8, 128), F32)],
        compiler_params=_params("arbitrary"),
        name="outproj_router",
    )(x, *ys, w_bf16, mod, mod, g2.reshape(1, D_MODEL), wr_t, br)


def _row_copies_wait(src_rows, dst_rows, sem):
    pltpu.make_async_copy(src_rows, dst_rows, sem).wait()


def _invert_kernel(pos_ref, tok_ref):
    def clear(p, carry):
        tok_ref[p] = 0
        return carry

    def put(t, carry):
        tok_ref[pos_ref[t]] = t
        return carry

    lax.fori_loop(0, tok_ref.shape[0], clear, 0, unroll=8)
    lax.fori_loop(0, pos_ref.shape[0], put, 0, unroll=8)


def _invert(pos, n_slots):
    return pl.pallas_call(
        _invert_kernel,
        in_specs=[pl.BlockSpec(memory_space=pltpu.SMEM)],
        out_specs=pl.BlockSpec(memory_space=pltpu.SMEM),
        out_shape=jax.ShapeDtypeStruct((n_slots,), jnp.int32),
        name="moe_invert",
    )(pos)


def _dispatch_kernel(tok_ref, hp_hbm, hs_ref, sem, *, tm):
    def body(p, carry):
        pltpu.make_async_copy(hp_hbm.at[pl.ds(tok_ref[0, 0, p], 1)], hs_ref.at[pl.ds(p, 1)],
                              sem).start()
        return carry

    lax.fori_loop(0, tm, body, 0, unroll=8)
    _row_copies_wait(hp_hbm.at[pl.ds(0, tm)], hs_ref, sem)


def _dispatch(tok, hp):
    n_slots = tok.shape[0]
    tm = max(t for t in range(MOE_R, DISPATCH_TM + 1, MOE_R) if n_slots % t == 0)
    return pl.pallas_call(
        functools.partial(_dispatch_kernel, tm=tm),
        grid=(n_slots // tm,),
        in_specs=[
            pl.BlockSpec((1, 1, tm), lambda i: (i, 0, 0), memory_space=pltpu.SMEM),
            pl.BlockSpec(memory_space=pl.ANY),
        ],
        out_specs=pl.BlockSpec((tm, HP_COLS), lambda i: (i, 0)),
        out_shape=jax.ShapeDtypeStruct((n_slots, HP_COLS), F32),
        scratch_shapes=[pltpu.SemaphoreType.DMA],
        compiler_params=_params("arbitrary"),
        name="moe_dispatch",
    )(tok.reshape(n_slots // tm, 1, tm), hp)


def _experts_kernel(grp_ref, valid_ref, hs_ref, w1_ref, w3_ref, w2_ref, ys_ref):
    n_valid = valid_ref[pl.program_id(0)]

    @pl.when(n_valid > 0)
    def _():
        w = hs_ref[...]
        ok = lax.broadcasted_iota(jnp.int32, (MOE_R, 1), 0) < n_valid
        h = jnp.where(ok, w[:, 0:D_MODEL], 0.0).astype(BF16)
        gates = jnp.where(ok, w[:, D_MODEL:], 0.0)
        acc = jnp.zeros((MOE_R, D_MODEL), F32)
        for e in range(EXP_PER_GROUP):
            a = jnp.dot(h, w1_ref[e], preferred_element_type=F32)
            b = jnp.dot(h, w3_ref[e], preferred_element_type=F32)
            he = (_silu(a) * b * gates[:, e:e + 1]).astype(BF16)
            acc = acc + jnp.dot(he, w2_ref[e], preferred_element_type=F32)
        ys_ref[...] = acc

    @pl.when(n_valid == 0)
    def _():
        ys_ref[...] = jnp.zeros_like(ys_ref)


def _experts(hs, blk_grp, blk_valid, w1, w3, w2, layer):
    up = pl.BlockSpec((None, EXP_PER_GROUP, D_MODEL, D_EXPERT),
                      lambda k, grp, valid: (layer, grp[k], 0, 0))
    down = pl.BlockSpec((None, EXP_PER_GROUP, D_EXPERT, D_MODEL),
                        lambda k, grp, valid: (layer, grp[k], 0, 0))
    grid_spec = pltpu.PrefetchScalarGridSpec(
        num_scalar_prefetch=2,
        grid=(hs.shape[0] // MOE_R,),
        in_specs=[pl.BlockSpec((MOE_R, HP_COLS), lambda k, grp, valid: (k, 0)), up, up, down],
        out_specs=pl.BlockSpec((MOE_R, D_MODEL), lambda k, grp, valid: (k, 0)),
    )
    return pl.pallas_call(
        _experts_kernel,
        grid_spec=grid_spec,
        out_shape=jax.ShapeDtypeStruct((hs.shape[0], D_MODEL), F32),
        compiler_params=_params("arbitrary"),
        name="moe_experts",
    )(blk_grp, blk_valid, hs, w1, w3, w2)


def _combine_kernel(pos_ref, x_ref, modl_ref, modc_ref, fg_ref, ys_hbm, o_ref, buf, sem,
                    *, tm, has_ctx, final_norm):
    def body(t, carry):
        pltpu.make_async_copy(ys_hbm.at[pl.ds(pos_ref[0, 0, t], 1)], buf.at[pl.ds(t, 1)],
                              sem).start()
        return carry

    lax.fori_loop(0, tm, body, 0, unroll=8)
    _row_copies_wait(ys_hbm.at[pl.ds(0, tm)], buf, sem)
    pick = _mod_picker(modl_ref, modc_ref, tm, has_ctx)
    x = x_ref[...] + pick(5) * buf[...]
    o_ref[...] = _rms(x, fg_ref[...]) if final_norm else x


def _combine(pos, x, ys, mod, final_g, has_ctx, final_norm):
    n = x.shape[0]
    tm = COMBINE_TM if has_ctx else COMBINE_TM_LATENT
    tpb = (SEQ_ALL if has_ctx else SEQ) // tm
    return pl.pallas_call(
        functools.partial(_combine_kernel, tm=tm, has_ctx=has_ctx, final_norm=final_norm),
        grid=(n // tm,),
        in_specs=[
            pl.BlockSpec((1, 1, tm), lambda i: (i, 0, 0), memory_space=pltpu.SMEM),
            pl.BlockSpec((tm, D_MODEL), lambda i: (i, 0)),
            *_mod_specs(tpb),
            pl.BlockSpec((1, D_MODEL), lambda i: (0, 0)),
            pl.BlockSpec(memory_space=pl.ANY),
        ],
        out_specs=pl.BlockSpec((tm, D_MODEL), lambda i: (i, 0)),
        out_shape=jax.ShapeDtypeStruct((n, D_MODEL), F32),
        scratch_shapes=[pltpu.VMEM((tm, D_MODEL), F32), pltpu.SemaphoreType.DMA],
        compiler_params=_params("arbitrary"),
        name="moe_combine",
    )(pos.reshape(n // tm, 1, tm), x, mod, mod, final_g.reshape(1, D_MODEL), ys)


def _routing_tables(slot, counts, n):
    n_blk = n // MOE_R + N_GROUPS
    n_g = counts[:N_GROUPS, 0]
    nb = (n_g + MOE_R - 1) // MOE_R
    ends = jnp.cumsum(nb)
    starts = ends - nb
    g_tok = slot // n
    pos = slot - g_tok * n + starts[g_tok] * MOE_R
    k = jnp.arange(n_blk, dtype=jnp.int32)
    g = jnp.minimum(jnp.sum((k[:, None] >= ends[None, :]).astype(jnp.int32), axis=1), N_GROUPS - 1)
    valid = jnp.where(k < ends[-1], jnp.clip(n_g[g] - (k - starts[g]) * MOE_R, 0, MOE_R), 0)
    return pos.astype(jnp.int32), g.astype(jnp.int32), valid.astype(jnp.int32)


def _moe(hp, slot, counts, x, w1, w3, w2, layer, mod, final_g, has_ctx, final_norm):
    n = x.shape[0]
    assert n % MOE_R == 0
    pos, blk_grp, blk_valid = _routing_tables(slot.reshape(n), counts, n)
    tok = _invert(pos, n + N_GROUPS * MOE_R)
    hs = _dispatch(tok, hp)
    ys = _experts(hs, blk_grp, blk_valid, w1, w3, w2, layer)
    return _combine(pos, x, ys, mod, final_g, has_ctx, final_norm)


def _block_diag(w):
    eye = jnp.eye(HEADS, dtype=w.dtype)
    return jnp.einsum('hij,hg->higj', w, eye).reshape(WIDTH, WIDTH)


def _rope_tables():
    rows = SEQ // GRID_W
    row_ids = jnp.repeat(jnp.arange(rows, dtype=F32), GRID_W)
    col_ids = jnp.tile(jnp.arange(GRID_W, dtype=F32), rows)
    n_freq = DA_QD // 4
    freqs = ROPE_THETA ** (-jnp.arange(n_freq, dtype=F32) / n_freq)
    ang = jnp.stack([row_ids[:, None] * freqs, col_ids[:, None] * freqs], axis=1)
    cos = jnp.cos(ang)
    sin = jnp.sin(ang)
    cos_l = jnp.broadcast_to(cos[:, None, None, :, None, :], (SEQ, HEADS, 2, 2, 2, n_freq))
    sin_l = jnp.broadcast_to(sin[:, None, None, :, None, :], (SEQ, HEADS, 2, 2, 2, n_freq))
    sign = jnp.array([-1.0, 1.0], F32)[None, None, None, None, :, None]
    cos_l = cos_l.reshape(SEQ, WIDTH)
    sin_l = (sin_l * sign).reshape(SEQ, WIDTH)
    cos_all = jnp.concatenate([jnp.ones((CTX_LEN, WIDTH), F32), cos_l], axis=0)
    sin_all = jnp.concatenate([jnp.zeros((CTX_LEN, WIDTH), F32), sin_l], axis=0)
    return cos_all, sin_all


def kernel(x, c, ctx, c_ctx, w_ada, b_ada, norm1_g, norm2_g, w_in, w_out, lru_conv_w, lru_conv_b,
           lru_wr, lru_br, lru_wi, lru_bi, lru_lam, da_lam, da_subln_g, sg_norm_g, sg_w, sg_b,
           hg_lb, hg_norm_g, router_w, router_b, moe_w1, moe_w3, moe_w2, final_norm_g):
    nb = x.shape[0]
    assert nb <= CTX_ROW and x.shape[1:] == (SEQ, D_MODEL) and ctx.shape[1:] == (CTX_LEN, D_MODEL)
    xs = jnp.concatenate([ctx, x], axis=1).reshape(nb * SEQ_ALL, D_MODEL)

    cvec = jnp.zeros((ADA_ROWS, D_MODEL), F32).at[:nb].set(c).at[CTX_ROW].set(c_ctx)
    mods = _ada(cvec, w_ada, b_ada).reshape(DEPTH, ADA_ROWS, 6, D_MODEL)

    cos_all, sin_all = _rope_tables()
    lb_cum = jnp.cumsum(jax.nn.softmax(hg_lb.astype(F32), axis=1), axis=1)
    lb_all = lb_cum - lb_cum[:, :1]
    wr_t = router_w.T
    br = router_b.reshape(N_EXPERTS, 1)
    w1_bf16, w3_bf16, w2_bf16 = (w.astype(BF16) for w in (moe_w1, moe_w3, moe_w2))

    for l in range(DEPTH):
        last = l == DEPTH - 1
        out_rows = SEQ if last else SEQ_ALL
        z = _inproj(xs, norm1_g[l], mods[l], w_in[l].astype(BF16))

        ya = _lru(z, lru_conv_w[l], lru_conv_b[l].reshape(1, WIDTH),
                  jax.vmap(_block_diag)(lru_wr[l]).astype(BF16), lru_br[l],
                  jax.vmap(_block_diag)(lru_wi[l]).astype(BF16), lru_bi[l], lru_lam[l], out_rows)

        lam_init = 0.8 - 0.6 * math.exp(-0.3 * l)
        lf = da_lam[l].astype(F32)
        lam = jnp.exp(jnp.sum(lf[0] * lf[1])) - jnp.exp(jnp.sum(lf[2] * lf[3])) + lam_init
        yb = _attn(z, lam.reshape(1, 1), cos_all, sin_all,
                   jnp.tile(da_subln_g[l], HEADS).reshape(1, WIDTH),
                   with_ctx=not last, out_scale=1.0 - lam_init)

        w_cat = jnp.transpose(sg_w[l], (1, 0, 2)).reshape(SG_CHUNK, HEADS * SG_CHUNK).astype(BF16)
        bias2d = jnp.repeat(sg_b[l].T, HEAD_DIM, axis=1)
        ys = _sgu(z, sg_norm_g[l].reshape(1, WIDTH), w_cat, bias2d, out_rows)

        yh = _hgrn(z, lb_all[:, l], hg_norm_g[l].reshape(1, WIDTH), out_rows)

        xs, hp, pos, counts = _outproj(xs, (ya, yb, ys, yh), w_out[l].astype(BF16), mods[l],
                                       norm2_g[l], wr_t, br, has_ctx=not last)
        xs = _moe(hp, pos, counts, xs, w1_bf16, w3_bf16, w2_bf16, l, mods[l], final_norm_g,
                  has_ctx=not last, final_norm=last)

    return xs.reshape(nb, SEQ, D_MODEL)
```

```python
import functools
import math

import jax
import jax.numpy as jnp
from jax import lax
from jax.experimental import pallas as pl
from jax.experimental.pallas import tpu as pltpu

F32 = jnp.float32
BF16 = jnp.bfloat16

D_MODEL = 1024
SEQ = 2048
CTX_LEN = 256
SEQ_ALL = CTX_LEN + SEQ
DEPTH = 2
GRID_W = 64
EPS = 1e-6
WIDTH = 256
HEADS = 4
HEAD_DIM = 64
CONV_W = 4
RG_C = 8.0
DA_QD = 32
ROPE_THETA = 10000.0
SG_CHUNK = 128
N_EXPERTS = 16
N_GROUPS = 4
EXP_PER_GROUP = 4
D_EXPERT = 512
IN_COLS = 3072
COL_A_X, COL_A_G = 0, 1
COL_B_Q, COL_B_K, COL_B_V = 2, 3, 4
COL_C_U, COL_C_V = 5, 6
COL_D_Q, COL_D_FF, COL_D_FB, COL_D_I, COL_D_G = 7, 8, 9, 10, 11

ADA_ROWS = 16
CTX_ROW = 8
VMEM_LIMIT = 56 * 1024 * 1024

LRU_BLK = 8
HG_CHUNK = 16
ATT_TQ = 256
DENSE_TM = 768
MOE_R = 512
HP_COLS = D_MODEL + 128
COMBINE_TM = 1152
COMBINE_TM_LATENT = 1024


def _params(*sem):
    return pltpu.CompilerParams(dimension_semantics=sem, vmem_limit_bytes=VMEM_LIMIT)


def _rms(xf, g):
    return xf * lax.rsqrt(jnp.mean(xf * xf, axis=-1, keepdims=True) + EPS) * g


def _sigmoid(x):
    return 1.0 / (1.0 + jnp.exp(-x))


def _silu(x):
    return x * _sigmoid(x)


def _gelu(x):
    return jax.nn.gelu(x)


def _split3(x):
    hi = x.astype(BF16)
    r = x - hi.astype(F32)
    mid = r.astype(BF16)
    lo = (r - mid.astype(F32)).astype(BF16)
    return hi, mid, lo


def _dot_f32_rhs(m_bf16, x):
    return functools.reduce(jnp.add, [jnp.dot(m_bf16, p, preferred_element_type=F32)
                                      for p in _split3(x)])


def _dot_f32_lhs(x, m_bf16):
    return functools.reduce(jnp.add, [jnp.dot(p, m_bf16, preferred_element_type=F32)
                                      for p in _split3(x)])


def _head_ones(dtype):
    r = lax.broadcasted_iota(jnp.int32, (WIDTH, WIDTH), 0) // HEAD_DIM
    c = lax.broadcasted_iota(jnp.int32, (WIDTH, WIDTH), 1) // HEAD_DIM
    return (r == c).astype(dtype)


def _mod_picker(modl_ref, modc_ref, tm, has_ctx):
    ml = modl_ref[0]
    if not has_ctx:
        return lambda r: ml[r:r + 1]
    mc = modc_ref[0]
    row0 = (pl.program_id(0) % (SEQ_ALL // tm)) * tm
    is_ctx = row0 + lax.broadcasted_iota(jnp.int32, (tm, 1), 0) < CTX_LEN
    return lambda r: jnp.where(is_ctx, mc[r:r + 1], ml[r:r + 1])


def _ada_kernel(c_ref, w_ref, b_ref, o_ref):
    s = _silu(c_ref[...]).astype(BF16)
    o_ref[...] = jnp.dot(s, w_ref[...].astype(BF16), preferred_element_type=F32) + b_ref[...]


def _ada(cvec, w_ada, b_ada):
    tn = 1536
    return pl.pallas_call(
        _ada_kernel,
        grid=(DEPTH, 6 * D_MODEL // tn),
        in_specs=[
            pl.BlockSpec((ADA_ROWS, D_MODEL), lambda l, j: (0, 0)),
            pl.BlockSpec((None, D_MODEL, tn), lambda l, j: (l, 0, j)),
            pl.BlockSpec((None, 1, tn), lambda l, j: (l, 0, j)),
        ],
        out_specs=pl.BlockSpec((None, ADA_ROWS, tn), lambda l, j: (l, 0, j)),
        out_shape=jax.ShapeDtypeStruct((DEPTH, ADA_ROWS, 6 * D_MODEL), F32),
        compiler_params=_params("arbitrary", "arbitrary"),
        name="ada",
    )(cvec, w_ada, b_ada.reshape(DEPTH, 1, 6 * D_MODEL))


def _inproj_kernel(x_ref, g_ref, modl_ref, modc_ref, w_ref, z_ref, *, tm):
    pick = _mod_picker(modl_ref, modc_ref, tm, True)
    h = _rms(x_ref[...], g_ref[...]) * (1.0 + pick(1)) + pick(0)
    z_ref[...] = jnp.dot(h.astype(BF16), w_ref[...], preferred_element_type=F32)


def _mod_specs(tiles_per_batch):
    return [
        pl.BlockSpec((1, 6, D_MODEL), lambda i, *_: (i // tiles_per_batch, 0, 0)),
        pl.BlockSpec((1, 6, D_MODEL), lambda i, *_: (CTX_ROW, 0, 0)),
    ]


def _inproj(x, g, mod, w_bf16):
    n = x.shape[0]
    tm = DENSE_TM
    return pl.pallas_call(
        functools.partial(_inproj_kernel, tm=tm),
        grid=(n // tm,),
        in_specs=[
            pl.BlockSpec((tm, D_MODEL), lambda i: (i, 0)),
            pl.BlockSpec((1, D_MODEL), lambda i: (0, 0)),
            *_mod_specs(SEQ_ALL // tm),
            pl.BlockSpec((D_MODEL, IN_COLS), lambda i: (0, 0)),
        ],
        out_specs=pl.BlockSpec((tm, IN_COLS), lambda i: (i, 0)),
        out_shape=jax.ShapeDtypeStruct((n, IN_COLS), F32),
        compiler_params=_params("arbitrary"),
        name="inproj",
    )(x, g.reshape(1, D_MODEL), mod, mod, w_bf16)


def _lru_kernel(x_ref, gate_ref, cw_ref, cb_ref, wr_ref, br_ref, wi_ref, bi_ref, lam_ref,
                y_ref, a_s, b_s, h_s, *, out_rows):
    x = x_ref[...]
    rows = lax.broadcasted_iota(jnp.int32, (SEQ_ALL, 1), 0)
    seg = rows < CTX_LEN
    u = jnp.zeros_like(x)
    for j in range(CONV_W):
        off = j - CONV_W // 2
        xs = x if off == 0 else pltpu.roll(x, (-off) % SEQ_ALL, 0)
        src = rows + off
        ok = (src >= 0) & (src < SEQ_ALL) & ((src < CTX_LEN) == seg)
        u = u + jnp.where(ok, xs, 0.0) * cw_ref[j:j + 1, :]
    u = u + cb_ref[...]
    ub = u.astype(BF16)
    for d in range(2):
        r = _sigmoid(jnp.dot(ub, wr_ref[d], preferred_element_type=F32) + br_ref[d:d + 1, :])
        i = _sigmoid(jnp.dot(ub, wi_ref[d], preferred_element_type=F32) + bi_ref[d:d + 1, :])
        nl = -lam_ref[d:d + 1, :]
        softplus = jnp.maximum(nl, 0.0) + jnp.log(1.0 + jnp.exp(-jnp.abs(nl)))
        log_a = -RG_C * r * softplus
        a_s[d] = jnp.exp(log_a)
        b_s[d] = jnp.sqrt(1.0 - jnp.exp(2.0 * log_a)) * i * u

    n_blk = SEQ_ALL // LRU_BLK
    n_ctx_blk = CTX_LEN // LRU_BLK
    sub = lax.broadcasted_iota(jnp.int32, (LRU_BLK, 1), 0)

    def block_scan(a, b, reverse):
        s = 1
        while s < LRU_BLK:
            if reverse:
                a_sh = pltpu.roll(a, LRU_BLK - s, 0)
                b_sh = pltpu.roll(b, LRU_BLK - s, 0)
                ok = sub < LRU_BLK - s
            else:
                a_sh = pltpu.roll(a, s, 0)
                b_sh = pltpu.roll(b, s, 0)
                ok = sub >= s
            b = jnp.where(ok, a * b_sh + b, b)
            a = jnp.where(ok, a * a_sh, a)
            s *= 2
        return a, b

    def body(n, carry):
        hf, hb = carry
        rf = pl.multiple_of(n * LRU_BLK, LRU_BLK)
        af, bf = block_scan(a_s[0, pl.ds(rf, LRU_BLK), :], b_s[0, pl.ds(rf, LRU_BLK), :], False)
        h = af * hf + bf
        h_s[0, pl.ds(rf, LRU_BLK), :] = h
        hf = h[LRU_BLK - 1:LRU_BLK, :]
        nb = jnp.where(n < n_ctx_blk, n_ctx_blk - 1 - n, n_blk + n_ctx_blk - 1 - n)
        rb = pl.multiple_of(nb * LRU_BLK, LRU_BLK)
        ab, bb = block_scan(a_s[1, pl.ds(rb, LRU_BLK), :], b_s[1, pl.ds(rb, LRU_BLK), :], True)
        h = ab * hb + bb
        h_s[1, pl.ds(rb, LRU_BLK), :] = h
        hb = h[0:1, :]
        return hf, hb

    zero = jnp.zeros((1, WIDTH), F32)
    lax.fori_loop(0, n_blk, body, (zero, zero))
    first = SEQ_ALL - out_rows
    y_ref[...] = (h_s[0, first:, :] + h_s[1, first:, :]) * _gelu(gate_ref[first:, :])


def _lru(z, cw, cb, wr_bd, br, wi_bd, bi, lam, out_rows):
    nb = z.shape[0] // SEQ_ALL
    full = lambda shape: pl.BlockSpec(shape, lambda b: (0,) * len(shape))
    return pl.pallas_call(
        functools.partial(_lru_kernel, out_rows=out_rows),
        grid=(nb,),
        in_specs=[
            pl.BlockSpec((SEQ_ALL, WIDTH), lambda b: (b, COL_A_X)),
            pl.BlockSpec((SEQ_ALL, WIDTH), lambda b: (b, COL_A_G)),
            full((CONV_W, WIDTH)), full((1, WIDTH)),
            full((2, WIDTH, WIDTH)), full((2, WIDTH)),
            full((2, WIDTH, WIDTH)), full((2, WIDTH)),
            full((2, WIDTH)),
        ],
        out_specs=pl.BlockSpec((out_rows, WIDTH), lambda b: (b, 0)),
        out_shape=jax.ShapeDtypeStruct((nb * out_rows, WIDTH), F32),
        scratch_shapes=[
            pltpu.VMEM((2, SEQ_ALL, WIDTH), F32),
            pltpu.VMEM((2, SEQ_ALL, WIDTH), F32),
            pltpu.VMEM((2, SEQ_ALL, WIDTH), F32),
        ],
        compiler_params=_params("arbitrary"),
        name="rglru",
    )(z, z, cw, cb, wr_bd, br, wi_bd, bi, lam)


def _rope(x, cos, sin_signed):
    lane = lax.broadcasted_iota(jnp.int32, (1, WIDTH), 1)
    first_half = (lane % 16) < 8
    partner = jnp.where(first_half, pltpu.roll(x, WIDTH - 8, 1), pltpu.roll(x, 8, 1))
    return x * cos + partner * sin_signed


def _attn_kernel(lam_ref, q_ref, k_ref, v_ref, cos_ref, sin_ref, g_ref, y_ref, kt_s, v_s,
                 *, with_ctx, out_scale):
    j = pl.program_id(1)

    @pl.when(j == 0)
    def _():
        kr = _rope(k_ref[...], cos_ref[...], sin_ref[...])
        kt_s[...] = kr.T.astype(BF16)
        v = v_ref[...]
        ones = jnp.ones((SEQ_ALL, HEAD_DIM), F32)
        for h in range(HEADS):
            v_s[h] = jnp.concatenate([v[:, h * HEAD_DIM:(h + 1) * HEAD_DIM], ones],
                                     axis=1).astype(BF16)

    lam = lam_ref[0, 0]
    tile = j if with_ctx else j + 1
    row0 = pl.multiple_of(tile * ATT_TQ, ATT_TQ)
    q = _rope(q_ref[...], cos_ref[pl.ds(row0, ATT_TQ), :], sin_ref[pl.ds(row0, ATT_TQ), :])
    q = q * (DA_QD ** -0.5)

    def attend(n_keys):
        outs = []
        for h in range(HEADS):
            branch = []
            for n in range(2):
                c0 = h * HEAD_DIM + n * DA_QD
                s = jnp.dot(q[:, c0:c0 + DA_QD].astype(BF16), kt_s[c0:c0 + DA_QD, 0:n_keys],
                            preferred_element_type=F32)
                p = jnp.exp(s - jnp.max(s, axis=-1, keepdims=True)).astype(BF16)
                ov = jnp.dot(p, v_s[h, 0:n_keys, :], preferred_element_type=F32)
                branch.append(ov[:, 0:HEAD_DIM] * (1.0 / ov[:, HEAD_DIM:HEAD_DIM + 1]))
            o = branch[0] - lam * branch[1]
            o = o * lax.rsqrt(jnp.mean(o * o, axis=-1, keepdims=True) + EPS)
            outs.append(o)
        y_ref[...] = jnp.concatenate(outs, axis=-1) * g_ref[...] * out_scale

    if with_ctx:
        @pl.when(j == 0)
        def _():
            attend(CTX_LEN)

        @pl.when(j > 0)
        def _():
            attend(SEQ_ALL)
    else:
        attend(SEQ_ALL)


def _attn(z, lam, cos, sin_signed, sub_g4, with_ctx, out_scale):
    nb = z.shape[0] // SEQ_ALL
    tiles = SEQ_ALL // ATT_TQ
    nq = tiles if with_ctx else tiles - 1
    first = 0 if with_ctx else 1
    return pl.pallas_call(
        functools.partial(_attn_kernel, with_ctx=with_ctx, out_scale=out_scale),
        grid=(nb, nq),
        in_specs=[
            pl.BlockSpec(memory_space=pltpu.SMEM),
            pl.BlockSpec((ATT_TQ, WIDTH), lambda b, j: (b * tiles + j + first, COL_B_Q)),
            pl.BlockSpec((SEQ_ALL, WIDTH), lambda b, j: (b, COL_B_K)),
            pl.BlockSpec((SEQ_ALL, WIDTH), lambda b, j: (b, COL_B_V)),
            pl.BlockSpec((SEQ_ALL, WIDTH), lambda b, j: (0, 0)),
            pl.BlockSpec((SEQ_ALL, WIDTH), lambda b, j: (0, 0)),
            pl.BlockSpec((1, WIDTH), lambda b, j: (0, 0)),
        ],
        out_specs=pl.BlockSpec((ATT_TQ, WIDTH), lambda b, j: (b * nq + j, 0)),
        out_shape=jax.ShapeDtypeStruct((nb * nq * ATT_TQ, WIDTH), F32),
        scratch_shapes=[
            pltpu.VMEM((WIDTH, SEQ_ALL), BF16),
            pltpu.VMEM((HEADS, SEQ_ALL, 2 * HEAD_DIM), BF16),
        ],
        compiler_params=_params("arbitrary", "arbitrary"),
        name="diffattn",
    )(lam, z, z, z, cos, sin_signed, sub_g4)


def _sgu_kernel(u_ref, v_ref, g_ref, w_ref, b_ref, y_ref, *, out_rows):
    row_head = lax.broadcasted_iota(jnp.int32, (HEADS * SG_CHUNK, WIDTH), 0) // SG_CHUNK
    col_head = lax.broadcasted_iota(jnp.int32, (HEADS * SG_CHUNK, WIDTH), 1) // HEAD_DIM
    head_mask = row_head == col_head
    w = w_ref[...]
    bias = b_ref[...]
    g = g_ref[...]
    first = SEQ_ALL - out_rows
    for n in range(first // SG_CHUNK, SEQ_ALL // SG_CHUNK):
        rows = slice(n * SG_CHUNK, (n + 1) * SG_CHUNK)
        vn = _rms(_gelu(v_ref[rows, :]), g).astype(BF16)
        stacked = jnp.where(head_mask, jnp.concatenate([vn] * HEADS, axis=0), jnp.zeros((), BF16))
        vm = jnp.dot(w, stacked, preferred_element_type=F32) + bias
        y_ref[n * SG_CHUNK - first:(n + 1) * SG_CHUNK - first, :] = _gelu(u_ref[rows, :]) * vm


def _sgu(z, norm_g, w_cat, bias2d, out_rows):
    nb = z.shape[0] // SEQ_ALL
    return pl.pallas_call(
        functools.partial(_sgu_kernel, out_rows=out_rows),
        grid=(nb,),
        in_specs=[
            pl.BlockSpec((SEQ_ALL, WIDTH), lambda b: (b, COL_C_U)),
            pl.BlockSpec((SEQ_ALL, WIDTH), lambda b: (b, COL_C_V)),
            pl.BlockSpec((1, WIDTH), lambda b: (0, 0)),
            pl.BlockSpec((SG_CHUNK, HEADS * SG_CHUNK), lambda b: (0, 0)),
            pl.BlockSpec((SG_CHUNK, WIDTH), lambda b: (0, 0)),
        ],
        out_specs=pl.BlockSpec((out_rows, WIDTH), lambda b: (b, 0)),
        out_shape=jax.ShapeDtypeStruct((nb * out_rows, WIDTH), F32),
        compiler_params=_params("arbitrary"),
        name="sgu",
    )(z, z, norm_g, w_cat, bias2d)


def _hgrn_kernel(q_ref, ff_ref, fb_ref, i_ref, gate_ref, lb_ref, gn_ref, y_ref,
                 b_s, k_s, o_s, st_s, stb_s, *, out_rows):
    c = HG_CHUNK
    pre = 128
    ones_bf = _head_ones(BF16)
    pr = lax.broadcasted_iota(jnp.int32, (pre, pre), 0)
    pc = lax.broadcasted_iota(jnp.int32, (pre, pre), 1)
    same = (pr // c) == (pc // c)
    tri = ((same & (pc <= pr)).astype(BF16), (same & (pc >= pr)).astype(BF16))

    for d, f_ref in enumerate((ff_ref, fb_ref)):
        lb = lb_ref[d:d + 1, :]
        for n in range(SEQ_ALL // pre):
            rows = slice(n * pre, (n + 1) * pre)
            f = lb + (1.0 - lb) * _sigmoid(f_ref[rows, :])
            k_s[d, rows, :] = 1.0 - f
            b_s[d, rows, :] = _dot_f32_rhs(tri[d], jnp.log(f))
    st_s[...] = jnp.zeros_like(st_s)
    stb_s[...] = jnp.zeros_like(stb_s)

    n_chunks = SEQ_ALL // c
    n_ctx = CTX_LEN // c
    half = c // 2
    sub = lax.broadcasted_iota(jnp.int32, (half, 1), 0)
    lane_head = lax.broadcasted_iota(jnp.int32, (1, 128), 1) // HEAD_DIM
    scale = HEAD_DIM ** -0.5

    def one_direction(d, chunk):
        r0 = pl.multiple_of(chunk * c, c)
        q = q_ref[pl.ds(r0, c), :] * scale
        v = i_ref[pl.ds(r0, c), :]
        b = b_s[d, pl.ds(r0, c), :]
        k = k_s[d, pl.ds(r0, c), :]
        tot = b[c - 1:c, :] if d == 0 else b[0:1, :]
        o = lax.dot_general((q * jnp.exp(b)).astype(BF16), stb_s[d],
                            (((1,), (1,)), ((), ())), preferred_element_type=F32)
        parts = []
        for s in range(c):
            for tile in range(2):
                lo_row, hi_row = tile * half, (tile + 1) * half
                if (d == 0 and hi_row <= s) or (d == 1 and lo_row > s):
                    parts.append(jnp.zeros((half, WIDTH), F32))
                    continue
                diff = b[lo_row:hi_row, :] - b[s:s + 1, :]
                if d == 0 and lo_row < s:
                    diff = jnp.where(sub + lo_row >= s, diff, -jnp.inf)
                elif d == 1 and hi_row - 1 > s:
                    diff = jnp.where(sub + lo_row <= s, diff, -jnp.inf)
                parts.append(q[lo_row:hi_row, :] * k[s:s + 1, :] * jnp.exp(diff))
        att = jnp.dot(jnp.concatenate(parts, axis=0).astype(BF16), ones_bf,
                      preferred_element_type=F32)
        for s in range(c):
            o = o + att[s * c:(s + 1) * c, :] * v[s:s + 1, :]
        o_s[d, pl.ds(r0, c), :] = o
        ke = (k * jnp.exp(tot - b)).astype(BF16)
        upd = lax.dot_general(v.astype(BF16), ke, (((0,), (0,)), ((), ())),
                              preferred_element_type=F32)
        decay = jnp.exp(tot)
        for h in range(HEADS):
            r = slice(h * HEAD_DIM, (h + 1) * HEAD_DIM)
            ct = slice((h // 2) * 128, (h // 2 + 1) * 128)
            new = st_s[d, r, ct] * decay[:, ct] + jnp.where(lane_head == h % 2, upd[r, ct], 0.0)
            st_s[d, r, ct] = new
            stb_s[d, r, ct] = new.astype(BF16)

    def body(n, carry):
        one_direction(0, n)
        one_direction(1, jnp.where(n < n_ctx, n_ctx - 1 - n, n_chunks + n_ctx - 1 - n))
        return carry

    lax.fori_loop(0, n_chunks, body, 0, unroll=4)

    first = SEQ_ALL - out_rows
    o = o_s[0, first:, :] + o_s[1, first:, :]
    ms = _dot_f32_lhs(o * o, ones_bf) * (1.0 / HEAD_DIM)
    y_ref[...] = o * lax.rsqrt(ms + EPS) * gn_ref[...] * _silu(gate_ref[first:, :])


def _hgrn(z, lb, gn, out_rows):
    nb = z.shape[0] // SEQ_ALL
    col = lambda cidx: pl.BlockSpec((SEQ_ALL, WIDTH), lambda b: (b, cidx))
    return pl.pallas_call(
        functools.partial(_hgrn_kernel, out_rows=out_rows),
        grid=(nb,),
        in_specs=[
            col(COL_D_Q), col(COL_D_FF), col(COL_D_FB), col(COL_D_I), col(COL_D_G),
            pl.BlockSpec((2, WIDTH), lambda b: (0, 0)),
            pl.BlockSpec((1, WIDTH), lambda b: (0, 0)),
        ],
        out_specs=pl.BlockSpec((out_rows, WIDTH), lambda b: (b, 0)),
        out_shape=jax.ShapeDtypeStruct((nb * out_rows, WIDTH), F32),
        scratch_shapes=[
            pltpu.VMEM((2, SEQ_ALL, WIDTH), F32),
            pltpu.VMEM((2, SEQ_ALL, WIDTH), F32),
            pltpu.VMEM((2, SEQ_ALL, WIDTH), F32),
            pltpu.VMEM((2, WIDTH, WIDTH), F32),
            pltpu.VMEM((2, WIDTH, WIDTH), BF16),
        ],
        compiler_params=_params("arbitrary"),
        name="hgrn2",
    )(z, z, z, z, z, lb, gn)


def _route(logits):
    rows = [logits[e:e + 1, :] for e in range(N_EXPERTS)]
    m = functools.reduce(jnp.maximum, rows)
    ex = [jnp.exp(r - m) for r in rows]
    inv = 1.0 / functools.reduce(jnp.add, ex)
    sc = [e * inv for e in ex]
    g_score = []
    for g in range(N_GROUPS):
        grp = sc[g * EXP_PER_GROUP:(g + 1) * EXP_PER_GROUP]
        pairs = [grp[a] + grp[b] for a in range(EXP_PER_GROUP) for b in range(a + 1, EXP_PER_GROUP)]
        g_score.append(functools.reduce(jnp.maximum, pairs))
    gate4 = [jnp.zeros_like(sc[0]) for _ in range(EXP_PER_GROUP)]
    onehot = []
    for g in range(N_GROUPS):
        g_ok = None
        for o in range(N_GROUPS):
            if o == g:
                continue
            t = (g_score[g] > g_score[o]) if o < g else (g_score[g] >= g_score[o])
            g_ok = t if g_ok is None else (g_ok & t)
        grp = sc[g * EXP_PER_GROUP:(g + 1) * EXP_PER_GROUP]
        picked = []
        for a in range(EXP_PER_GROUP):
            beaten = jnp.zeros_like(grp[a])
            for o in range(EXP_PER_GROUP):
                if o == a:
                    continue
                t = (grp[o] >= grp[a]) if o < a else (grp[o] > grp[a])
                beaten = beaten + jnp.where(t, 1.0, 0.0)
            picked.append(jnp.where((beaten < 1.5) & g_ok, grp[a], 0.0))
        denom = functools.reduce(jnp.add, picked)
        denom = jnp.where(g_ok, denom, 1.0)
        gate4 = [acc + p / denom for acc, p in zip(gate4, picked)]
        onehot.append(jnp.where(g_ok, 1.0, 0.0))
    return gate4, onehot


def _outproj_kernel(x_ref, ya_ref, yb_ref, ys_ref, yh_ref, w_ref, modl_ref, modc_ref, g_ref,
                    wr_ref, br_ref, xo_ref, hp_ref, pos_ref, cnt_ref, cnt_s, *, tm, has_ctx, cap):
    @pl.when(pl.program_id(0) == 0)
    def _():
        cnt_s[...] = jnp.zeros_like(cnt_s)

    pick = _mod_picker(modl_ref, modc_ref, tm, has_ctx)
    acc = jnp.zeros((tm, D_MODEL), F32)
    for kblk, y_ref in enumerate((ya_ref, yb_ref, ys_ref, yh_ref)):
        acc = acc + jnp.dot(y_ref[...].astype(BF16), w_ref[kblk * WIDTH:(kblk + 1) * WIDTH, :],
                            preferred_element_type=F32)
    x = x_ref[...] + pick(2) * acc
    xo_ref[...] = x
    h = _rms(x, g_ref[...]) * (1.0 + pick(4)) + pick(3)
    nt = (((1,), (1,)), ((), ()))
    wr = wr_ref[...]
    wr_hi = wr.astype(BF16)
    wr_lo = (wr - wr_hi.astype(F32)).astype(BF16)
    h_hi = h.astype(BF16)
    h_lo = (h - h_hi.astype(F32)).astype(BF16)
    logits = (lax.dot_general(wr_hi, h_hi, nt, preferred_element_type=F32)
              + lax.dot_general(wr_hi, h_lo, nt, preferred_element_type=F32)
              + lax.dot_general(wr_lo, h_hi, nt, preferred_element_type=F32)) + br_ref[...]
    gate4, onehot = _route(logits)
    hp_ref[:, 0:D_MODEL] = h
    gate_rows = jnp.concatenate(gate4 + [jnp.zeros((128 - EXP_PER_GROUP, tm), F32)], axis=0)
    hp_ref[:, D_MODEL:] = gate_rows.T
    sel = jnp.concatenate(onehot + [jnp.zeros((8 - N_GROUPS, tm), F32)], axis=0)
    before = (lax.broadcasted_iota(jnp.int32, (tm, tm), 0)
              <= lax.broadcasted_iota(jnp.int32, (tm, tm), 1)).astype(BF16)
    seen = jnp.dot(sel.astype(BF16), before, preferred_element_type=F32)
    carried = cnt_s[:, 0:1]
    base = lax.broadcasted_iota(jnp.int32, (8, 1), 0).astype(F32) * float(cap)
    slot = jnp.sum(sel * (base + carried + seen - 1.0), axis=0, keepdims=True)
    pos_ref[0] = slot.astype(jnp.int32)
    cnt_s[...] = cnt_s[...] + jnp.sum(sel, axis=1, keepdims=True)
    cnt_ref[...] = cnt_s[...].astype(jnp.int32)


def _outproj(x, ys, w_bf16, mod, g2, wr_t, br, has_ctx):
    n = ys[0].shape[0]
    if has_ctx:
        tm = DENSE_TM
        tpb = SEQ_ALL // tm
        x_spec = pl.BlockSpec((tm, D_MODEL), lambda i: (i, 0))
    else:
        tm = CTX_LEN
        tpb = SEQ // tm
        x_spec = pl.BlockSpec((tm, D_MODEL),
                              lambda i: ((i // tpb) * (SEQ_ALL // tm) + 1 + i % tpb, 0))
    tile = lambda w: pl.BlockSpec((tm, w), lambda i: (i, 0))
    return pl.pallas_call(
        functools.partial(_outproj_kernel, tm=tm, has_ctx=has_ctx, cap=n),
        grid=(n // tm,),
        in_specs=[
            x_spec, tile(WIDTH), tile(WIDTH), tile(WIDTH), tile(WIDTH),
            pl.BlockSpec((D_MODEL, D_MODEL), lambda i: (0, 0)),
            *_mod_specs(tpb),
            pl.BlockSpec((1, D_MODEL), lambda i: (0, 0)),
            pl.BlockSpec((N_EXPERTS, D_MODEL), lambda i: (0, 0)),
            pl.BlockSpec((N_EXPERTS, 1), lambda i: (0, 0)),
        ],
        out_specs=[
            tile(D_MODEL), tile(HP_COLS),
            pl.BlockSpec((1, 1, tm), lambda i: (i, 0, 0)),
            pl.BlockSpec((8, 128), lambda i: (0, 0)),
        ],
        out_shape=[
            jax.ShapeDtypeStruct((n, D_MODEL), F32),
            jax.ShapeDtypeStruct((n, HP_COLS), F32),
            jax.ShapeDtypeStruct((n // tm, 1, tm), jnp.int32),
            jax.ShapeDtypeStruct((8, 128), jnp.int32),
        ],
        scratch_shapes=[pltpu.VMEM((8, 128), F32)],
        compiler_params=_params("arbitrary"),
        name="outproj_router",
    )(x, *ys, w_bf16, mod, mod, g2.reshape(1, D_MODEL), wr_t, br)


def _row_copies_wait(src_rows, dst_rows, sem):
    pltpu.make_async_copy(src_rows, dst_rows, sem).wait()


def _invert_kernel(pos_ref, pad_ref, tok_ref):
    def clear(p, carry):
        tok_ref[p] = 0
        return carry

    def put(t, carry):
        tok_ref[pos_ref[t]] = t
        return carry

    for r in range(N_GROUPS + 1):
        lax.fori_loop(pad_ref[r, 0], pad_ref[r, 1], clear, 0)
    lax.fori_loop(0, pos_ref.shape[0], put, 0, unroll=8)


def _invert(pos, pad, n_slots):
    return pl.pallas_call(
        _invert_kernel,
        in_specs=[pl.BlockSpec(memory_space=pltpu.SMEM), pl.BlockSpec(memory_space=pltpu.SMEM)],
        out_specs=pl.BlockSpec(memory_space=pltpu.SMEM),
        out_shape=jax.ShapeDtypeStruct((n_slots,), jnp.int32),
        name="moe_invert",
    )(pos, pad)


def _experts_kernel(grp_ref, valid_ref, tok_ref, hp_hbm, w1_ref, w3_ref, w2_ref, ys_ref, buf, sems):
    k = pl.program_id(0)
    n_blk = pl.num_programs(0)
    n_valid = valid_ref[k]
    slot = k % 2

    def gather(blk, buf_slot):
        def body(p, carry):
            pltpu.make_async_copy(hp_hbm.at[pl.ds(tok_ref[blk * MOE_R + p], 1)],
                                  buf.at[buf_slot, pl.ds(p, 1)], sems.at[buf_slot]).start()
            return carry

        lax.fori_loop(0, MOE_R, body, 0, unroll=8)

    @pl.when(k == 0)
    def _():
        gather(0, 0)

    nxt = jnp.minimum(k + 1, n_blk - 1)

    @pl.when((k + 1 < n_blk) & (valid_ref[nxt] > 0))
    def _():
        gather(k + 1, 1 - slot)

    @pl.when(n_valid > 0)
    def _():
        _row_copies_wait(hp_hbm.at[pl.ds(0, MOE_R)], buf.at[slot], sems.at[slot])
        w = buf[slot]
        ok = lax.broadcasted_iota(jnp.int32, (MOE_R, 1), 0) < n_valid
        h = jnp.where(ok, w[:, 0:D_MODEL], 0.0).astype(BF16)
        gates = jnp.where(ok, w[:, D_MODEL:], 0.0)
        acc = jnp.zeros((MOE_R, D_MODEL), F32)
        for e in range(EXP_PER_GROUP):
            a = jnp.dot(h, w1_ref[e], preferred_element_type=F32)
            b = jnp.dot(h, w3_ref[e], preferred_element_type=F32)
            he = (_silu(a) * b * gates[:, e:e + 1]).astype(BF16)
            acc = acc + jnp.dot(he, w2_ref[e], preferred_element_type=F32)
        ys_ref[...] = acc

    @pl.when(n_valid == 0)
    def _():
        ys_ref[...] = jnp.zeros_like(ys_ref)


def _experts(hp, tok, blk_grp, blk_valid, w1, w3, w2, layer):
    n_slots = tok.shape[0]
    up = pl.BlockSpec((None, EXP_PER_GROUP, D_MODEL, D_EXPERT),
                      lambda k, grp, valid, tok: (layer, grp[k], 0, 0))
    down = pl.BlockSpec((None, EXP_PER_GROUP, D_EXPERT, D_MODEL),
                        lambda k, grp, valid, tok: (layer, grp[k], 0, 0))
    grid_spec = pltpu.PrefetchScalarGridSpec(
        num_scalar_prefetch=3,
        grid=(n_slots // MOE_R,),
        in_specs=[pl.BlockSpec(memory_space=pl.ANY), up, up, down],
        out_specs=pl.BlockSpec((MOE_R, D_MODEL), lambda k, grp, valid, tok: (k, 0)),
        scratch_shapes=[pltpu.VMEM((2, MOE_R, HP_COLS), F32), pltpu.SemaphoreType.DMA((2,))],
    )
    return pl.pallas_call(
        _experts_kernel,
        grid_spec=grid_spec,
        out_shape=jax.ShapeDtypeStruct((n_slots, D_MODEL), F32),
        compiler_params=_params("arbitrary"),
        name="moe_experts",
    )(blk_grp, blk_valid, tok, hp, w1, w3, w2)


def _combine_kernel(pos_ref, x_ref, modl_ref, modc_ref, fg_ref, ys_hbm, o_ref, buf, sem,
                    *, tm, has_ctx, final_norm):
    def body(t, carry):
        pltpu.make_async_copy(ys_hbm.at[pl.ds(pos_ref[0, 0, t], 1)], buf.at[pl.ds(t, 1)],
                              sem).start()
        return carry

    lax.fori_loop(0, tm, body, 0, unroll=8)
    _row_copies_wait(ys_hbm.at[pl.ds(0, tm)], buf, sem)
    pick = _mod_picker(modl_ref, modc_ref, tm, has_ctx)
    x = x_ref[...] + pick(5) * buf[...]
    o_ref[...] = _rms(x, fg_ref[...]) if final_norm else x


def _combine(pos, x, ys, mod, final_g, has_ctx, final_norm):
    n = x.shape[0]
    tm = COMBINE_TM if has_ctx else COMBINE_TM_LATENT
    tpb = (SEQ_ALL if has_ctx else SEQ) // tm
    return pl.pallas_call(
        functools.partial(_combine_kernel, tm=tm, has_ctx=has_ctx, final_norm=final_norm),
        grid=(n // tm,),
        in_specs=[
            pl.BlockSpec((1, 1, tm), lambda i: (i, 0, 0), memory_space=pltpu.SMEM),
            pl.BlockSpec((tm, D_MODEL), lambda i: (i, 0)),
            *_mod_specs(tpb),
            pl.BlockSpec((1, D_MODEL), lambda i: (0, 0)),
            pl.BlockSpec(memory_space=pl.ANY),
        ],
        out_specs=pl.BlockSpec((tm, D_MODEL), lambda i: (i, 0)),
        out_shape=jax.ShapeDtypeStruct((n, D_MODEL), F32),
        scratch_shapes=[pltpu.VMEM((tm, D_MODEL), F32), pltpu.SemaphoreType.DMA],
        compiler_params=_params("arbitrary"),
        name="moe_combine",
    )(pos.reshape(n // tm, 1, tm), x, mod, mod, final_g.reshape(1, D_MODEL), ys)


def _routing_tables(slot, counts, n):
    n_blk = n // MOE_R + N_GROUPS
    n_g = counts[:N_GROUPS, 0]
    nb = (n_g + MOE_R - 1) // MOE_R
    ends = jnp.cumsum(nb)
    starts = ends - nb
    g_tok = slot // n
    pos = slot - g_tok * n + starts[g_tok] * MOE_R
    k = jnp.arange(n_blk, dtype=jnp.int32)
    g = jnp.minimum(jnp.sum((k[:, None] >= ends[None, :]).astype(jnp.int32), axis=1), N_GROUPS - 1)
    valid = jnp.where(k < ends[-1], jnp.clip(n_g[g] - (k - starts[g]) * MOE_R, 0, MOE_R), 0)
    pad = jnp.stack([jnp.append(starts * MOE_R + n_g, ends[-1] * MOE_R),
                     jnp.append(ends * MOE_R, n_blk * MOE_R)], axis=1)
    return (pos.astype(jnp.int32), g.astype(jnp.int32), valid.astype(jnp.int32),
            pad.astype(jnp.int32))


def _moe(hp, slot, counts, x, w1, w3, w2, layer, mod, final_g, has_ctx, final_norm):
    n = x.shape[0]
    assert n % MOE_R == 0
    pos, blk_grp, blk_valid, pad = _routing_tables(slot.reshape(n), counts, n)
    tok = _invert(pos, pad, n + N_GROUPS * MOE_R)
    ys = _experts(hp, tok, blk_grp, blk_valid, w1, w3, w2, layer)
    return _combine(pos, x, ys, mod, final_g, has_ctx, final_norm)


def _block_diag(w):
    eye = jnp.eye(HEADS, dtype=w.dtype)
    return jnp.einsum('hij,hg->higj', w, eye).reshape(WIDTH, WIDTH)


def _rope_tables():
    rows = SEQ // GRID_W
    row_ids = jnp.repeat(jnp.arange(rows, dtype=F32), GRID_W)
    col_ids = jnp.tile(jnp.arange(GRID_W, dtype=F32), rows)
    n_freq = DA_QD // 4
    freqs = ROPE_THETA ** (-jnp.arange(n_freq, dtype=F32) / n_freq)
    ang = jnp.stack([row_ids[:, None] * freqs, col_ids[:, None] * freqs], axis=1)
    cos = jnp.cos(ang)
    sin = jnp.sin(ang)
    cos_l = jnp.broadcast_to(cos[:, None, None, :, None, :], (SEQ, HEADS, 2, 2, 2, n_freq))
    sin_l = jnp.broadcast_to(sin[:, None, None, :, None, :], (SEQ, HEADS, 2, 2, 2, n_freq))
    sign = jnp.array([-1.0, 1.0], F32)[None, None, None, None, :, None]
    cos_l = cos_l.reshape(SEQ, WIDTH)
    sin_l = (sin_l * sign).reshape(SEQ, WIDTH)
    cos_all = jnp.concatenate([jnp.ones((CTX_LEN, WIDTH), F32), cos_l], axis=0)
    sin_all = jnp.concatenate([jnp.zeros((CTX_LEN, WIDTH), F32), sin_l], axis=0)
    return cos_all, sin_all


def kernel(x, c, ctx, c_ctx, w_ada, b_ada, norm1_g, norm2_g, w_in, w_out, lru_conv_w, lru_conv_b,
           lru_wr, lru_br, lru_wi, lru_bi, lru_lam, da_lam, da_subln_g, sg_norm_g, sg_w, sg_b,
           hg_lb, hg_norm_g, router_w, router_b, moe_w1, moe_w3, moe_w2, final_norm_g):
    nb = x.shape[0]
    assert nb <= CTX_ROW and x.shape[1:] == (SEQ, D_MODEL) and ctx.shape[1:] == (CTX_LEN, D_MODEL)
    xs = jnp.concatenate([ctx, x], axis=1).reshape(nb * SEQ_ALL, D_MODEL)

    cvec = jnp.zeros((ADA_ROWS, D_MODEL), F32).at[:nb].set(c).at[CTX_ROW].set(c_ctx)
    mods = _ada(cvec, w_ada, b_ada).reshape(DEPTH, ADA_ROWS, 6, D_MODEL)

    cos_all, sin_all = _rope_tables()
    lb_cum = jnp.cumsum(jax.nn.softmax(hg_lb.astype(F32), axis=1), axis=1)
    lb_all = lb_cum - lb_cum[:, :1]
    wr_t = router_w.T
    br = router_b.reshape(N_EXPERTS, 1)
    w1_bf16, w3_bf16, w2_bf16 = (w.astype(BF16) for w in (moe_w1, moe_w3, moe_w2))

    for l in range(DEPTH):
        last = l == DEPTH - 1
        out_rows = SEQ if last else SEQ_ALL
        z = _inproj(xs, norm1_g[l], mods[l], w_in[l].astype(BF16))

        ya = _lru(z, lru_conv_w[l], lru_conv_b[l].reshape(1, WIDTH),
                  jax.vmap(_block_diag)(lru_wr[l]).astype(BF16), lru_br[l],
                  jax.vmap(_block_diag)(lru_wi[l]).astype(BF16), lru_bi[l], lru_lam[l], out_rows)

        lam_init = 0.8 - 0.6 * math.exp(-0.3 * l)
        lf = da_lam[l].astype(F32)
        lam = jnp.exp(jnp.sum(lf[0] * lf[1])) - jnp.exp(jnp.sum(lf[2] * lf[3])) + lam_init
        yb = _attn(z, lam.reshape(1, 1), cos_all, sin_all,
                   jnp.tile(da_subln_g[l], HEADS).reshape(1, WIDTH),
                   with_ctx=not last, out_scale=1.0 - lam_init)

        w_cat = jnp.transpose(sg_w[l], (1, 0, 2)).reshape(SG_CHUNK, HEADS * SG_CHUNK).astype(BF16)
        bias2d = jnp.repeat(sg_b[l].T, HEAD_DIM, axis=1)
        ys = _sgu(z, sg_norm_g[l].reshape(1, WIDTH), w_cat, bias2d, out_rows)

        yh = _hgrn(z, lb_all[:, l], hg_norm_g[l].reshape(1, WIDTH), out_rows)

        xs, hp, pos, counts = _outproj(xs, (ya, yb, ys, yh), w_out[l].astype(BF16), mods[l],
                                       norm2_g[l], wr_t, br, has_ctx=not last)
        xs = _moe(hp, pos, counts, xs, w1_bf16, w3_bf16, w2_bf16, l, mods[l], final_norm_g,
                  has_ctx=not last, final_norm=last)

    return xs.reshape(nb, SEQ, D_MODEL)
```

```python
import functools
import math

import jax
import jax.numpy as jnp
from jax import lax
from jax.experimental import pallas as pl
from jax.experimental.pallas import tpu as pltpu

F32 = jnp.float32
BF16 = jnp.bfloat16

D_MODEL = 1024
SEQ = 2048
CTX_LEN = 256
SEQ_ALL = CTX_LEN + SEQ
DEPTH = 2
GRID_W = 64
EPS = 1e-6
LOG2_E = 1.4426950408889634
WIDTH = 256
HEADS = 4
HEAD_DIM = 64
CONV_W = 4
RG_C = 8.0
DA_QD = 32
ROPE_THETA = 10000.0
SG_CHUNK = 128
N_EXPERTS = 16
N_GROUPS = 4
EXP_PER_GROUP = 4
D_EXPERT = 512
IN_COLS = 3072
COL_A_X, COL_A_G = 0, 1
COL_B_Q, COL_B_K, COL_B_V = 2, 3, 4
COL_C_U, COL_C_V = 5, 6
COL_D_Q, COL_D_FF, COL_D_FB, COL_D_I, COL_D_G = 7, 8, 9, 10, 11

ADA_ROWS = 16
CTX_ROW = 8
VMEM_LIMIT = 56 * 1024 * 1024

LRU_BLK = 8
HG_CHUNK = 16
ATT_TQ = 256
DENSE_TM = 768
MOE_R = 512
HP_COLS = D_MODEL + 128
COMBINE_TM = 1152
COMBINE_TM_LATENT = 1024


def _params(*sem):
    return pltpu.CompilerParams(dimension_semantics=sem, vmem_limit_bytes=VMEM_LIMIT)


def _rms(xf, g):
    return xf * lax.rsqrt(jnp.mean(xf * xf, axis=-1, keepdims=True) + EPS) * g


def _sigmoid(x):
    return 1.0 / (1.0 + jnp.exp(-x))


def _silu(x):
    return x * _sigmoid(x)


def _gelu(x):
    return jax.nn.gelu(x)


def _split3(x):
    hi = x.astype(BF16)
    r = x - hi.astype(F32)
    mid = r.astype(BF16)
    lo = (r - mid.astype(F32)).astype(BF16)
    return hi, mid, lo


def _dot_f32_rhs(m_bf16, x):
    return functools.reduce(jnp.add, [jnp.dot(m_bf16, p, preferred_element_type=F32)
                                      for p in _split3(x)])


def _dot_f32_lhs(x, m_bf16):
    return functools.reduce(jnp.add, [jnp.dot(p, m_bf16, preferred_element_type=F32)
                                      for p in _split3(x)])


def _head_ones(dtype):
    r = lax.broadcasted_iota(jnp.int32, (WIDTH, WIDTH), 0) // HEAD_DIM
    c = lax.broadcasted_iota(jnp.int32, (WIDTH, WIDTH), 1) // HEAD_DIM
    return (r == c).astype(dtype)


def _mod_picker(modl_ref, modc_ref, tm, has_ctx):
    ml = modl_ref[0]
    if not has_ctx:
        return lambda r: ml[r:r + 1]
    mc = modc_ref[0]
    row0 = (pl.program_id(0) % (SEQ_ALL // tm)) * tm
    is_ctx = row0 + lax.broadcasted_iota(jnp.int32, (tm, 1), 0) < CTX_LEN
    return lambda r: jnp.where(is_ctx, mc[r:r + 1], ml[r:r + 1])


def _ada_kernel(c_ref, w_ref, b_ref, o_ref):
    s = _silu(c_ref[...]).astype(BF16)
    o_ref[...] = jnp.dot(s, w_ref[...].astype(BF16), preferred_element_type=F32) + b_ref[...]


def _ada(cvec, w_ada, b_ada):
    tn = 1536
    return pl.pallas_call(
        _ada_kernel,
        grid=(DEPTH, 6 * D_MODEL // tn),
        in_specs=[
            pl.BlockSpec((ADA_ROWS, D_MODEL), lambda l, j: (0, 0)),
            pl.BlockSpec((None, D_MODEL, tn), lambda l, j: (l, 0, j)),
            pl.BlockSpec((None, 1, tn), lambda l, j: (l, 0, j)),
        ],
        out_specs=pl.BlockSpec((None, ADA_ROWS, tn), lambda l, j: (l, 0, j)),
        out_shape=jax.ShapeDtypeStruct((DEPTH, ADA_ROWS, 6 * D_MODEL), F32),
        compiler_params=_params("arbitrary", "arbitrary"),
        name="ada",
    )(cvec, w_ada, b_ada.reshape(DEPTH, 1, 6 * D_MODEL))


def _inproj_kernel(x_ref, g_ref, modl_ref, modc_ref, w_ref, z_ref, *, tm):
    pick = _mod_picker(modl_ref, modc_ref, tm, True)
    h = _rms(x_ref[...], g_ref[...]) * (1.0 + pick(1)) + pick(0)
    z_ref[...] = jnp.dot(h.astype(BF16), w_ref[...], preferred_element_type=F32)


def _mod_specs(tiles_per_batch):
    return [
        pl.BlockSpec((1, 6, D_MODEL), lambda i, *_: (i // tiles_per_batch, 0, 0)),
        pl.BlockSpec((1, 6, D_MODEL), lambda i, *_: (CTX_ROW, 0, 0)),
    ]


def _inproj(x, g, mod, w_bf16):
    n = x.shape[0]
    tm = DENSE_TM
    return pl.pallas_call(
        functools.partial(_inproj_kernel, tm=tm),
        grid=(n // tm,),
        in_specs=[
            pl.BlockSpec((tm, D_MODEL), lambda i: (i, 0)),
            pl.BlockSpec((1, D_MODEL), lambda i: (0, 0)),
            *_mod_specs(SEQ_ALL // tm),
            pl.BlockSpec((D_MODEL, IN_COLS), lambda i: (0, 0)),
        ],
        out_specs=pl.BlockSpec((tm, IN_COLS), lambda i: (i, 0)),
        out_shape=jax.ShapeDtypeStruct((n, IN_COLS), F32),
        compiler_params=_params("arbitrary"),
        name="inproj",
    )(x, g.reshape(1, D_MODEL), mod, mod, w_bf16)


def _lru_kernel(x_ref, gate_ref, cw_ref, cb_ref, wr_ref, br_ref, wi_ref, bi_ref, lam_ref,
                y_ref, a_s, b_s, h_s, *, out_rows):
    x = x_ref[...]
    rows = lax.broadcasted_iota(jnp.int32, (SEQ_ALL, 1), 0)
    seg = rows < CTX_LEN
    u = jnp.zeros_like(x)
    for j in range(CONV_W):
        off = j - CONV_W // 2
        xs = x if off == 0 else pltpu.roll(x, (-off) % SEQ_ALL, 0)
        src = rows + off
        ok = (src >= 0) & (src < SEQ_ALL) & ((src < CTX_LEN) == seg)
        u = u + jnp.where(ok, xs, 0.0) * cw_ref[j:j + 1, :]
    u = u + cb_ref[...]
    ub = u.astype(BF16)
    for d in range(2):
        r = _sigmoid(jnp.dot(ub, wr_ref[d], preferred_element_type=F32) + br_ref[d:d + 1, :])
        i = _sigmoid(jnp.dot(ub, wi_ref[d], preferred_element_type=F32) + bi_ref[d:d + 1, :])
        nl = -lam_ref[d:d + 1, :]
        softplus = jnp.maximum(nl, 0.0) + jnp.log(1.0 + jnp.exp(-jnp.abs(nl)))
        log_a = -RG_C * r * softplus
        a_s[d] = jnp.exp(log_a)
        b_s[d] = jnp.sqrt(1.0 - jnp.exp(2.0 * log_a)) * i * u

    n_blk = SEQ_ALL // LRU_BLK
    n_ctx_blk = CTX_LEN // LRU_BLK
    sub = lax.broadcasted_iota(jnp.int32, (LRU_BLK, 1), 0)

    def block_scan(a, b, reverse):
        s = 1
        while s < LRU_BLK:
            if reverse:
                a_sh = pltpu.roll(a, LRU_BLK - s, 0)
                b_sh = pltpu.roll(b, LRU_BLK - s, 0)
                ok = sub < LRU_BLK - s
            else:
                a_sh = pltpu.roll(a, s, 0)
                b_sh = pltpu.roll(b, s, 0)
                ok = sub >= s
            b = jnp.where(ok, a * b_sh + b, b)
            a = jnp.where(ok, a * a_sh, a)
            s *= 2
        return a, b

    def body(n, carry):
        hf, hb = carry
        rf = pl.multiple_of(n * LRU_BLK, LRU_BLK)
        af, bf = block_scan(a_s[0, pl.ds(rf, LRU_BLK), :], b_s[0, pl.ds(rf, LRU_BLK), :], False)
        h = af * hf + bf
        h_s[0, pl.ds(rf, LRU_BLK), :] = h
        hf = h[LRU_BLK - 1:LRU_BLK, :]
        nb = jnp.where(n < n_ctx_blk, n_ctx_blk - 1 - n, n_blk + n_ctx_blk - 1 - n)
        rb = pl.multiple_of(nb * LRU_BLK, LRU_BLK)
        ab, bb = block_scan(a_s[1, pl.ds(rb, LRU_BLK), :], b_s[1, pl.ds(rb, LRU_BLK), :], True)
        h = ab * hb + bb
        h_s[1, pl.ds(rb, LRU_BLK), :] = h
        hb = h[0:1, :]
        return hf, hb

    zero = jnp.zeros((1, WIDTH), F32)
    lax.fori_loop(0, n_blk, body, (zero, zero))
    first = SEQ_ALL - out_rows
    y_ref[...] = (h_s[0, first:, :] + h_s[1, first:, :]) * _gelu(gate_ref[first:, :])


def _lru(z, cw, cb, wr_bd, br, wi_bd, bi, lam, out_rows):
    nb = z.shape[0] // SEQ_ALL
    full = lambda shape: pl.BlockSpec(shape, lambda b: (0,) * len(shape))
    return pl.pallas_call(
        functools.partial(_lru_kernel, out_rows=out_rows),
        grid=(nb,),
        in_specs=[
            pl.BlockSpec((SEQ_ALL, WIDTH), lambda b: (b, COL_A_X)),
            pl.BlockSpec((SEQ_ALL, WIDTH), lambda b: (b, COL_A_G)),
            full((CONV_W, WIDTH)), full((1, WIDTH)),
            full((2, WIDTH, WIDTH)), full((2, WIDTH)),
            full((2, WIDTH, WIDTH)), full((2, WIDTH)),
            full((2, WIDTH)),
        ],
        out_specs=pl.BlockSpec((out_rows, WIDTH), lambda b: (b, 0)),
        out_shape=jax.ShapeDtypeStruct((nb * out_rows, WIDTH), F32),
        scratch_shapes=[
            pltpu.VMEM((2, SEQ_ALL, WIDTH), F32),
            pltpu.VMEM((2, SEQ_ALL, WIDTH), F32),
            pltpu.VMEM((2, SEQ_ALL, WIDTH), F32),
        ],
        compiler_params=_params("arbitrary"),
        name="rglru",
    )(z, z, cw, cb, wr_bd, br, wi_bd, bi, lam)


def _rope(x, cos, sin_signed):
    lane = lax.broadcasted_iota(jnp.int32, (1, WIDTH), 1)
    first_half = (lane % 16) < 8
    partner = jnp.where(first_half, pltpu.roll(x, WIDTH - 8, 1), pltpu.roll(x, 8, 1))
    return x * cos + partner * sin_signed


def _attn_kernel(lam_ref, q_ref, k_ref, v_ref, cos_ref, sin_ref, g_ref, y_ref, kt_s, v_s,
                 *, with_ctx, out_scale):
    j = pl.program_id(1)

    @pl.when(j == 0)
    def _():
        kr = _rope(k_ref[...], cos_ref[...], sin_ref[...])
        kt_s[...] = kr.T.astype(BF16)
        v = v_ref[...]
        ones = jnp.ones((SEQ_ALL, HEAD_DIM), F32)
        for h in range(HEADS):
            v_s[h] = jnp.concatenate([v[:, h * HEAD_DIM:(h + 1) * HEAD_DIM], ones],
                                     axis=1).astype(BF16)

    lam = lam_ref[0, 0]
    tile = j if with_ctx else j + 1
    row0 = pl.multiple_of(tile * ATT_TQ, ATT_TQ)
    q = _rope(q_ref[...], cos_ref[pl.ds(row0, ATT_TQ), :], sin_ref[pl.ds(row0, ATT_TQ), :])
    q = q * (DA_QD ** -0.5)

    def attend(n_keys):
        outs = []
        for h in range(HEADS):
            scores = []
            for n in range(2):
                c0 = h * HEAD_DIM + n * DA_QD
                scores.append(jnp.dot(q[:, c0:c0 + DA_QD].astype(BF16),
                                      kt_s[c0:c0 + DA_QD, 0:n_keys], preferred_element_type=F32))
            probs = [jnp.exp(s - jnp.max(s, axis=-1, keepdims=True)).astype(BF16) for s in scores]
            branch = []
            for p in probs:
                ov = jnp.dot(p, v_s[h, 0:n_keys, :], preferred_element_type=F32)
                branch.append(ov[:, 0:HEAD_DIM] * (1.0 / ov[:, HEAD_DIM:HEAD_DIM + 1]))
            o = branch[0] - lam * branch[1]
            o = o * lax.rsqrt(jnp.mean(o * o, axis=-1, keepdims=True) + EPS)
            outs.append(o)
        y_ref[...] = jnp.concatenate(outs, axis=-1) * g_ref[...] * out_scale

    if with_ctx:
        @pl.when(j == 0)
        def _():
            attend(CTX_LEN)

        @pl.when(j > 0)
        def _():
            attend(SEQ_ALL)
    else:
        attend(SEQ_ALL)


def _attn(z, lam, cos, sin_signed, sub_g4, with_ctx, out_scale):
    nb = z.shape[0] // SEQ_ALL
    tiles = SEQ_ALL // ATT_TQ
    nq = tiles if with_ctx else tiles - 1
    first = 0 if with_ctx else 1
    return pl.pallas_call(
        functools.partial(_attn_kernel, with_ctx=with_ctx, out_scale=out_scale),
        grid=(nb, nq),
        in_specs=[
            pl.BlockSpec(memory_space=pltpu.SMEM),
            pl.BlockSpec((ATT_TQ, WIDTH), lambda b, j: (b * tiles + j + first, COL_B_Q)),
            pl.BlockSpec((SEQ_ALL, WIDTH), lambda b, j: (b, COL_B_K)),
            pl.BlockSpec((SEQ_ALL, WIDTH), lambda b, j: (b, COL_B_V)),
            pl.BlockSpec((SEQ_ALL, WIDTH), lambda b, j: (0, 0)),
            pl.BlockSpec((SEQ_ALL, WIDTH), lambda b, j: (0, 0)),
            pl.BlockSpec((1, WIDTH), lambda b, j: (0, 0)),
        ],
        out_specs=pl.BlockSpec((ATT_TQ, WIDTH), lambda b, j: (b * nq + j, 0)),
        out_shape=jax.ShapeDtypeStruct((nb * nq * ATT_TQ, WIDTH), F32),
        scratch_shapes=[
            pltpu.VMEM((WIDTH, SEQ_ALL), BF16),
            pltpu.VMEM((HEADS, SEQ_ALL, 2 * HEAD_DIM), BF16),
        ],
        compiler_params=_params("arbitrary", "arbitrary"),
        name="diffattn",
    )(lam, z, z, z, cos, sin_signed, sub_g4)


def _sgu_kernel(u_ref, v_ref, g_ref, w_ref, b_ref, y_ref, *, out_rows):
    row_head = lax.broadcasted_iota(jnp.int32, (HEADS * SG_CHUNK, WIDTH), 0) // SG_CHUNK
    col_head = lax.broadcasted_iota(jnp.int32, (HEADS * SG_CHUNK, WIDTH), 1) // HEAD_DIM
    head_mask = row_head == col_head
    w = w_ref[...]
    bias = b_ref[...]
    g = g_ref[...]
    first = SEQ_ALL - out_rows
    for n in range(first // SG_CHUNK, SEQ_ALL // SG_CHUNK):
        rows = slice(n * SG_CHUNK, (n + 1) * SG_CHUNK)
        vn = _rms(_gelu(v_ref[rows, :]), g).astype(BF16)
        stacked = jnp.where(head_mask, jnp.concatenate([vn] * HEADS, axis=0), jnp.zeros((), BF16))
        vm = jnp.dot(w, stacked, preferred_element_type=F32) + bias
        y_ref[n * SG_CHUNK - first:(n + 1) * SG_CHUNK - first, :] = _gelu(u_ref[rows, :]) * vm


def _sgu(z, norm_g, w_cat, bias2d, out_rows):
    nb = z.shape[0] // SEQ_ALL
    return pl.pallas_call(
        functools.partial(_sgu_kernel, out_rows=out_rows),
        grid=(nb,),
        in_specs=[
            pl.BlockSpec((SEQ_ALL, WIDTH), lambda b: (b, COL_C_U)),
            pl.BlockSpec((SEQ_ALL, WIDTH), lambda b: (b, COL_C_V)),
            pl.BlockSpec((1, WIDTH), lambda b: (0, 0)),
            pl.BlockSpec((SG_CHUNK, HEADS * SG_CHUNK), lambda b: (0, 0)),
            pl.BlockSpec((SG_CHUNK, WIDTH), lambda b: (0, 0)),
        ],
        out_specs=pl.BlockSpec((out_rows, WIDTH), lambda b: (b, 0)),
        out_shape=jax.ShapeDtypeStruct((nb * out_rows, WIDTH), F32),
        compiler_params=_params("arbitrary"),
        name="sgu",
    )(z, z, norm_g, w_cat, bias2d)


def _hgrn_kernel(q_ref, ff_ref, fb_ref, i_ref, gate_ref, lb_ref, gn_ref, y_ref,
                 b_s, k_s, o_s, st_s, stb_s, *, out_rows):
    c = HG_CHUNK
    pre = 128
    ones_bf = _head_ones(BF16)
    pr = lax.broadcasted_iota(jnp.int32, (pre, pre), 0)
    pc = lax.broadcasted_iota(jnp.int32, (pre, pre), 1)
    same = (pr // c) == (pc // c)
    tri = ((same & (pc <= pr)).astype(BF16), (same & (pc >= pr)).astype(BF16))

    for d, f_ref in enumerate((ff_ref, fb_ref)):
        lb = lb_ref[d:d + 1, :]
        for n in range(SEQ_ALL // pre):
            rows = slice(n * pre, (n + 1) * pre)
            f = lb + (1.0 - lb) * _sigmoid(f_ref[rows, :])
            k_s[d, rows, :] = 1.0 - f
            b_s[d, rows, :] = _dot_f32_rhs(tri[d], jnp.log(f) * LOG2_E)
    st_s[...] = jnp.zeros_like(st_s)
    stb_s[...] = jnp.zeros_like(stb_s)

    n_chunks = SEQ_ALL // c
    n_ctx = CTX_LEN // c
    half = c // 2
    sub = lax.broadcasted_iota(jnp.int32, (half, 1), 0)
    lane_head = lax.broadcasted_iota(jnp.int32, (1, 128), 1) // HEAD_DIM
    scale = HEAD_DIM ** -0.5

    def front(d, chunk):
        r0 = pl.multiple_of(chunk * c, c)
        q = q_ref[pl.ds(r0, c), :] * scale
        v = i_ref[pl.ds(r0, c), :]
        b = b_s[d, pl.ds(r0, c), :]
        k = k_s[d, pl.ds(r0, c), :]
        tot = b[c - 1:c, :] if d == 0 else b[0:1, :]
        o = lax.dot_general((q * jnp.exp2(b)).astype(BF16), stb_s[d],
                            (((1,), (1,)), ((), ())), preferred_element_type=F32)
        parts = []
        for s in range(c):
            for tile in range(2):
                lo_row, hi_row = tile * half, (tile + 1) * half
                if (d == 0 and hi_row <= s) or (d == 1 and lo_row > s):
                    parts.append(jnp.zeros((half, WIDTH), F32))
                    continue
                diff = b[lo_row:hi_row, :] - b[s:s + 1, :]
                if d == 0 and lo_row < s:
                    diff = jnp.where(sub + lo_row >= s, diff, -jnp.inf)
                elif d == 1 and hi_row - 1 > s:
                    diff = jnp.where(sub + lo_row <= s, diff, -jnp.inf)
                parts.append(q[lo_row:hi_row, :] * k[s:s + 1, :] * jnp.exp2(diff))
        att = jnp.dot(jnp.concatenate(parts, axis=0).astype(BF16), ones_bf,
                      preferred_element_type=F32)
        ke = (k * jnp.exp2(tot - b)).astype(BF16)
        upd = lax.dot_general(v.astype(BF16), ke, (((0,), (0,)), ((), ())),
                              preferred_element_type=F32)
        return r0, o, att, upd, v, jnp.exp2(tot)

    def back(d, r0, o, att, upd, v, decay):
        for s in range(c):
            o = o + att[s * c:(s + 1) * c, :] * v[s:s + 1, :]
        o_s[d, pl.ds(r0, c), :] = o
        for h in range(HEADS):
            r = slice(h * HEAD_DIM, (h + 1) * HEAD_DIM)
            ct = slice((h // 2) * 128, (h // 2 + 1) * 128)
            new = st_s[d, r, ct] * decay[:, ct] + jnp.where(lane_head == h % 2, upd[r, ct], 0.0)
            st_s[d, r, ct] = new
            stb_s[d, r, ct] = new.astype(BF16)

    def body(n, carry):
        fwd = front(0, n)
        bwd = front(1, jnp.where(n < n_ctx, n_ctx - 1 - n, n_chunks + n_ctx - 1 - n))
        back(0, *fwd)
        back(1, *bwd)
        return carry

    lax.fori_loop(0, n_chunks, body, 0, unroll=4)

    first = SEQ_ALL - out_rows
    o = o_s[0, first:, :] + o_s[1, first:, :]
    ms = _dot_f32_lhs(o * o, ones_bf) * (1.0 / HEAD_DIM)
    y_ref[...] = o * lax.rsqrt(ms + EPS) * gn_ref[...] * _silu(gate_ref[first:, :])


def _hgrn(z, lb, gn, out_rows):
    nb = z.shape[0] // SEQ_ALL
    col = lambda cidx: pl.BlockSpec((SEQ_ALL, WIDTH), lambda b: (b, cidx))
    return pl.pallas_call(
        functools.partial(_hgrn_kernel, out_rows=out_rows),
        grid=(nb,),
        in_specs=[
            col(COL_D_Q), col(COL_D_FF), col(COL_D_FB), col(COL_D_I), col(COL_D_G),
            pl.BlockSpec((2, WIDTH), lambda b: (0, 0)),
            pl.BlockSpec((1, WIDTH), lambda b: (0, 0)),
        ],
        out_specs=pl.BlockSpec((out_rows, WIDTH), lambda b: (b, 0)),
        out_shape=jax.ShapeDtypeStruct((nb * out_rows, WIDTH), F32),
        scratch_shapes=[
            pltpu.VMEM((2, SEQ_ALL, WIDTH), F32),
            pltpu.VMEM((2, SEQ_ALL, WIDTH), F32),
            pltpu.VMEM((2, SEQ_ALL, WIDTH), F32),
            pltpu.VMEM((2, WIDTH, WIDTH), F32),
            pltpu.VMEM((2, WIDTH, WIDTH), BF16),
        ],
        compiler_params=_params("arbitrary"),
        name="hgrn2",
    )(z, z, z, z, z, lb, gn)


def _route(logits):
    rows = [logits[e:e + 1, :] for e in range(N_EXPERTS)]
    m = functools.reduce(jnp.maximum, rows)
    ex = [jnp.exp(r - m) for r in rows]
    inv = 1.0 / functools.reduce(jnp.add, ex)
    sc = [e * inv for e in ex]
    g_score = []
    for g in range(N_GROUPS):
        grp = sc[g * EXP_PER_GROUP:(g + 1) * EXP_PER_GROUP]
        pairs = [grp[a] + grp[b] for a in range(EXP_PER_GROUP) for b in range(a + 1, EXP_PER_GROUP)]
        g_score.append(functools.reduce(jnp.maximum, pairs))
    gate4 = [jnp.zeros_like(sc[0]) for _ in range(EXP_PER_GROUP)]
    onehot = []
    for g in range(N_GROUPS):
        g_ok = None
        for o in range(N_GROUPS):
            if o == g:
                continue
            t = (g_score[g] > g_score[o]) if o < g else (g_score[g] >= g_score[o])
            g_ok = t if g_ok is None else (g_ok & t)
        grp = sc[g * EXP_PER_GROUP:(g + 1) * EXP_PER_GROUP]
        picked = []
        for a in range(EXP_PER_GROUP):
            beaten = jnp.zeros_like(grp[a])
            for o in range(EXP_PER_GROUP):
                if o == a:
                    continue
                t = (grp[o] >= grp[a]) if o < a else (grp[o] > grp[a])
                beaten = beaten + jnp.where(t, 1.0, 0.0)
            picked.append(jnp.where((beaten < 1.5) & g_ok, grp[a], 0.0))
        denom = functools.reduce(jnp.add, picked)
        denom = jnp.where(g_ok, denom, 1.0)
        gate4 = [acc + p / denom for acc, p in zip(gate4, picked)]
        onehot.append(jnp.where(g_ok, 1.0, 0.0))
    return gate4, onehot


def _outproj_kernel(x_ref, ya_ref, yb_ref, ys_ref, yh_ref, w_ref, modl_ref, modc_ref, g_ref,
                    wr_ref, br_ref, xo_ref, hp_ref, pos_ref, cnt_ref, cnt_s, *, tm, has_ctx, cap):
    @pl.when(pl.program_id(0) == 0)
    def _():
        cnt_s[...] = jnp.zeros_like(cnt_s)

    pick = _mod_picker(modl_ref, modc_ref, tm, has_ctx)
    acc = jnp.zeros((tm, D_MODEL), F32)
    for kblk, y_ref in enumerate((ya_ref, yb_ref, ys_ref, yh_ref)):
        acc = acc + jnp.dot(y_ref[...].astype(BF16), w_ref[kblk * WIDTH:(kblk + 1) * WIDTH, :],
                            preferred_element_type=F32)
    x = x_ref[...] + pick(2) * acc
    xo_ref[...] = x
    h = _rms(x, g_ref[...]) * (1.0 + pick(4)) + pick(3)
    nt = (((1,), (1,)), ((), ()))
    wr = wr_ref[...]
    wr_hi = wr.astype(BF16)
    wr_lo = (wr - wr_hi.astype(F32)).astype(BF16)
    h_hi = h.astype(BF16)
    h_lo = (h - h_hi.astype(F32)).astype(BF16)
    logits = (lax.dot_general(wr_hi, h_hi, nt, preferred_element_type=F32)
              + lax.dot_general(wr_hi, h_lo, nt, preferred_element_type=F32)
              + lax.dot_general(wr_lo, h_hi, nt, preferred_element_type=F32)) + br_ref[...]
    gate4, onehot = _route(logits)
    hp_ref[:, 0:D_MODEL] = h
    gate_rows = jnp.concatenate(gate4 + [jnp.zeros((128 - EXP_PER_GROUP, tm), F32)], axis=0)
    hp_ref[:, D_MODEL:] = gate_rows.T
    sel = jnp.concatenate(onehot + [jnp.zeros((8 - N_GROUPS, tm), F32)], axis=0)
    before = (lax.broadcasted_iota(jnp.int32, (tm, tm), 0)
              <= lax.broadcasted_iota(jnp.int32, (tm, tm), 1)).astype(BF16)
    seen = jnp.dot(sel.astype(BF16), before, preferred_element_type=F32)
    carried = cnt_s[:, 0:1]
    base = lax.broadcasted_iota(jnp.int32, (8, 1), 0).astype(F32) * float(cap)
    slot = jnp.sum(sel * (base + carried + seen - 1.0), axis=0, keepdims=True)
    pos_ref[0] = slot.astype(jnp.int32)
    cnt_s[...] = cnt_s[...] + jnp.sum(sel, axis=1, keepdims=True)
    cnt_ref[...] = cnt_s[...].astype(jnp.int32)


def _outproj(x, ys, w_bf16, mod, g2, wr_t, br, has_ctx):
    n = ys[0].shape[0]
    if has_ctx:
        tm = DENSE_TM
        tpb = SEQ_ALL // tm
        x_spec = pl.BlockSpec((tm, D_MODEL), lambda i: (i, 0))
    else:
        tm = CTX_LEN
        tpb = SEQ // tm
        x_spec = pl.BlockSpec((tm, D_MODEL),
                              lambda i: ((i // tpb) * (SEQ_ALL // tm) + 1 + i % tpb, 0))
    tile = lambda w: pl.BlockSpec((tm, w), lambda i: (i, 0))
    return pl.pallas_call(
        functools.partial(_outproj_kernel, tm=tm, has_ctx=has_ctx, cap=n),
        grid=(n // tm,),
        in_specs=[
            x_spec, tile(WIDTH), tile(WIDTH), tile(WIDTH), tile(WIDTH),
            pl.BlockSpec((D_MODEL, D_MODEL), lambda i: (0, 0)),
            *_mod_specs(tpb),
            pl.BlockSpec((1, D_MODEL), lambda i: (0, 0)),
            pl.BlockSpec((N_EXPERTS, D_MODEL), lambda i: (0, 0)),
            pl.BlockSpec((N_EXPERTS, 1), lambda i: (0, 0)),
        ],
        out_specs=[
            tile(D_MODEL), tile(HP_COLS),
            pl.BlockSpec((1, 1, tm), lambda i: (i, 0, 0)),
            pl.BlockSpec((8, 128), lambda i: (0, 0)),
        ],
        out_shape=[
            jax.ShapeDtypeStruct((n, D_MODEL), F32),
            jax.ShapeDtypeStruct((n, HP_COLS), F32),
            jax.ShapeDtypeStruct((n // tm, 1, tm), jnp.int32),
            jax.ShapeDtypeStruct((8, 128), jnp.int32),
        ],
        scratch_shapes=[pltpu.VMEM((8, 128), F32)],
        compiler_params=_params("arbitrary"),
        name="outproj_router",
    )(x, *ys, w_bf16, mod, mod, g2.reshape(1, D_MODEL), wr_t, br)


def _row_copies_wait(src_rows, dst_rows, sem):
    pltpu.make_async_copy(src_rows, dst_rows, sem).wait()


def _invert_kernel(pos_ref, pad_ref, tok_ref):
    def clear(p, carry):
        tok_ref[p] = 0
        return carry

    def put(t, carry):
        tok_ref[pos_ref[t]] = t
        return carry

    for r in range(N_GROUPS + 1):
        lax.fori_loop(pad_ref[r, 0], pad_ref[r, 1], clear, 0)
    lax.fori_loop(0, pos_ref.shape[0], put, 0, unroll=8)


def _invert(pos, pad, n_slots):
    return pl.pallas_call(
        _invert_kernel,
        in_specs=[pl.BlockSpec(memory_space=pltpu.SMEM), pl.BlockSpec(memory_space=pltpu.SMEM)],
        out_specs=pl.BlockSpec(memory_space=pltpu.SMEM),
        out_shape=jax.ShapeDtypeStruct((n_slots,), jnp.int32),
        name="moe_invert",
    )(pos, pad)


def _experts_kernel(grp_ref, valid_ref, tok_ref, hp_hbm, w1_ref, w3_ref, w2_ref, ys_ref, buf, sems):
    k = pl.program_id(0)
    n_blk = pl.num_programs(0)
    n_valid = valid_ref[k]
    slot = k % 2

    def gather(blk, buf_slot):
        def body(p, carry):
            pltpu.make_async_copy(hp_hbm.at[pl.ds(tok_ref[blk * MOE_R + p], 1)],
                                  buf.at[buf_slot, pl.ds(p, 1)], sems.at[buf_slot]).start()
            return carry

        lax.fori_loop(0, MOE_R, body, 0, unroll=8)

    @pl.when(k == 0)
    def _():
        gather(0, 0)

    nxt = jnp.minimum(k + 1, n_blk - 1)

    @pl.when((k + 1 < n_blk) & (valid_ref[nxt] > 0))
    def _():
        gather(k + 1, 1 - slot)

    @pl.when(n_valid > 0)
    def _():
        _row_copies_wait(hp_hbm.at[pl.ds(0, MOE_R)], buf.at[slot], sems.at[slot])
        w = buf[slot]
        ok = lax.broadcasted_iota(jnp.int32, (MOE_R, 1), 0) < n_valid
        h = jnp.where(ok, w[:, 0:D_MODEL], 0.0).astype(BF16)
        gates = jnp.where(ok, w[:, D_MODEL:], 0.0)
        acc = jnp.zeros((MOE_R, D_MODEL), F32)
        for e in range(EXP_PER_GROUP):
            a = jnp.dot(h, w1_ref[e], preferred_element_type=F32)
            b = jnp.dot(h, w3_ref[e], preferred_element_type=F32)
            he = (_silu(a) * b * gates[:, e:e + 1]).astype(BF16)
            acc = acc + jnp.dot(he, w2_ref[e], preferred_element_type=F32)
        ys_ref[...] = acc

    @pl.when(n_valid == 0)
    def _():
        ys_ref[...] = jnp.zeros_like(ys_ref)


def _experts(hp, tok, blk_grp, blk_valid, w1, w3, w2, layer):
    n_slots = tok.shape[0]
    up = pl.BlockSpec((None, EXP_PER_GROUP, D_MODEL, D_EXPERT),
                      lambda k, grp, valid, tok: (layer, grp[k], 0, 0))
    down = pl.BlockSpec((None, EXP_PER_GROUP, D_EXPERT, D_MODEL),
                        lambda k, grp, valid, tok: (layer, grp[k], 0, 0))
    grid_spec = pltpu.PrefetchScalarGridSpec(
        num_scalar_prefetch=3,
        grid=(n_slots // MOE_R,),
        in_specs=[pl.BlockSpec(memory_space=pl.ANY), up, up, down],
        out_specs=pl.BlockSpec((MOE_R, D_MODEL), lambda k, grp, valid, tok: (k, 0)),
        scratch_shapes=[pltpu.VMEM((2, MOE_R, HP_COLS), F32), pltpu.SemaphoreType.DMA((2,))],
    )
    return pl.pallas_call(
        _experts_kernel,
        grid_spec=grid_spec,
        out_shape=jax.ShapeDtypeStruct((n_slots, D_MODEL), F32),
        compiler_params=_params("arbitrary"),
        name="moe_experts",
    )(blk_grp, blk_valid, tok, hp, w1, w3, w2)


def _combine_kernel(pos_ref, x_ref, modl_ref, modc_ref, fg_ref, ys_hbm, o_ref, buf, sem,
                    *, tm, has_ctx, final_norm):
    def body(t, carry):
        pltpu.make_async_copy(ys_hbm.at[pl.ds(pos_ref[0, 0, t], 1)], buf.at[pl.ds(t, 1)],
                              sem).start()
        return carry

    lax.fori_loop(0, tm, body, 0, unroll=8)
    _row_copies_wait(ys_hbm.at[pl.ds(0, tm)], buf, sem)
    pick = _mod_picker(modl_ref, modc_ref, tm, has_ctx)
    x = x_ref[...] + pick(5) * buf[...]
    o_ref[...] = _rms(x, fg_ref[...]) if final_norm else x


def _combine(pos, x, ys, mod, final_g, has_ctx, final_norm):
    n = x.shape[0]
    tm = COMBINE_TM if has_ctx else COMBINE_TM_LATENT
    tpb = (SEQ_ALL if has_ctx else SEQ) // tm
    return pl.pallas_call(
        functools.partial(_combine_kernel, tm=tm, has_ctx=has_ctx, final_norm=final_norm),
        grid=(n // tm,),
        in_specs=[
            pl.BlockSpec((1, 1, tm), lambda i: (i, 0, 0), memory_space=pltpu.SMEM),
            pl.BlockSpec((tm, D_MODEL), lambda i: (i, 0)),
            *_mod_specs(tpb),
            pl.BlockSpec((1, D_MODEL), lambda i: (0, 0)),
            pl.BlockSpec(memory_space=pl.ANY),
        ],
        out_specs=pl.BlockSpec((tm, D_MODEL), lambda i: (i, 0)),
        out_shape=jax.ShapeDtypeStruct((n, D_MODEL), F32),
        scratch_shapes=[pltpu.VMEM((tm, D_MODEL), F32), pltpu.SemaphoreType.DMA],
        compiler_params=_params("arbitrary"),
        name="moe_combine",
    )(pos.reshape(n // tm, 1, tm), x, mod, mod, final_g.reshape(1, D_MODEL), ys)


def _routing_tables(slot, counts, n):
    n_blk = n // MOE_R + N_GROUPS
    n_g = counts[:N_GROUPS, 0]
    nb = (n_g + MOE_R - 1) // MOE_R
    ends = jnp.cumsum(nb)
    starts = ends - nb
    g_tok = slot // n
    pos = slot - g_tok * n + starts[g_tok] * MOE_R
    k = jnp.arange(n_blk, dtype=jnp.int32)
    g = jnp.minimum(jnp.sum((k[:, None] >= ends[None, :]).astype(jnp.int32), axis=1), N_GROUPS - 1)
    valid = jnp.where(k < ends[-1], jnp.clip(n_g[g] - (k - starts[g]) * MOE_R, 0, MOE_R), 0)
    pad = jnp.stack([jnp.append(starts * MOE_R + n_g, ends[-1] * MOE_R),
                     jnp.append(ends * MOE_R, n_blk * MOE_R)], axis=1)
    return (pos.astype(jnp.int32), g.astype(jnp.int32), valid.astype(jnp.int32),
            pad.astype(jnp.int32))


def _moe(hp, slot, counts, x, w1, w3, w2, layer, mod, final_g, has_ctx, final_norm):
    n = x.shape[0]
    assert n % MOE_R == 0
    pos, blk_grp, blk_valid, pad = _routing_tables(slot.reshape(n), counts, n)
    tok = _invert(pos, pad, n + N_GROUPS * MOE_R)
    ys = _experts(hp, tok, blk_grp, blk_valid, w1, w3, w2, layer)
    return _combine(pos, x, ys, mod, final_g, has_ctx, final_norm)


def _block_diag(w):
    eye = jnp.eye(HEADS, dtype=w.dtype)
    return jnp.einsum('hij,hg->higj', w, eye).reshape(WIDTH, WIDTH)


def _rope_tables():
    rows = SEQ // GRID_W
    row_ids = jnp.repeat(jnp.arange(rows, dtype=F32), GRID_W)
    col_ids = jnp.tile(jnp.arange(GRID_W, dtype=F32), rows)
    n_freq = DA_QD // 4
    freqs = ROPE_THETA ** (-jnp.arange(n_freq, dtype=F32) / n_freq)
    ang = jnp.stack([row_ids[:, None] * freqs, col_ids[:, None] * freqs], axis=1)
    cos = jnp.cos(ang)
    sin = jnp.sin(ang)
    cos_l = jnp.broadcast_to(cos[:, None, None, :, None, :], (SEQ, HEADS, 2, 2, 2, n_freq))
    sin_l = jnp.broadcast_to(sin[:, None, None, :, None, :], (SEQ, HEADS, 2, 2, 2, n_freq))
    sign = jnp.array([-1.0, 1.0], F32)[None, None, None, None, :, None]
    cos_l = cos_l.reshape(SEQ, WIDTH)
    sin_l = (sin_l * sign).reshape(SEQ, WIDTH)
    cos_all = jnp.concatenate([jnp.ones((CTX_LEN, WIDTH), F32), cos_l], axis=0)
    sin_all = jnp.concatenate([jnp.zeros((CTX_LEN, WIDTH), F32), sin_l], axis=0)
    return cos_all, sin_all


def kernel(x, c, ctx, c_ctx, w_ada, b_ada, norm1_g, norm2_g, w_in, w_out, lru_conv_w, lru_conv_b,
           lru_wr, lru_br, lru_wi, lru_bi, lru_lam, da_lam, da_subln_g, sg_norm_g, sg_w, sg_b,
           hg_lb, hg_norm_g, router_w, router_b, moe_w1, moe_w3, moe_w2, final_norm_g):
    nb = x.shape[0]
    assert nb <= CTX_ROW and x.shape[1:] == (SEQ, D_MODEL) and ctx.shape[1:] == (CTX_LEN, D_MODEL)
    xs = jnp.concatenate([ctx, x], axis=1).reshape(nb * SEQ_ALL, D_MODEL)

    cvec = jnp.zeros((ADA_ROWS, D_MODEL), F32).at[:nb].set(c).at[CTX_ROW].set(c_ctx)
    mods = _ada(cvec, w_ada, b_ada).reshape(DEPTH, ADA_ROWS, 6, D_MODEL)

    cos_all, sin_all = _rope_tables()
    lb_cum = jnp.cumsum(jax.nn.softmax(hg_lb.astype(F32), axis=1), axis=1)
    lb_all = lb_cum - lb_cum[:, :1]
    wr_t = router_w.T
    br = router_b.reshape(N_EXPERTS, 1)
    w1_bf16, w3_bf16, w2_bf16 = (w.astype(BF16) for w in (moe_w1, moe_w3, moe_w2))

    for l in range(DEPTH):
        last = l == DEPTH - 1
        out_rows = SEQ if last else SEQ_ALL
        z = _inproj(xs, norm1_g[l], mods[l], w_in[l].astype(BF16))

        ya = _lru(z, lru_conv_w[l], lru_conv_b[l].reshape(1, WIDTH),
                  jax.vmap(_block_diag)(lru_wr[l]).astype(BF16), lru_br[l],
                  jax.vmap(_block_diag)(lru_wi[l]).astype(BF16), lru_bi[l], lru_lam[l], out_rows)

        lam_init = 0.8 - 0.6 * math.exp(-0.3 * l)
        lf = da_lam[l].astype(F32)
        lam = jnp.exp(jnp.sum(lf[0] * lf[1])) - jnp.exp(jnp.sum(lf[2] * lf[3])) + lam_init
        yb = _attn(z, lam.reshape(1, 1), cos_all, sin_all,
                   jnp.tile(da_subln_g[l], HEADS).reshape(1, WIDTH),
                   with_ctx=not last, out_scale=1.0 - lam_init)

        w_cat = jnp.transpose(sg_w[l], (1, 0, 2)).reshape(SG_CHUNK, HEADS * SG_CHUNK).astype(BF16)
        bias2d = jnp.repeat(sg_b[l].T, HEAD_DIM, axis=1)
        ys = _sgu(z, sg_norm_g[l].reshape(1, WIDTH), w_cat, bias2d, out_rows)

        yh = _hgrn(z, lb_all[:, l], hg_norm_g[l].reshape(1, WIDTH), out_rows)

        xs, hp, pos, counts = _outproj(xs, (ya, yb, ys, yh), w_out[l].astype(BF16), mods[l],
                                       norm2_g[l], wr_t, br, has_ctx=not last)
        xs = _moe(hp, pos, counts, xs, w1_bf16, w3_bf16, w2_bf16, l, mods[l], final_norm_g,
                  has_ctx=not last, final_norm=last)

    return xs.reshape(nb, SEQ, D_MODEL)
```

```python
import functools
import math

import jax
import jax.numpy as jnp
from jax import lax
from jax.experimental import pallas as pl
from jax.experimental.pallas import tpu as pltpu

F32 = jnp.float32
BF16 = jnp.bfloat16

D_MODEL = 1024
SEQ = 2048
CTX_LEN = 256
SEQ_ALL = CTX_LEN + SEQ
DEPTH = 2
GRID_W = 64
EPS = 1e-6
LOG2_E = 1.4426950408889634
WIDTH = 256
HEADS = 4
HEAD_DIM = 64
CONV_W = 4
RG_C = 8.0
DA_QD = 32
ROPE_THETA = 10000.0
SG_CHUNK = 128
N_EXPERTS = 16
N_GROUPS = 4
EXP_PER_GROUP = 4
D_EXPERT = 512
IN_COLS = 3072
COL_A_X, COL_A_G = 0, 1
COL_B_Q, COL_B_K, COL_B_V = 2, 3, 4
COL_C_U, COL_C_V = 5, 6
COL_D_Q, COL_D_FF, COL_D_FB, COL_D_I, COL_D_G = 7, 8, 9, 10, 11

ADA_ROWS = 16
CTX_ROW = 8
VMEM_LIMIT = 56 * 1024 * 1024

LRU_BLK = 8
HG_CHUNK = 16
ATT_TQ = 256
DENSE_TM = 768
MOE_R = 512
HP_COLS = D_MODEL + 128
COMBINE_TM = 1152
COMBINE_TM_LATENT = 1024


def _params(*sem):
    return pltpu.CompilerParams(dimension_semantics=sem, vmem_limit_bytes=VMEM_LIMIT)


def _rms(xf, g):
    return xf * lax.rsqrt(jnp.mean(xf * xf, axis=-1, keepdims=True) + EPS) * g


def _sigmoid(x):
    return 1.0 / (1.0 + jnp.exp(-x))


def _silu(x):
    return x * _sigmoid(x)


def _gelu(x):
    return jax.nn.gelu(x)


def _split3(x):
    hi = x.astype(BF16)
    r = x - hi.astype(F32)
    mid = r.astype(BF16)
    lo = (r - mid.astype(F32)).astype(BF16)
    return hi, mid, lo


def _dot_f32_rhs(m_bf16, x):
    return functools.reduce(jnp.add, [jnp.dot(m_bf16, p, preferred_element_type=F32)
                                      for p in _split3(x)])


def _dot_f32_lhs(x, m_bf16):
    return functools.reduce(jnp.add, [jnp.dot(p, m_bf16, preferred_element_type=F32)
                                      for p in _split3(x)])


def _head_ones(dtype):
    r = lax.broadcasted_iota(jnp.int32, (WIDTH, WIDTH), 0) // HEAD_DIM
    c = lax.broadcasted_iota(jnp.int32, (WIDTH, WIDTH), 1) // HEAD_DIM
    return (r == c).astype(dtype)


def _mod_picker(modl_ref, modc_ref, tm, has_ctx):
    ml = modl_ref[0]
    if not has_ctx:
        return lambda r: ml[r:r + 1]
    mc = modc_ref[0]
    row0 = (pl.program_id(0) % (SEQ_ALL // tm)) * tm
    is_ctx = row0 + lax.broadcasted_iota(jnp.int32, (tm, 1), 0) < CTX_LEN
    return lambda r: jnp.where(is_ctx, mc[r:r + 1], ml[r:r + 1])


def _ada_kernel(c_ref, w_ref, b_ref, o_ref):
    s = _silu(c_ref[...]).astype(BF16)
    o_ref[...] = jnp.dot(s, w_ref[...].astype(BF16), preferred_element_type=F32) + b_ref[...]


def _ada(cvec, w_ada, b_ada):
    tn = 1536
    return pl.pallas_call(
        _ada_kernel,
        grid=(DEPTH, 6 * D_MODEL // tn),
        in_specs=[
            pl.BlockSpec((ADA_ROWS, D_MODEL), lambda l, j: (0, 0)),
            pl.BlockSpec((None, D_MODEL, tn), lambda l, j: (l, 0, j)),
            pl.BlockSpec((None, 1, tn), lambda l, j: (l, 0, j)),
        ],
        out_specs=pl.BlockSpec((None, ADA_ROWS, tn), lambda l, j: (l, 0, j)),
        out_shape=jax.ShapeDtypeStruct((DEPTH, ADA_ROWS, 6 * D_MODEL), F32),
        compiler_params=_params("arbitrary", "arbitrary"),
        name="ada",
    )(cvec, w_ada, b_ada.reshape(DEPTH, 1, 6 * D_MODEL))


def _inproj_kernel(x_ref, g_ref, modl_ref, modc_ref, w_ref, z_ref, *, tm):
    pick = _mod_picker(modl_ref, modc_ref, tm, True)
    h = _rms(x_ref[...], g_ref[...]) * (1.0 + pick(1)) + pick(0)
    z_ref[...] = jnp.dot(h.astype(BF16), w_ref[...], preferred_element_type=F32)


def _mod_specs(tiles_per_batch):
    return [
        pl.BlockSpec((1, 6, D_MODEL), lambda i, *_: (i // tiles_per_batch, 0, 0)),
        pl.BlockSpec((1, 6, D_MODEL), lambda i, *_: (CTX_ROW, 0, 0)),
    ]


def _inproj(x, g, mod, w_bf16):
    n = x.shape[0]
    tm = DENSE_TM
    return pl.pallas_call(
        functools.partial(_inproj_kernel, tm=tm),
        grid=(n // tm,),
        in_specs=[
            pl.BlockSpec((tm, D_MODEL), lambda i: (i, 0)),
            pl.BlockSpec((1, D_MODEL), lambda i: (0, 0)),
            *_mod_specs(SEQ_ALL // tm),
            pl.BlockSpec((D_MODEL, IN_COLS), lambda i: (0, 0)),
        ],
        out_specs=pl.BlockSpec((tm, IN_COLS), lambda i: (i, 0)),
        out_shape=jax.ShapeDtypeStruct((n, IN_COLS), F32),
        compiler_params=_params("arbitrary"),
        name="inproj",
    )(x, g.reshape(1, D_MODEL), mod, mod, w_bf16)


def _lru_kernel(x_ref, gate_ref, cw_ref, cb_ref, wr_ref, br_ref, wi_ref, bi_ref, lam_ref,
                y_ref, a_s, b_s, h_s, *, out_rows):
    x = x_ref[...]
    rows = lax.broadcasted_iota(jnp.int32, (SEQ_ALL, 1), 0)
    seg = rows < CTX_LEN
    u = jnp.zeros_like(x)
    for j in range(CONV_W):
        off = j - CONV_W // 2
        xs = x if off == 0 else pltpu.roll(x, (-off) % SEQ_ALL, 0)
        src = rows + off
        ok = (src >= 0) & (src < SEQ_ALL) & ((src < CTX_LEN) == seg)
        u = u + jnp.where(ok, xs, 0.0) * cw_ref[j:j + 1, :]
    u = u + cb_ref[...]
    ub = u.astype(BF16)
    for d in range(2):
        r = _sigmoid(jnp.dot(ub, wr_ref[d], preferred_element_type=F32) + br_ref[d:d + 1, :])
        i = _sigmoid(jnp.dot(ub, wi_ref[d], preferred_element_type=F32) + bi_ref[d:d + 1, :])
        nl = -lam_ref[d:d + 1, :]
        softplus = jnp.maximum(nl, 0.0) + jnp.log(1.0 + jnp.exp(-jnp.abs(nl)))
        log_a = -RG_C * r * softplus
        a_s[d] = jnp.exp(log_a)
        b_s[d] = jnp.sqrt(1.0 - jnp.exp(2.0 * log_a)) * i * u

    n_blk = SEQ_ALL // LRU_BLK
    n_ctx_blk = CTX_LEN // LRU_BLK
    sub = lax.broadcasted_iota(jnp.int32, (LRU_BLK, 1), 0)

    def block_scan(a, b, reverse):
        s = 1
        while s < LRU_BLK:
            if reverse:
                a_sh = pltpu.roll(a, LRU_BLK - s, 0)
                b_sh = pltpu.roll(b, LRU_BLK - s, 0)
                ok = sub < LRU_BLK - s
            else:
                a_sh = pltpu.roll(a, s, 0)
                b_sh = pltpu.roll(b, s, 0)
                ok = sub >= s
            b = jnp.where(ok, a * b_sh + b, b)
            a = jnp.where(ok, a * a_sh, a)
            s *= 2
        return a, b

    def body(n, carry):
        hf, hb = carry
        rf = pl.multiple_of(n * LRU_BLK, LRU_BLK)
        af, bf = block_scan(a_s[0, pl.ds(rf, LRU_BLK), :], b_s[0, pl.ds(rf, LRU_BLK), :], False)
        h = af * hf + bf
        h_s[0, pl.ds(rf, LRU_BLK), :] = h
        hf = h[LRU_BLK - 1:LRU_BLK, :]
        nb = jnp.where(n < n_ctx_blk, n_ctx_blk - 1 - n, n_blk + n_ctx_blk - 1 - n)
        rb = pl.multiple_of(nb * LRU_BLK, LRU_BLK)
        ab, bb = block_scan(a_s[1, pl.ds(rb, LRU_BLK), :], b_s[1, pl.ds(rb, LRU_BLK), :], True)
        h = ab * hb + bb
        h_s[1, pl.ds(rb, LRU_BLK), :] = h
        hb = h[0:1, :]
        return hf, hb

    zero = jnp.zeros((1, WIDTH), F32)
    lax.fori_loop(0, n_blk, body, (zero, zero))
    first = SEQ_ALL - out_rows
    y_ref[...] = (h_s[0, first:, :] + h_s[1, first:, :]) * _gelu(gate_ref[first:, :])


def _lru(z, cw, cb, wr_bd, br, wi_bd, bi, lam, out_rows):
    nb = z.shape[0] // SEQ_ALL
    full = lambda shape: pl.BlockSpec(shape, lambda b: (0,) * len(shape))
    return pl.pallas_call(
        functools.partial(_lru_kernel, out_rows=out_rows),
        grid=(nb,),
        in_specs=[
            pl.BlockSpec((SEQ_ALL, WIDTH), lambda b: (b, COL_A_X)),
            pl.BlockSpec((SEQ_ALL, WIDTH), lambda b: (b, COL_A_G)),
            full((CONV_W, WIDTH)), full((1, WIDTH)),
            full((2, WIDTH, WIDTH)), full((2, WIDTH)),
            full((2, WIDTH, WIDTH)), full((2, WIDTH)),
            full((2, WIDTH)),
        ],
        out_specs=pl.BlockSpec((out_rows, WIDTH), lambda b: (b, 0)),
        out_shape=jax.ShapeDtypeStruct((nb * out_rows, WIDTH), F32),
        scratch_shapes=[
            pltpu.VMEM((2, SEQ_ALL, WIDTH), F32),
            pltpu.VMEM((2, SEQ_ALL, WIDTH), F32),
            pltpu.VMEM((2, SEQ_ALL, WIDTH), F32),
        ],
        compiler_params=_params("arbitrary"),
        name="rglru",
    )(z, z, cw, cb, wr_bd, br, wi_bd, bi, lam)


def _rope(x, cos, sin_signed):
    lane = lax.broadcasted_iota(jnp.int32, (1, WIDTH), 1)
    first_half = (lane % 16) < 8
    partner = jnp.where(first_half, pltpu.roll(x, WIDTH - 8, 1), pltpu.roll(x, 8, 1))
    return x * cos + partner * sin_signed


def _attn_kernel(lam_ref, q_ref, k_ref, v_ref, cos_ref, sin_ref, g_ref, y_ref, kt_s, v_s,
                 *, with_ctx, out_scale):
    j = pl.program_id(1)

    @pl.when(j == 0)
    def _():
        kr = _rope(k_ref[...], cos_ref[...], sin_ref[...])
        kt_s[...] = kr.T.astype(BF16)
        v = v_ref[...]
        ones = jnp.ones((SEQ_ALL, HEAD_DIM), F32)
        for h in range(HEADS):
            v_s[h] = jnp.concatenate([v[:, h * HEAD_DIM:(h + 1) * HEAD_DIM], ones],
                                     axis=1).astype(BF16)

    lam = lam_ref[0, 0]
    tile = j if with_ctx else j + 1
    row0 = pl.multiple_of(tile * ATT_TQ, ATT_TQ)
    q = _rope(q_ref[...], cos_ref[pl.ds(row0, ATT_TQ), :], sin_ref[pl.ds(row0, ATT_TQ), :])
    q = q * (DA_QD ** -0.5)

    def attend(n_keys):
        outs = []
        for h in range(HEADS):
            scores = []
            for n in range(2):
                c0 = h * HEAD_DIM + n * DA_QD
                scores.append(jnp.dot(q[:, c0:c0 + DA_QD].astype(BF16),
                                      kt_s[c0:c0 + DA_QD, 0:n_keys], preferred_element_type=F32))
            probs = [jnp.exp(s - jnp.max(s, axis=-1, keepdims=True)).astype(BF16) for s in scores]
            branch = []
            for p in probs:
                ov = jnp.dot(p, v_s[h, 0:n_keys, :], preferred_element_type=F32)
                branch.append(ov[:, 0:HEAD_DIM] * (1.0 / ov[:, HEAD_DIM:HEAD_DIM + 1]))
            o = branch[0] - lam * branch[1]
            o = o * lax.rsqrt(jnp.mean(o * o, axis=-1, keepdims=True) + EPS)
            outs.append(o)
        y_ref[...] = jnp.concatenate(outs, axis=-1) * g_ref[...] * out_scale

    if with_ctx:
        @pl.when(j == 0)
        def _():
            attend(CTX_LEN)

        @pl.when(j > 0)
        def _():
            attend(SEQ_ALL)
    else:
        attend(SEQ_ALL)


def _attn(z, lam, cos, sin_signed, sub_g4, with_ctx, out_scale):
    nb = z.shape[0] // SEQ_ALL
    tiles = SEQ_ALL // ATT_TQ
    nq = tiles if with_ctx else tiles - 1
    first = 0 if with_ctx else 1
    return pl.pallas_call(
        functools.partial(_attn_kernel, with_ctx=with_ctx, out_scale=out_scale),
        grid=(nb, nq),
        in_specs=[
            pl.BlockSpec(memory_space=pltpu.SMEM),
            pl.BlockSpec((ATT_TQ, WIDTH), lambda b, j: (b * tiles + j + first, COL_B_Q)),
            pl.BlockSpec((SEQ_ALL, WIDTH), lambda b, j: (b, COL_B_K)),
            pl.BlockSpec((SEQ_ALL, WIDTH), lambda b, j: (b, COL_B_V)),
            pl.BlockSpec((SEQ_ALL, WIDTH), lambda b, j: (0, 0)),
            pl.BlockSpec((SEQ_ALL, WIDTH), lambda b, j: (0, 0)),
            pl.BlockSpec((1, WIDTH), lambda b, j: (0, 0)),
        ],
        out_specs=pl.BlockSpec((ATT_TQ, WIDTH), lambda b, j: (b * nq + j, 0)),
        out_shape=jax.ShapeDtypeStruct((nb * nq * ATT_TQ, WIDTH), F32),
        scratch_shapes=[
            pltpu.VMEM((WIDTH, SEQ_ALL), BF16),
            pltpu.VMEM((HEADS, SEQ_ALL, 2 * HEAD_DIM), BF16),
        ],
        compiler_params=_params("arbitrary", "arbitrary"),
        name="diffattn",
    )(lam, z, z, z, cos, sin_signed, sub_g4)


def _sgu_kernel(u_ref, v_ref, g_ref, w_ref, b_ref, y_ref, *, out_rows):
    row_head = lax.broadcasted_iota(jnp.int32, (HEADS * SG_CHUNK, WIDTH), 0) // SG_CHUNK
    col_head = lax.broadcasted_iota(jnp.int32, (HEADS * SG_CHUNK, WIDTH), 1) // HEAD_DIM
    head_mask = row_head == col_head
    w = w_ref[...]
    bias = b_ref[...]
    g = g_ref[...]
    first = SEQ_ALL - out_rows
    for n in range(first // SG_CHUNK, SEQ_ALL // SG_CHUNK):
        rows = slice(n * SG_CHUNK, (n + 1) * SG_CHUNK)
        vn = _rms(_gelu(v_ref[rows, :]), g).astype(BF16)
        stacked = jnp.where(head_mask, jnp.concatenate([vn] * HEADS, axis=0), jnp.zeros((), BF16))
        vm = jnp.dot(w, stacked, preferred_element_type=F32) + bias
        y_ref[n * SG_CHUNK - first:(n + 1) * SG_CHUNK - first, :] = _gelu(u_ref[rows, :]) * vm


def _sgu(z, norm_g, w_cat, bias2d, out_rows):
    nb = z.shape[0] // SEQ_ALL
    return pl.pallas_call(
        functools.partial(_sgu_kernel, out_rows=out_rows),
        grid=(nb,),
        in_specs=[
            pl.BlockSpec((SEQ_ALL, WIDTH), lambda b: (b, COL_C_U)),
            pl.BlockSpec((SEQ_ALL, WIDTH), lambda b: (b, COL_C_V)),
            pl.BlockSpec((1, WIDTH), lambda b: (0, 0)),
            pl.BlockSpec((SG_CHUNK, HEADS * SG_CHUNK), lambda b: (0, 0)),
            pl.BlockSpec((SG_CHUNK, WIDTH), lambda b: (0, 0)),
        ],
        out_specs=pl.BlockSpec((out_rows, WIDTH), lambda b: (b, 0)),
        out_shape=jax.ShapeDtypeStruct((nb * out_rows, WIDTH), F32),
        compiler_params=_params("arbitrary"),
        name="sgu",
    )(z, z, norm_g, w_cat, bias2d)


def _hgrn_kernel(q_ref, ff_ref, fb_ref, i_ref, gate_ref, lb_ref, gn_ref, y_ref,
                 b_s, k_s, o_s, st_s, stb_s, *, out_rows):
    c = HG_CHUNK
    pre = 128
    ones_bf = _head_ones(BF16)
    pr = lax.broadcasted_iota(jnp.int32, (pre, pre), 0)
    pc = lax.broadcasted_iota(jnp.int32, (pre, pre), 1)
    same = (pr // c) == (pc // c)
    tri = ((same & (pc <= pr)).astype(BF16), (same & (pc >= pr)).astype(BF16))

    for d, f_ref in enumerate((ff_ref, fb_ref)):
        lb = lb_ref[d:d + 1, :]
        for n in range(SEQ_ALL // pre):
            rows = slice(n * pre, (n + 1) * pre)
            f = lb + (1.0 - lb) * _sigmoid(f_ref[rows, :])
            k_s[d, rows, :] = 1.0 - f
            b_s[d, rows, :] = _dot_f32_rhs(tri[d], jnp.log(f) * LOG2_E)
    st_s[...] = jnp.zeros_like(st_s)
    stb_s[...] = jnp.zeros_like(stb_s)

    n_chunks = SEQ_ALL // c
    n_ctx = CTX_LEN // c
    half = c // 2
    sub = lax.broadcasted_iota(jnp.int32, (half, 1), 0)
    lane_head = lax.broadcasted_iota(jnp.int32, (1, 128), 1) // HEAD_DIM
    scale = HEAD_DIM ** -0.5

    def front(d, chunk):
        r0 = pl.multiple_of(chunk * c, c)
        q = q_ref[pl.ds(r0, c), :] * scale
        v = i_ref[pl.ds(r0, c), :]
        b = b_s[d, pl.ds(r0, c), :]
        k = k_s[d, pl.ds(r0, c), :]
        tot = b[c - 1:c, :] if d == 0 else b[0:1, :]
        o = lax.dot_general((q * jnp.exp2(b)).astype(BF16), stb_s[d],
                            (((1,), (1,)), ((), ())), preferred_element_type=F32)
        parts = []
        for s in range(c):
            for tile in range(2):
                lo_row, hi_row = tile * half, (tile + 1) * half
                if (d == 0 and hi_row <= s) or (d == 1 and lo_row > s):
                    parts.append(jnp.zeros((half, WIDTH), F32))
                    continue
                diff = b[lo_row:hi_row, :] - b[s:s + 1, :]
                if d == 0 and lo_row < s:
                    diff = jnp.where(sub + lo_row >= s, diff, -jnp.inf)
                elif d == 1 and hi_row - 1 > s:
                    diff = jnp.where(sub + lo_row <= s, diff, -jnp.inf)
                parts.append(q[lo_row:hi_row, :] * k[s:s + 1, :] * jnp.exp2(diff))
        att = jnp.dot(jnp.concatenate(parts, axis=0).astype(BF16), ones_bf,
                      preferred_element_type=F32)
        ke = (k * jnp.exp2(tot - b)).astype(BF16)
        upd = lax.dot_general(v.astype(BF16), ke, (((0,), (0,)), ((), ())),
                              preferred_element_type=F32)
        return r0, o, att, upd, v, jnp.exp2(tot)

    def back(d, r0, o, att, upd, v, decay):
        for s in range(c):
            o = o + att[s * c:(s + 1) * c, :] * v[s:s + 1, :]
        o_s[d, pl.ds(r0, c), :] = o
        for h in range(HEADS):
            r = slice(h * HEAD_DIM, (h + 1) * HEAD_DIM)
            ct = slice((h // 2) * 128, (h // 2 + 1) * 128)
            new = st_s[d, r, ct] * decay[:, ct] + jnp.where(lane_head == h % 2, upd[r, ct], 0.0)
            st_s[d, r, ct] = new
            stb_s[d, r, ct] = new.astype(BF16)

    def body(n, carry):
        fwd = front(0, n)
        bwd = front(1, jnp.where(n < n_ctx, n_ctx - 1 - n, n_chunks + n_ctx - 1 - n))
        back(0, *fwd)
        back(1, *bwd)
        return carry

    lax.fori_loop(0, n_chunks, body, 0, unroll=4)

    first = SEQ_ALL - out_rows
    o = o_s[0, first:, :] + o_s[1, first:, :]
    ms = _dot_f32_lhs(o * o, ones_bf) * (1.0 / HEAD_DIM)
    y_ref[...] = o * lax.rsqrt(ms + EPS) * gn_ref[...] * _silu(gate_ref[first:, :])


def _hgrn(z, lb, gn, out_rows):
    nb = z.shape[0] // SEQ_ALL
    col = lambda cidx: pl.BlockSpec((SEQ_ALL, WIDTH), lambda b: (b, cidx))
    return pl.pallas_call(
        functools.partial(_hgrn_kernel, out_rows=out_rows),
        grid=(nb,),
        in_specs=[
            col(COL_D_Q), col(COL_D_FF), col(COL_D_FB), col(COL_D_I), col(COL_D_G),
            pl.BlockSpec((2, WIDTH), lambda b: (0, 0)),
            pl.BlockSpec((1, WIDTH), lambda b: (0, 0)),
        ],
        out_specs=pl.BlockSpec((out_rows, WIDTH), lambda b: (b, 0)),
        out_shape=jax.ShapeDtypeStruct((nb * out_rows, WIDTH), F32),
        scratch_shapes=[
            pltpu.VMEM((2, SEQ_ALL, WIDTH), F32),
            pltpu.VMEM((2, SEQ_ALL, WIDTH), F32),
            pltpu.VMEM((2, SEQ_ALL, WIDTH), F32),
            pltpu.VMEM((2, WIDTH, WIDTH), F32),
            pltpu.VMEM((2, WIDTH, WIDTH), BF16),
        ],
        compiler_params=_params("arbitrary"),
        name="hgrn2",
    )(z, z, z, z, z, lb, gn)


def _route(logits):
    rows = [logits[e:e + 1, :] for e in range(N_EXPERTS)]
    m = functools.reduce(jnp.maximum, rows)
    ex = [jnp.exp(r - m) for r in rows]
    inv = 1.0 / functools.reduce(jnp.add, ex)
    sc = [e * inv for e in ex]
    g_score = []
    for g in range(N_GROUPS):
        grp = sc[g * EXP_PER_GROUP:(g + 1) * EXP_PER_GROUP]
        pairs = [grp[a] + grp[b] for a in range(EXP_PER_GROUP) for b in range(a + 1, EXP_PER_GROUP)]
        g_score.append(functools.reduce(jnp.maximum, pairs))
    gate4 = [jnp.zeros_like(sc[0]) for _ in range(EXP_PER_GROUP)]
    onehot = []
    for g in range(N_GROUPS):
        g_ok = None
        for o in range(N_GROUPS):
            if o == g:
                continue
            t = (g_score[g] > g_score[o]) if o < g else (g_score[g] >= g_score[o])
            g_ok = t if g_ok is None else (g_ok & t)
        grp = sc[g * EXP_PER_GROUP:(g + 1) * EXP_PER_GROUP]
        picked = []
        for a in range(EXP_PER_GROUP):
            beaten = jnp.zeros_like(grp[a])
            for o in range(EXP_PER_GROUP):
                if o == a:
                    continue
                t = (grp[o] >= grp[a]) if o < a else (grp[o] > grp[a])
                beaten = beaten + jnp.where(t, 1.0, 0.0)
            picked.append(jnp.where((beaten < 1.5) & g_ok, grp[a], 0.0))
        denom = functools.reduce(jnp.add, picked)
        denom = jnp.where(g_ok, denom, 1.0)
        gate4 = [acc + p / denom for acc, p in zip(gate4, picked)]
        onehot.append(jnp.where(g_ok, 1.0, 0.0))
    return gate4, onehot


def _outproj_kernel(x_ref, ya_ref, yb_ref, ys_ref, yh_ref, w_ref, modl_ref, modc_ref, g_ref,
                    wr_ref, br_ref, xo_ref, hp_ref, pos_ref, cnt_ref, cnt_s, *, tm, has_ctx, cap):
    @pl.when(pl.program_id(0) == 0)
    def _():
        cnt_s[...] = jnp.zeros_like(cnt_s)

    pick = _mod_picker(modl_ref, modc_ref, tm, has_ctx)
    acc = jnp.zeros((tm, D_MODEL), F32)
    for kblk, y_ref in enumerate((ya_ref, yb_ref, ys_ref, yh_ref)):
        acc = acc + jnp.dot(y_ref[...].astype(BF16), w_ref[kblk * WIDTH:(kblk + 1) * WIDTH, :],
                            preferred_element_type=F32)
    x = x_ref[...] + pick(2) * acc
    xo_ref[...] = x
    h = _rms(x, g_ref[...]) * (1.0 + pick(4)) + pick(3)
    nt = (((1,), (1,)), ((), ()))
    wr = wr_ref[...]
    wr_hi = wr.astype(BF16)
    wr_lo = (wr - wr_hi.astype(F32)).astype(BF16)
    h_hi = h.astype(BF16)
    h_lo = (h - h_hi.astype(F32)).astype(BF16)
    logits = (lax.dot_general(wr_hi, h_hi, nt, preferred_element_type=F32)
              + lax.dot_general(wr_hi, h_lo, nt, preferred_element_type=F32)
              + lax.dot_general(wr_lo, h_hi, nt, preferred_element_type=F32)) + br_ref[...]
    gate4, onehot = _route(logits)
    hp_ref[:, 0:D_MODEL] = h
    gate_rows = jnp.concatenate(gate4 + [jnp.zeros((128 - EXP_PER_GROUP, tm), F32)], axis=0)
    hp_ref[:, D_MODEL:] = gate_rows.T
    sel = jnp.concatenate(onehot + [jnp.zeros((8 - N_GROUPS, tm), F32)], axis=0)
    before = (lax.broadcasted_iota(jnp.int32, (tm, tm), 0)
              <= lax.broadcasted_iota(jnp.int32, (tm, tm), 1)).astype(BF16)
    seen = jnp.dot(sel.astype(BF16), before, preferred_element_type=F32)
    carried = cnt_s[:, 0:1]
    base = lax.broadcasted_iota(jnp.int32, (8, 1), 0).astype(F32) * float(cap)
    slot = jnp.sum(sel * (base + carried + seen - 1.0), axis=0, keepdims=True)
    pos_ref[0] = slot.astype(jnp.int32)
    cnt_s[...] = cnt_s[...] + jnp.sum(sel, axis=1, keepdims=True)
    cnt_ref[...] = cnt_s[...].astype(jnp.int32)


def _outproj(x, ys, w_bf16, mod, g2, wr_t, br, has_ctx):
    n = ys[0].shape[0]
    if has_ctx:
        tm = DENSE_TM
        tpb = SEQ_ALL // tm
        x_spec = pl.BlockSpec((tm, D_MODEL), lambda i: (i, 0))
    else:
        tm = CTX_LEN
        tpb = SEQ // tm
        x_spec = pl.BlockSpec((tm, D_MODEL),
                              lambda i: ((i // tpb) * (SEQ_ALL // tm) + 1 + i % tpb, 0))
    tile = lambda w: pl.BlockSpec((tm, w), lambda i: (i, 0))
    return pl.pallas_call(
        functools.partial(_outproj_kernel, tm=tm, has_ctx=has_ctx, cap=n),
        grid=(n // tm,),
        in_specs=[
            x_spec, tile(WIDTH), tile(WIDTH), tile(WIDTH), tile(WIDTH),
            pl.BlockSpec((D_MODEL, D_MODEL), lambda i: (0, 0)),
            *_mod_specs(tpb),
            pl.BlockSpec((1, D_MODEL), lambda i: (0, 0)),
            pl.BlockSpec((N_EXPERTS, D_MODEL), lambda i: (0, 0)),
            pl.BlockSpec((N_EXPERTS, 1), lambda i: (0, 0)),
        ],
        out_specs=[
            tile(D_MODEL), tile(HP_COLS),
            pl.BlockSpec((1, 1, tm), lambda i: (i, 0, 0)),
            pl.BlockSpec((8, 128), lambda i: (0, 0)),
        ],
        out_shape=[
            jax.ShapeDtypeStruct((n, D_MODEL), F32),
            jax.ShapeDtypeStruct((n, HP_COLS), F32),
            jax.ShapeDtypeStruct((n // tm, 1, tm), jnp.int32),
            jax.ShapeDtypeStruct((8, 128), jnp.int32),
        ],
        scratch_shapes=[pltpu.VMEM((8, 128), F32)],
        compiler_params=_params("arbitrary"),
        name="outproj_router",
    )(x, *ys, w_bf16, mod, mod, g2.reshape(1, D_MODEL), wr_t, br)


def _row_copies_wait(dst_rows, sem):
    pltpu.make_async_copy(dst_rows, dst_rows, sem).wait()


def _invert_kernel(pos_ref, pad_ref, tok_ref):
    def clear(p, carry):
        tok_ref[p] = 0
        return carry

    def put(t, carry):
        tok_ref[pos_ref[t]] = t
        return carry

    for r in range(N_GROUPS + 1):
        lax.fori_loop(pad_ref[r, 0], pad_ref[r, 1], clear, 0)
    lax.fori_loop(0, pos_ref.shape[0], put, 0, unroll=8)


def _invert(pos, pad, n_slots):
    return pl.pallas_call(
        _invert_kernel,
        in_specs=[pl.BlockSpec(memory_space=pltpu.SMEM), pl.BlockSpec(memory_space=pltpu.SMEM)],
        out_specs=pl.BlockSpec(memory_space=pltpu.SMEM),
        out_shape=jax.ShapeDtypeStruct((n_slots,), jnp.int32),
        name="moe_invert",
    )(pos, pad)


def _experts_kernel(grp_ref, valid_ref, tok_ref, hp_hbm, w1_ref, w3_ref, w2_ref, ys_ref, buf, sems):
    k = pl.program_id(0)
    n_blk = pl.num_programs(0)
    n_valid = valid_ref[k]
    slot = k % 2

    def gather(blk, buf_slot):
        def body(i, carry):
            base = blk * MOE_R + i * 8
            for j in range(8):
                pltpu.make_async_copy(hp_hbm.at[pl.ds(tok_ref[base + j], 1)],
                                      buf.at[buf_slot, i, pl.ds(j, 1)], sems.at[buf_slot]).start()
            return carry

        lax.fori_loop(0, MOE_R // 8, body, 0)

    @pl.when(k == 0)
    def _():
        gather(0, 0)

    nxt = jnp.minimum(k + 1, n_blk - 1)

    @pl.when((k + 1 < n_blk) & (valid_ref[nxt] > 0))
    def _():
        gather(k + 1, 1 - slot)

    @pl.when(n_valid > 0)
    def _():
        _row_copies_wait(buf.at[slot], sems.at[slot])
        w = buf[slot].reshape(MOE_R, HP_COLS)
        ok = lax.broadcasted_iota(jnp.int32, (MOE_R, 1), 0) < n_valid
        h = jnp.where(ok, w[:, 0:D_MODEL], 0.0).astype(BF16)
        gates = jnp.where(ok, w[:, D_MODEL:], 0.0)
        acc = jnp.zeros((MOE_R, D_MODEL), F32)
        for e in range(EXP_PER_GROUP):
            a = jnp.dot(h, w1_ref[e], preferred_element_type=F32)
            b = jnp.dot(h, w3_ref[e], preferred_element_type=F32)
            he = (_silu(a) * b * gates[:, e:e + 1]).astype(BF16)
            acc = acc + jnp.dot(he, w2_ref[e], preferred_element_type=F32)
        ys_ref[...] = acc

    @pl.when(n_valid == 0)
    def _():
        ys_ref[...] = jnp.zeros_like(ys_ref)


def _experts(hp, tok, blk_grp, blk_valid, w1, w3, w2, layer):
    n_slots = tok.shape[0]
    up = pl.BlockSpec((None, EXP_PER_GROUP, D_MODEL, D_EXPERT),
                      lambda k, grp, valid, tok: (layer, grp[k], 0, 0))
    down = pl.BlockSpec((None, EXP_PER_GROUP, D_EXPERT, D_MODEL),
                        lambda k, grp, valid, tok: (layer, grp[k], 0, 0))
    grid_spec = pltpu.PrefetchScalarGridSpec(
        num_scalar_prefetch=3,
        grid=(n_slots // MOE_R,),
        in_specs=[pl.BlockSpec(memory_space=pl.ANY), up, up, down],
        out_specs=pl.BlockSpec((MOE_R, D_MODEL), lambda k, grp, valid, tok: (k, 0)),
        scratch_shapes=[pltpu.VMEM((2, MOE_R // 8, 8, HP_COLS), F32),
                        pltpu.SemaphoreType.DMA((2,))],
    )
    return pl.pallas_call(
        _experts_kernel,
        grid_spec=grid_spec,
        out_shape=jax.ShapeDtypeStruct((n_slots, D_MODEL), F32),
        compiler_params=_params("arbitrary"),
        name="moe_experts",
    )(blk_grp, blk_valid, tok, hp, w1, w3, w2)


def _combine_kernel(pos_ref, x_ref, modl_ref, modc_ref, fg_ref, ys_hbm, o_ref, buf, sem,
                    *, tm, has_ctx, final_norm):
    def body(i, carry):
        for j in range(8):
            pltpu.make_async_copy(ys_hbm.at[pl.ds(pos_ref[0, 0, i * 8 + j], 1)],
                                  buf.at[i, pl.ds(j, 1)], sem).start()
        return carry

    lax.fori_loop(0, tm // 8, body, 0)
    _row_copies_wait(buf, sem)
    pick = _mod_picker(modl_ref, modc_ref, tm, has_ctx)
    x = x_ref[...] + pick(5) * buf[...].reshape(tm, D_MODEL)
    o_ref[...] = _rms(x, fg_ref[...]) if final_norm else x


def _combine(pos, x, ys, mod, final_g, has_ctx, final_norm):
    n = x.shape[0]
    tm = COMBINE_TM if has_ctx else COMBINE_TM_LATENT
    tpb = (SEQ_ALL if has_ctx else SEQ) // tm
    return pl.pallas_call(
        functools.partial(_combine_kernel, tm=tm, has_ctx=has_ctx, final_norm=final_norm),
        grid=(n // tm,),
        in_specs=[
            pl.BlockSpec((1, 1, tm), lambda i: (i, 0, 0), memory_space=pltpu.SMEM),
            pl.BlockSpec((tm, D_MODEL), lambda i: (i, 0)),
            *_mod_specs(tpb),
            pl.BlockSpec((1, D_MODEL), lambda i: (0, 0)),
            pl.BlockSpec(memory_space=pl.ANY),
        ],
        out_specs=pl.BlockSpec((tm, D_MODEL), lambda i: (i, 0)),
        out_shape=jax.ShapeDtypeStruct((n, D_MODEL), F32),
        scratch_shapes=[pltpu.VMEM((tm // 8, 8, D_MODEL), F32), pltpu.SemaphoreType.DMA],
        compiler_params=_params("arbitrary"),
        name="moe_combine",
    )(pos.reshape(n // tm, 1, tm), x, mod, mod, final_g.reshape(1, D_MODEL), ys)


def _routing_tables(slot, counts, n):
    n_blk = n // MOE_R + N_GROUPS
    n_g = counts[:N_GROUPS, 0]
    nb = (n_g + MOE_R - 1) // MOE_R
    ends = jnp.cumsum(nb)
    starts = ends - nb
    g_tok = slot // n
    pos = slot - g_tok * n + starts[g_tok] * MOE_R
    k = jnp.arange(n_blk, dtype=jnp.int32)
    g = jnp.minimum(jnp.sum((k[:, None] >= ends[None, :]).astype(jnp.int32), axis=1), N_GROUPS - 1)
    valid = jnp.where(k < ends[-1], jnp.clip(n_g[g] - (k - starts[g]) * MOE_R, 0, MOE_R), 0)
    pad = jnp.stack([jnp.append(starts * MOE_R + n_g, ends[-1] * MOE_R),
                     jnp.append(ends * MOE_R, n_blk * MOE_R)], axis=1)
    return (pos.astype(jnp.int32), g.astype(jnp.int32), valid.astype(jnp.int32),
            pad.astype(jnp.int32))


def _moe(hp, slot, counts, x, w1, w3, w2, layer, mod, final_g, has_ctx, final_norm):
    n = x.shape[0]
    assert n % MOE_R == 0
    pos, blk_grp, blk_valid, pad = _routing_tables(slot.reshape(n), counts, n)
    tok = _invert(pos, pad, n + N_GROUPS * MOE_R)
    ys = _experts(hp, tok, blk_grp, blk_valid, w1, w3, w2, layer)
    return _combine(pos, x, ys, mod, final_g, has_ctx, final_norm)


def _block_diag(w):
    eye = jnp.eye(HEADS, dtype=w.dtype)
    return jnp.einsum('hij,hg->higj', w, eye).reshape(WIDTH, WIDTH)


def _rope_tables():
    rows = SEQ // GRID_W
    row_ids = jnp.repeat(jnp.arange(rows, dtype=F32), GRID_W)
    col_ids = jnp.tile(jnp.arange(GRID_W, dtype=F32), rows)
    n_freq = DA_QD // 4
    freqs = ROPE_THETA ** (-jnp.arange(n_freq, dtype=F32) / n_freq)
    ang = jnp.stack([row_ids[:, None] * freqs, col_ids[:, None] * freqs], axis=1)
    cos = jnp.cos(ang)
    sin = jnp.sin(ang)
    cos_l = jnp.broadcast_to(cos[:, None, None, :, None, :], (SEQ, HEADS, 2, 2, 2, n_freq))
    sin_l = jnp.broadcast_to(sin[:, None, None, :, None, :], (SEQ, HEADS, 2, 2, 2, n_freq))
    sign = jnp.array([-1.0, 1.0], F32)[None, None, None, None, :, None]
    cos_l = cos_l.reshape(SEQ, WIDTH)
    sin_l = (sin_l * sign).reshape(SEQ, WIDTH)
    cos_all = jnp.concatenate([jnp.ones((CTX_LEN, WIDTH), F32), cos_l], axis=0)
    sin_all = jnp.concatenate([jnp.zeros((CTX_LEN, WIDTH), F32), sin_l], axis=0)
    return cos_all, sin_all


def kernel(x, c, ctx, c_ctx, w_ada, b_ada, norm1_g, norm2_g, w_in, w_out, lru_conv_w, lru_conv_b,
           lru_wr, lru_br, lru_wi, lru_bi, lru_lam, da_lam, da_subln_g, sg_norm_g, sg_w, sg_b,
           hg_lb, hg_norm_g, router_w, router_b, moe_w1, moe_w3, moe_w2, final_norm_g):
    nb = x.shape[0]
    assert nb <= CTX_ROW and x.shape[1:] == (SEQ, D_MODEL) and ctx.shape[1:] == (CTX_LEN, D_MODEL)
    xs = jnp.concatenate([ctx, x], axis=1).reshape(nb * SEQ_ALL, D_MODEL)

    cvec = jnp.zeros((ADA_ROWS, D_MODEL), F32).at[:nb].set(c).at[CTX_ROW].set(c_ctx)
    mods = _ada(cvec, w_ada, b_ada).reshape(DEPTH, ADA_ROWS, 6, D_MODEL)

    cos_all, sin_all = _rope_tables()
    lb_cum = jnp.cumsum(jax.nn.softmax(hg_lb.astype(F32), axis=1), axis=1)
    lb_all = lb_cum - lb_cum[:, :1]
    wr_t = router_w.T
    br = router_b.reshape(N_EXPERTS, 1)
    w1_bf16, w3_bf16, w2_bf16 = (w.astype(BF16) for w in (moe_w1, moe_w3, moe_w2))

    for l in range(DEPTH):
        last = l == DEPTH - 1
        out_rows = SEQ if last else SEQ_ALL
        z = _inproj(xs, norm1_g[l], mods[l], w_in[l].astype(BF16))

        ya = _lru(z, lru_conv_w[l], lru_conv_b[l].reshape(1, WIDTH),
                  jax.vmap(_block_diag)(lru_wr[l]).astype(BF16), lru_br[l],
                  jax.vmap(_block_diag)(lru_wi[l]).astype(BF16), lru_bi[l], lru_lam[l], out_rows)

        lam_init = 0.8 - 0.6 * math.exp(-0.3 * l)
        lf = da_lam[l].astype(F32)
        lam = jnp.exp(jnp.sum(lf[0] * lf[1])) - jnp.exp(jnp.sum(lf[2] * lf[3])) + lam_init
        yb = _attn(z, lam.reshape(1, 1), cos_all, sin_all,
                   jnp.tile(da_subln_g[l], HEADS).reshape(1, WIDTH),
                   with_ctx=not last, out_scale=1.0 - lam_init)

        w_cat = jnp.transpose(sg_w[l], (1, 0, 2)).reshape(SG_CHUNK, HEADS * SG_CHUNK).astype(BF16)
        bias2d = jnp.repeat(sg_b[l].T, HEAD_DIM, axis=1)
        ys = _sgu(z, sg_norm_g[l].reshape(1, WIDTH), w_cat, bias2d, out_rows)

        yh = _hgrn(z, lb_all[:, l], hg_norm_g[l].reshape(1, WIDTH), out_rows)

        xs, hp, pos, counts = _outproj(xs, (ya, yb, ys, yh), w_out[l].astype(BF16), mods[l],
                                       norm2_g[l], wr_t, br, has_ctx=not last)
        xs = _moe(hp, pos, counts, xs, w1_bf16, w3_bf16, w2_bf16, l, mods[l], final_norm_g,
                  has_ctx=not last, final_norm=last)

    return xs.reshape(nb, SEQ, D_MODEL)
```

```python
import functools
import math

import jax
import jax.numpy as jnp
from jax import lax
from jax.experimental import pallas as pl
from jax.experimental.pallas import tpu as pltpu

F32 = jnp.float32
BF16 = jnp.bfloat16

D_MODEL = 1024
SEQ = 2048
CTX_LEN = 256
SEQ_ALL = CTX_LEN + SEQ
DEPTH = 2
GRID_W = 64
EPS = 1e-6
LOG2_E = 1.4426950408889634
WIDTH = 256
HEADS = 4
HEAD_DIM = 64
CONV_W = 4
RG_C = 8.0
DA_QD = 32
ROPE_THETA = 10000.0
SG_CHUNK = 128
N_EXPERTS = 16
N_GROUPS = 4
EXP_PER_GROUP = 4
D_EXPERT = 512
IN_COLS = 3072
COL_A_X, COL_A_G = 0, 1
COL_B_Q, COL_B_K, COL_B_V = 2, 3, 4
COL_C_U, COL_C_V = 5, 6
COL_D_Q, COL_D_FF, COL_D_FB, COL_D_I, COL_D_G = 7, 8, 9, 10, 11

ADA_ROWS = 16
CTX_ROW = 8
VMEM_LIMIT = 56 * 1024 * 1024

LRU_BLK = 8
HG_CHUNK = 16
ATT_TQ = 256
DENSE_TM = 768
MOE_R = 512
PAIRS = tuple((a, b) for a in range(EXP_PER_GROUP) for b in range(a + 1, EXP_PER_GROUP))
N_BUCKETS = N_GROUPS * len(PAIRS)
BUCKET_ROWS = 32
HP_COLS = D_MODEL + 128
COMBINE_TM = 1152
COMBINE_TM_LATENT = 1024


def _params(*sem):
    return pltpu.CompilerParams(dimension_semantics=sem, vmem_limit_bytes=VMEM_LIMIT)


def _rms(xf, g):
    return xf * lax.rsqrt(jnp.mean(xf * xf, axis=-1, keepdims=True) + EPS) * g


def _sigmoid(x):
    return 1.0 / (1.0 + jnp.exp(-x))


def _silu(x):
    return x * _sigmoid(x)


def _gelu(x):
    return jax.nn.gelu(x)


def _split3(x):
    hi = x.astype(BF16)
    r = x - hi.astype(F32)
    mid = r.astype(BF16)
    lo = (r - mid.astype(F32)).astype(BF16)
    return hi, mid, lo


def _dot_f32_rhs(m_bf16, x):
    return functools.reduce(jnp.add, [jnp.dot(m_bf16, p, preferred_element_type=F32)
                                      for p in _split3(x)])


def _dot_f32_lhs(x, m_bf16):
    return functools.reduce(jnp.add, [jnp.dot(p, m_bf16, preferred_element_type=F32)
                                      for p in _split3(x)])


def _head_ones(dtype):
    r = lax.broadcasted_iota(jnp.int32, (WIDTH, WIDTH), 0) // HEAD_DIM
    c = lax.broadcasted_iota(jnp.int32, (WIDTH, WIDTH), 1) // HEAD_DIM
    return (r == c).astype(dtype)


def _mod_picker(modl_ref, modc_ref, tm, has_ctx):
    ml = modl_ref[0]
    if not has_ctx:
        return lambda r: ml[r:r + 1]
    mc = modc_ref[0]
    row0 = (pl.program_id(0) % (SEQ_ALL // tm)) * tm
    is_ctx = row0 + lax.broadcasted_iota(jnp.int32, (tm, 1), 0) < CTX_LEN
    return lambda r: jnp.where(is_ctx, mc[r:r + 1], ml[r:r + 1])


def _ada_kernel(c_ref, w_ref, b_ref, o_ref):
    s = _silu(c_ref[...]).astype(BF16)
    o_ref[...] = jnp.dot(s, w_ref[...].astype(BF16), preferred_element_type=F32) + b_ref[...]


def _ada(cvec, w_ada, b_ada):
    tn = 1536
    return pl.pallas_call(
        _ada_kernel,
        grid=(DEPTH, 6 * D_MODEL // tn),
        in_specs=[
            pl.BlockSpec((ADA_ROWS, D_MODEL), lambda l, j: (0, 0)),
            pl.BlockSpec((None, D_MODEL, tn), lambda l, j: (l, 0, j)),
            pl.BlockSpec((None, 1, tn), lambda l, j: (l, 0, j)),
        ],
        out_specs=pl.BlockSpec((None, ADA_ROWS, tn), lambda l, j: (l, 0, j)),
        out_shape=jax.ShapeDtypeStruct((DEPTH, ADA_ROWS, 6 * D_MODEL), F32),
        compiler_params=_params("arbitrary", "arbitrary"),
        name="ada",
    )(cvec, w_ada, b_ada.reshape(DEPTH, 1, 6 * D_MODEL))


def _inproj_kernel(x_ref, g_ref, modl_ref, modc_ref, w_ref, z_ref, *, tm):
    pick = _mod_picker(modl_ref, modc_ref, tm, True)
    h = _rms(x_ref[...], g_ref[...]) * (1.0 + pick(1)) + pick(0)
    z_ref[...] = jnp.dot(h.astype(BF16), w_ref[...], preferred_element_type=F32)


def _mod_specs(tiles_per_batch):
    return [
        pl.BlockSpec((1, 6, D_MODEL), lambda i, *_: (i // tiles_per_batch, 0, 0)),
        pl.BlockSpec((1, 6, D_MODEL), lambda i, *_: (CTX_ROW, 0, 0)),
    ]


def _inproj(x, g, mod, w_bf16):
    n = x.shape[0]
    tm = DENSE_TM
    return pl.pallas_call(
        functools.partial(_inproj_kernel, tm=tm),
        grid=(n // tm,),
        in_specs=[
            pl.BlockSpec((tm, D_MODEL), lambda i: (i, 0)),
            pl.BlockSpec((1, D_MODEL), lambda i: (0, 0)),
            *_mod_specs(SEQ_ALL // tm),
            pl.BlockSpec((D_MODEL, IN_COLS), lambda i: (0, 0)),
        ],
        out_specs=pl.BlockSpec((tm, IN_COLS), lambda i: (i, 0)),
        out_shape=jax.ShapeDtypeStruct((n, IN_COLS), F32),
        compiler_params=_params("arbitrary"),
        name="inproj",
    )(x, g.reshape(1, D_MODEL), mod, mod, w_bf16)


def _lru_kernel(x_ref, gate_ref, cw_ref, cb_ref, wr_ref, br_ref, wi_ref, bi_ref, lam_ref,
                y_ref, a_s, b_s, h_s, *, out_rows):
    x = x_ref[...]
    rows = lax.broadcasted_iota(jnp.int32, (SEQ_ALL, 1), 0)
    seg = rows < CTX_LEN
    u = jnp.zeros_like(x)
    for j in range(CONV_W):
        off = j - CONV_W // 2
        xs = x if off == 0 else pltpu.roll(x, (-off) % SEQ_ALL, 0)
        src = rows + off
        ok = (src >= 0) & (src < SEQ_ALL) & ((src < CTX_LEN) == seg)
        u = u + jnp.where(ok, xs, 0.0) * cw_ref[j:j + 1, :]
    u = u + cb_ref[...]
    ub = u.astype(BF16)
    for d in range(2):
        r = _sigmoid(jnp.dot(ub, wr_ref[d], preferred_element_type=F32) + br_ref[d:d + 1, :])
        i = _sigmoid(jnp.dot(ub, wi_ref[d], preferred_element_type=F32) + bi_ref[d:d + 1, :])
        nl = -lam_ref[d:d + 1, :]
        softplus = jnp.maximum(nl, 0.0) + jnp.log(1.0 + jnp.exp(-jnp.abs(nl)))
        log_a = -RG_C * r * softplus
        a_s[d] = jnp.exp(log_a)
        b_s[d] = jnp.sqrt(1.0 - jnp.exp(2.0 * log_a)) * i * u

    n_blk = SEQ_ALL // LRU_BLK
    n_ctx_blk = CTX_LEN // LRU_BLK
    sub = lax.broadcasted_iota(jnp.int32, (LRU_BLK, 1), 0)

    def block_scan(a, b, reverse):
        s = 1
        while s < LRU_BLK:
            if reverse:
                a_sh = pltpu.roll(a, LRU_BLK - s, 0)
                b_sh = pltpu.roll(b, LRU_BLK - s, 0)
                ok = sub < LRU_BLK - s
            else:
                a_sh = pltpu.roll(a, s, 0)
                b_sh = pltpu.roll(b, s, 0)
                ok = sub >= s
            b = jnp.where(ok, a * b_sh + b, b)
            a = jnp.where(ok, a * a_sh, a)
            s *= 2
        return a, b

    def body(n, carry):
        hf, hb = carry
        rf = pl.multiple_of(n * LRU_BLK, LRU_BLK)
        af, bf = block_scan(a_s[0, pl.ds(rf, LRU_BLK), :], b_s[0, pl.ds(rf, LRU_BLK), :], False)
        h = af * hf + bf
        h_s[0, pl.ds(rf, LRU_BLK), :] = h
        hf = h[LRU_BLK - 1:LRU_BLK, :]
        nb = jnp.where(n < n_ctx_blk, n_ctx_blk - 1 - n, n_blk + n_ctx_blk - 1 - n)
        rb = pl.multiple_of(nb * LRU_BLK, LRU_BLK)
        ab, bb = block_scan(a_s[1, pl.ds(rb, LRU_BLK), :], b_s[1, pl.ds(rb, LRU_BLK), :], True)
        h = ab * hb + bb
        h_s[1, pl.ds(rb, LRU_BLK), :] = h
        hb = h[0:1, :]
        return hf, hb

    zero = jnp.zeros((1, WIDTH), F32)
    lax.fori_loop(0, n_blk, body, (zero, zero))
    first = SEQ_ALL - out_rows
    y_ref[...] = (h_s[0, first:, :] + h_s[1, first:, :]) * _gelu(gate_ref[first:, :])


def _lru(z, cw, cb, wr_bd, br, wi_bd, bi, lam, out_rows):
    nb = z.shape[0] // SEQ_ALL
    full = lambda shape: pl.BlockSpec(shape, lambda b: (0,) * len(shape))
    return pl.pallas_call(
        functools.partial(_lru_kernel, out_rows=out_rows),
        grid=(nb,),
        in_specs=[
            pl.BlockSpec((SEQ_ALL, WIDTH), lambda b: (b, COL_A_X)),
            pl.BlockSpec((SEQ_ALL, WIDTH), lambda b: (b, COL_A_G)),
            full((CONV_W, WIDTH)), full((1, WIDTH)),
            full((2, WIDTH, WIDTH)), full((2, WIDTH)),
            full((2, WIDTH, WIDTH)), full((2, WIDTH)),
            full((2, WIDTH)),
        ],
        out_specs=pl.BlockSpec((out_rows, WIDTH), lambda b: (b, 0)),
        out_shape=jax.ShapeDtypeStruct((nb * out_rows, WIDTH), F32),
        scratch_shapes=[
            pltpu.VMEM((2, SEQ_ALL, WIDTH), F32),
            pltpu.VMEM((2, SEQ_ALL, WIDTH), F32),
            pltpu.VMEM((2, SEQ_ALL, WIDTH), F32),
        ],
        compiler_params=_params("arbitrary"),
        name="rglru",
    )(z, z, cw, cb, wr_bd, br, wi_bd, bi, lam)


def _rope(x, cos, sin_signed):
    lane = lax.broadcasted_iota(jnp.int32, (1, WIDTH), 1)
    first_half = (lane % 16) < 8
    partner = jnp.where(first_half, pltpu.roll(x, WIDTH - 8, 1), pltpu.roll(x, 8, 1))
    return x * cos + partner * sin_signed


def _attn_kernel(lam_ref, q_ref, k_ref, v_ref, cos_ref, sin_ref, g_ref, y_ref, kt_s, v_s,
                 *, with_ctx, out_scale):
    j = pl.program_id(1)

    @pl.when(j == 0)
    def _():
        kr = _rope(k_ref[...], cos_ref[...], sin_ref[...])
        kt_s[...] = kr.T.astype(BF16)
        v = v_ref[...]
        ones = jnp.ones((SEQ_ALL, HEAD_DIM), F32)
        for h in range(HEADS):
            v_s[h] = jnp.concatenate([v[:, h * HEAD_DIM:(h + 1) * HEAD_DIM], ones],
                                     axis=1).astype(BF16)

    lam = lam_ref[0, 0]
    tile = j if with_ctx else j + 1
    row0 = pl.multiple_of(tile * ATT_TQ, ATT_TQ)
    q = _rope(q_ref[...], cos_ref[pl.ds(row0, ATT_TQ), :], sin_ref[pl.ds(row0, ATT_TQ), :])
    q = q * (DA_QD ** -0.5)

    def attend(n_keys):
        outs = []
        for h in range(HEADS):
            scores = []
            for n in range(2):
                c0 = h * HEAD_DIM + n * DA_QD
                scores.append(jnp.dot(q[:, c0:c0 + DA_QD].astype(BF16),
                                      kt_s[c0:c0 + DA_QD, 0:n_keys], preferred_element_type=F32))
            probs = [jnp.exp(s - jnp.max(s, axis=-1, keepdims=True)).astype(BF16) for s in scores]
            branch = []
            for p in probs:
                ov = jnp.dot(p, v_s[h, 0:n_keys, :], preferred_element_type=F32)
                branch.append(ov[:, 0:HEAD_DIM] * (1.0 / ov[:, HEAD_DIM:HEAD_DIM + 1]))
            o = branch[0] - lam * branch[1]
            o = o * lax.rsqrt(jnp.mean(o * o, axis=-1, keepdims=True) + EPS)
            outs.append(o)
        y_ref[...] = jnp.concatenate(outs, axis=-1) * g_ref[...] * out_scale

    if with_ctx:
        @pl.when(j == 0)
        def _():
            attend(CTX_LEN)

        @pl.when(j > 0)
        def _():
            attend(SEQ_ALL)
    else:
        attend(SEQ_ALL)


def _attn(z, lam, cos, sin_signed, sub_g4, with_ctx, out_scale):
    nb = z.shape[0] // SEQ_ALL
    tiles = SEQ_ALL // ATT_TQ
    nq = tiles if with_ctx else tiles - 1
    first = 0 if with_ctx else 1
    return pl.pallas_call(
        functools.partial(_attn_kernel, with_ctx=with_ctx, out_scale=out_scale),
        grid=(nb, nq),
        in_specs=[
            pl.BlockSpec(memory_space=pltpu.SMEM),
            pl.BlockSpec((ATT_TQ, WIDTH), lambda b, j: (b * tiles + j + first, COL_B_Q)),
            pl.BlockSpec((SEQ_ALL, WIDTH), lambda b, j: (b, COL_B_K)),
            pl.BlockSpec((SEQ_ALL, WIDTH), lambda b, j: (b, COL_B_V)),
            pl.BlockSpec((SEQ_ALL, WIDTH), lambda b, j: (0, 0)),
            pl.BlockSpec((SEQ_ALL, WIDTH), lambda b, j: (0, 0)),
            pl.BlockSpec((1, WIDTH), lambda b, j: (0, 0)),
        ],
        out_specs=pl.BlockSpec((ATT_TQ, WIDTH), lambda b, j: (b * nq + j, 0)),
        out_shape=jax.ShapeDtypeStruct((nb * nq * ATT_TQ, WIDTH), F32),
        scratch_shapes=[
            pltpu.VMEM((WIDTH, SEQ_ALL), BF16),
            pltpu.VMEM((HEADS, SEQ_ALL, 2 * HEAD_DIM), BF16),
        ],
        compiler_params=_params("arbitrary", "arbitrary"),
        name="diffattn",
    )(lam, z, z, z, cos, sin_signed, sub_g4)


def _sgu_kernel(u_ref, v_ref, g_ref, w_ref, b_ref, y_ref, *, out_rows):
    row_head = lax.broadcasted_iota(jnp.int32, (HEADS * SG_CHUNK, WIDTH), 0) // SG_CHUNK
    col_head = lax.broadcasted_iota(jnp.int32, (HEADS * SG_CHUNK, WIDTH), 1) // HEAD_DIM
    head_mask = row_head == col_head
    w = w_ref[...]
    bias = b_ref[...]
    g = g_ref[...]
    first = SEQ_ALL - out_rows
    for n in range(first // SG_CHUNK, SEQ_ALL // SG_CHUNK):
        rows = slice(n * SG_CHUNK, (n + 1) * SG_CHUNK)
        vn = _rms(_gelu(v_ref[rows, :]), g).astype(BF16)
        stacked = jnp.where(head_mask, jnp.concatenate([vn] * HEADS, axis=0), jnp.zeros((), BF16))
        vm = jnp.dot(w, stacked, preferred_element_type=F32) + bias
        y_ref[n * SG_CHUNK - first:(n + 1) * SG_CHUNK - first, :] = _gelu(u_ref[rows, :]) * vm


def _sgu(z, norm_g, w_cat, bias2d, out_rows):
    nb = z.shape[0] // SEQ_ALL
    return pl.pallas_call(
        functools.partial(_sgu_kernel, out_rows=out_rows),
        grid=(nb,),
        in_specs=[
            pl.BlockSpec((SEQ_ALL, WIDTH), lambda b: (b, COL_C_U)),
            pl.BlockSpec((SEQ_ALL, WIDTH), lambda b: (b, COL_C_V)),
            pl.BlockSpec((1, WIDTH), lambda b: (0, 0)),
            pl.BlockSpec((SG_CHUNK, HEADS * SG_CHUNK), lambda b: (0, 0)),
            pl.BlockSpec((SG_CHUNK, WIDTH), lambda b: (0, 0)),
        ],
        out_specs=pl.BlockSpec((out_rows, WIDTH), lambda b: (b, 0)),
        out_shape=jax.ShapeDtypeStruct((nb * out_rows, WIDTH), F32),
        compiler_params=_params("arbitrary"),
        name="sgu",
    )(z, z, norm_g, w_cat, bias2d)


def _hgrn_kernel(q_ref, ff_ref, fb_ref, i_ref, gate_ref, lb_ref, gn_ref, y_ref,
                 b_s, k_s, o_s, st_s, stb_s, *, out_rows):
    c = HG_CHUNK
    pre = 128
    ones_bf = _head_ones(BF16)
    pr = lax.broadcasted_iota(jnp.int32, (pre, pre), 0)
    pc = lax.broadcasted_iota(jnp.int32, (pre, pre), 1)
    same = (pr // c) == (pc // c)
    tri = ((same & (pc <= pr)).astype(BF16), (same & (pc >= pr)).astype(BF16))

    for d, f_ref in enumerate((ff_ref, fb_ref)):
        lb = lb_ref[d:d + 1, :]
        for n in range(SEQ_ALL // pre):
            rows = slice(n * pre, (n + 1) * pre)
            f = lb + (1.0 - lb) * _sigmoid(f_ref[rows, :])
            k_s[d, rows, :] = 1.0 - f
            b_s[d, rows, :] = _dot_f32_rhs(tri[d], jnp.log(f) * LOG2_E)
    st_s[...] = jnp.zeros_like(st_s)
    stb_s[...] = jnp.zeros_like(stb_s)

    n_chunks = SEQ_ALL // c
    n_ctx = CTX_LEN // c
    half = c // 2
    sub = lax.broadcasted_iota(jnp.int32, (half, 1), 0)
    lane_head = lax.broadcasted_iota(jnp.int32, (1, 128), 1) // HEAD_DIM
    scale = HEAD_DIM ** -0.5

    def front(d, chunk):
        r0 = pl.multiple_of(chunk * c, c)
        q = q_ref[pl.ds(r0, c), :] * scale
        v = i_ref[pl.ds(r0, c), :]
        b = b_s[d, pl.ds(r0, c), :]
        k = k_s[d, pl.ds(r0, c), :]
        tot = b[c - 1:c, :] if d == 0 else b[0:1, :]
        o = lax.dot_general((q * jnp.exp2(b)).astype(BF16), stb_s[d],
                            (((1,), (1,)), ((), ())), preferred_element_type=F32)
        parts = []
        for s in range(c):
            for tile in range(2):
                lo_row, hi_row = tile * half, (tile + 1) * half
                if (d == 0 and hi_row <= s) or (d == 1 and lo_row > s):
                    parts.append(jnp.zeros((half, WIDTH), F32))
                    continue
                diff = b[lo_row:hi_row, :] - b[s:s + 1, :]
                if d == 0 and lo_row < s:
                    diff = jnp.where(sub + lo_row >= s, diff, -jnp.inf)
                elif d == 1 and hi_row - 1 > s:
                    diff = jnp.where(sub + lo_row <= s, diff, -jnp.inf)
                parts.append(q[lo_row:hi_row, :] * k[s:s + 1, :] * jnp.exp2(diff))
        att = jnp.dot(jnp.concatenate(parts, axis=0).astype(BF16), ones_bf,
                      preferred_element_type=F32)
        ke = (k * jnp.exp2(tot - b)).astype(BF16)
        upd = lax.dot_general(v.astype(BF16), ke, (((0,), (0,)), ((), ())),
                              preferred_element_type=F32)
        return r0, o, att, upd, v, jnp.exp2(tot)

    def back(d, r0, o, att, upd, v, decay):
        for s in range(c):
            o = o + att[s * c:(s + 1) * c, :] * v[s:s + 1, :]
        o_s[d, pl.ds(r0, c), :] = o
        for h in range(HEADS):
            r = slice(h * HEAD_DIM, (h + 1) * HEAD_DIM)
            ct = slice((h // 2) * 128, (h // 2 + 1) * 128)
            new = st_s[d, r, ct] * decay[:, ct] + jnp.where(lane_head == h % 2, upd[r, ct], 0.0)
            st_s[d, r, ct] = new
            stb_s[d, r, ct] = new.astype(BF16)

    def body(n, carry):
        fwd = front(0, n)
        bwd = front(1, jnp.where(n < n_ctx, n_ctx - 1 - n, n_chunks + n_ctx - 1 - n))
        back(0, *fwd)
        back(1, *bwd)
        return carry

    lax.fori_loop(0, n_chunks, body, 0, unroll=4)

    first = SEQ_ALL - out_rows
    o = o_s[0, first:, :] + o_s[1, first:, :]
    ms = _dot_f32_lhs(o * o, ones_bf) * (1.0 / HEAD_DIM)
    y_ref[...] = o * lax.rsqrt(ms + EPS) * gn_ref[...] * _silu(gate_ref[first:, :])


def _hgrn(z, lb, gn, out_rows):
    nb = z.shape[0] // SEQ_ALL
    col = lambda cidx: pl.BlockSpec((SEQ_ALL, WIDTH), lambda b: (b, cidx))
    return pl.pallas_call(
        functools.partial(_hgrn_kernel, out_rows=out_rows),
        grid=(nb,),
        in_specs=[
            col(COL_D_Q), col(COL_D_FF), col(COL_D_FB), col(COL_D_I), col(COL_D_G),
            pl.BlockSpec((2, WIDTH), lambda b: (0, 0)),
            pl.BlockSpec((1, WIDTH), lambda b: (0, 0)),
        ],
        out_specs=pl.BlockSpec((out_rows, WIDTH), lambda b: (b, 0)),
        out_shape=jax.ShapeDtypeStruct((nb * out_rows, WIDTH), F32),
        scratch_shapes=[
            pltpu.VMEM((2, SEQ_ALL, WIDTH), F32),
            pltpu.VMEM((2, SEQ_ALL, WIDTH), F32),
            pltpu.VMEM((2, SEQ_ALL, WIDTH), F32),
            pltpu.VMEM((2, WIDTH, WIDTH), F32),
            pltpu.VMEM((2, WIDTH, WIDTH), BF16),
        ],
        compiler_params=_params("arbitrary"),
        name="hgrn2",
    )(z, z, z, z, z, lb, gn)


def _route(logits):
    rows = [logits[e:e + 1, :] for e in range(N_EXPERTS)]
    m = functools.reduce(jnp.maximum, rows)
    ex = [jnp.exp(r - m) for r in rows]
    inv = 1.0 / functools.reduce(jnp.add, ex)
    sc = [e * inv for e in ex]
    g_score = []
    for g in range(N_GROUPS):
        grp = sc[g * EXP_PER_GROUP:(g + 1) * EXP_PER_GROUP]
        pairs = [grp[a] + grp[b] for a in range(EXP_PER_GROUP) for b in range(a + 1, EXP_PER_GROUP)]
        g_score.append(functools.reduce(jnp.maximum, pairs))
    gate2 = [jnp.zeros_like(sc[0]) for _ in range(2)]
    onehot = []
    for g in range(N_GROUPS):
        g_ok = None
        for o in range(N_GROUPS):
            if o == g:
                continue
            t = (g_score[g] > g_score[o]) if o < g else (g_score[g] >= g_score[o])
            g_ok = t if g_ok is None else (g_ok & t)
        grp = sc[g * EXP_PER_GROUP:(g + 1) * EXP_PER_GROUP]
        sel = []
        for a in range(EXP_PER_GROUP):
            beaten = jnp.zeros_like(grp[a])
            for o in range(EXP_PER_GROUP):
                if o == a:
                    continue
                t = (grp[o] >= grp[a]) if o < a else (grp[o] > grp[a])
                beaten = beaten + jnp.where(t, 1.0, 0.0)
            sel.append((beaten < 1.5) & g_ok)
        denom = functools.reduce(jnp.add, [jnp.where(ok, p, 0.0) for ok, p in zip(sel, grp)])
        denom = jnp.where(g_ok, denom, 1.0)
        for a, b in PAIRS:
            both = sel[a] & sel[b]
            onehot.append(jnp.where(both, 1.0, 0.0))
            gate2[0] = gate2[0] + jnp.where(both, grp[a] / denom, 0.0)
            gate2[1] = gate2[1] + jnp.where(both, grp[b] / denom, 0.0)
    return gate2, onehot


def _outproj_kernel(x_ref, ya_ref, yb_ref, ys_ref, yh_ref, w_ref, modl_ref, modc_ref, g_ref,
                    wr_ref, br_ref, xo_ref, hp_ref, pos_ref, cnt_ref, cnt_s, *, tm, has_ctx, cap):
    @pl.when(pl.program_id(0) == 0)
    def _():
        cnt_s[...] = jnp.zeros_like(cnt_s)

    pick = _mod_picker(modl_ref, modc_ref, tm, has_ctx)
    acc = jnp.zeros((tm, D_MODEL), F32)
    for kblk, y_ref in enumerate((ya_ref, yb_ref, ys_ref, yh_ref)):
        acc = acc + jnp.dot(y_ref[...].astype(BF16), w_ref[kblk * WIDTH:(kblk + 1) * WIDTH, :],
                            preferred_element_type=F32)
    x = x_ref[...] + pick(2) * acc
    xo_ref[...] = x
    h = _rms(x, g_ref[...]) * (1.0 + pick(4)) + pick(3)
    nt = (((1,), (1,)), ((), ()))
    wr = wr_ref[...]
    wr_hi = wr.astype(BF16)
    wr_lo = (wr - wr_hi.astype(F32)).astype(BF16)
    h_hi = h.astype(BF16)
    h_lo = (h - h_hi.astype(F32)).astype(BF16)
    logits = (lax.dot_general(wr_hi, h_hi, nt, preferred_element_type=F32)
              + lax.dot_general(wr_hi, h_lo, nt, preferred_element_type=F32)
              + lax.dot_general(wr_lo, h_hi, nt, preferred_element_type=F32)) + br_ref[...]
    gate2, onehot = _route(logits)
    hp_ref[:, 0:D_MODEL] = h
    gate_rows = jnp.concatenate(gate2 + [jnp.zeros((128 - 2, tm), F32)], axis=0)
    hp_ref[:, D_MODEL:] = gate_rows.T
    sel = jnp.concatenate(onehot + [jnp.zeros((BUCKET_ROWS - N_BUCKETS, tm), F32)], axis=0)
    before = (lax.broadcasted_iota(jnp.int32, (tm, tm), 0)
              <= lax.broadcasted_iota(jnp.int32, (tm, tm), 1)).astype(BF16)
    seen = jnp.dot(sel.astype(BF16), before, preferred_element_type=F32)
    carried = cnt_s[:, 0:1]
    base = lax.broadcasted_iota(jnp.int32, (BUCKET_ROWS, 1), 0).astype(F32) * float(cap)
    slot = jnp.sum(sel * (base + carried + seen - 1.0), axis=0, keepdims=True)
    pos_ref[0] = slot.astype(jnp.int32)
    cnt_s[...] = cnt_s[...] + jnp.sum(sel, axis=1, keepdims=True)
    cnt_ref[...] = cnt_s[...].astype(jnp.int32)


def _outproj(x, ys, w_bf16, mod, g2, wr_t, br, has_ctx):
    n = ys[0].shape[0]
    if has_ctx:
        tm = DENSE_TM
        tpb = SEQ_ALL // tm
        x_spec = pl.BlockSpec((tm, D_MODEL), lambda i: (i, 0))
    else:
        tm = CTX_LEN
        tpb = SEQ // tm
        x_spec = pl.BlockSpec((tm, D_MODEL),
                              lambda i: ((i // tpb) * (SEQ_ALL // tm) + 1 + i % tpb, 0))
    tile = lambda w: pl.BlockSpec((tm, w), lambda i: (i, 0))
    return pl.pallas_call(
        functools.partial(_outproj_kernel, tm=tm, has_ctx=has_ctx, cap=n),
        grid=(n // tm,),
        in_specs=[
            x_spec, tile(WIDTH), tile(WIDTH), tile(WIDTH), tile(WIDTH),
            pl.BlockSpec((D_MODEL, D_MODEL), lambda i: (0, 0)),
            *_mod_specs(tpb),
            pl.BlockSpec((1, D_MODEL), lambda i: (0, 0)),
            pl.BlockSpec((N_EXPERTS, D_MODEL), lambda i: (0, 0)),
            pl.BlockSpec((N_EXPERTS, 1), lambda i: (0, 0)),
        ],
        out_specs=[
            tile(D_MODEL), tile(HP_COLS),
            pl.BlockSpec((1, 1, tm), lambda i: (i, 0, 0)),
            pl.BlockSpec((BUCKET_ROWS, 128), lambda i: (0, 0)),
        ],
        out_shape=[
            jax.ShapeDtypeStruct((n, D_MODEL), F32),
            jax.ShapeDtypeStruct((n, HP_COLS), F32),
            jax.ShapeDtypeStruct((n // tm, 1, tm), jnp.int32),
            jax.ShapeDtypeStruct((BUCKET_ROWS, 128), jnp.int32),
        ],
        scratch_shapes=[pltpu.VMEM((BUCKET_ROWS, 128), F32)],
        compiler_params=_params("arbitrary"),
        name="outproj_router",
    )(x, *ys, w_bf16, mod, mod, g2.reshape(1, D_MODEL), wr_t, br)


def _row_copies_wait(dst_rows, sem):
    pltpu.make_async_copy(dst_rows, dst_rows, sem).wait()


def _invert_kernel(pos_ref, pad_ref, tok_ref):
    def clear(p, carry):
        tok_ref[p] = 0
        return carry

    def put(t, carry):
        tok_ref[pos_ref[t]] = t
        return carry

    for r in range(N_BUCKETS + 1):
        lax.fori_loop(pad_ref[r, 0], pad_ref[r, 1], clear, 0)
    lax.fori_loop(0, pos_ref.shape[0], put, 0, unroll=8)


def _invert(pos, pad, n_slots):
    return pl.pallas_call(
        _invert_kernel,
        in_specs=[pl.BlockSpec(memory_space=pltpu.SMEM), pl.BlockSpec(memory_space=pltpu.SMEM)],
        out_specs=pl.BlockSpec(memory_space=pltpu.SMEM),
        out_shape=jax.ShapeDtypeStruct((n_slots,), jnp.int32),
        name="moe_invert",
    )(pos, pad)


def _experts_kernel(ea_ref, eb_ref, valid_ref, tok_ref, hp_hbm, w1a_ref, w1b_ref, w3a_ref, w3b_ref,
                    w2a_ref, w2b_ref, ys_ref, buf, sems):
    k = pl.program_id(0)
    n_blk = pl.num_programs(0)
    n_valid = valid_ref[k]
    slot = k % 2

    def gather(blk, buf_slot):
        def body(i, carry):
            base = blk * MOE_R + i * 8
            for j in range(8):
                pltpu.make_async_copy(hp_hbm.at[pl.ds(tok_ref[base + j], 1)],
                                      buf.at[buf_slot, i, pl.ds(j, 1)], sems.at[buf_slot]).start()
            return carry

        lax.fori_loop(0, MOE_R // 8, body, 0)

    @pl.when(k == 0)
    def _():
        gather(0, 0)

    nxt = jnp.minimum(k + 1, n_blk - 1)

    @pl.when((k + 1 < n_blk) & (valid_ref[nxt] > 0))
    def _():
        gather(k + 1, 1 - slot)

    @pl.when(n_valid > 0)
    def _():
        _row_copies_wait(buf.at[slot], sems.at[slot])
        w = buf[slot].reshape(MOE_R, HP_COLS)
        ok = lax.broadcasted_iota(jnp.int32, (MOE_R, 1), 0) < n_valid
        h = jnp.where(ok, w[:, 0:D_MODEL], 0.0).astype(BF16)
        gates = jnp.where(ok, w[:, D_MODEL:], 0.0)
        acc = jnp.zeros((MOE_R, D_MODEL), F32)
        for e, (w1_ref, w3_ref, w2_ref) in enumerate(((w1a_ref, w3a_ref, w2a_ref),
                                                      (w1b_ref, w3b_ref, w2b_ref))):
            a = jnp.dot(h, w1_ref[...], preferred_element_type=F32)
            b = jnp.dot(h, w3_ref[...], preferred_element_type=F32)
            he = (_silu(a) * b * gates[:, e:e + 1]).astype(BF16)
            acc = acc + jnp.dot(he, w2_ref[...], preferred_element_type=F32)
        ys_ref[...] = acc

    @pl.when(n_valid == 0)
    def _():
        ys_ref[...] = jnp.zeros_like(ys_ref)


def _experts(hp, tok, blk_ea, blk_eb, blk_valid, w1, w3, w2, layer):
    n_slots = tok.shape[0]

    def weight(shape, which):
        return pl.BlockSpec((None, None) + shape,
                            lambda k, ea, eb, valid, tok: (layer, (ea, eb)[which][k], 0, 0))

    up, down = (D_MODEL, D_EXPERT), (D_EXPERT, D_MODEL)
    grid_spec = pltpu.PrefetchScalarGridSpec(
        num_scalar_prefetch=4,
        grid=(n_slots // MOE_R,),
        in_specs=[pl.BlockSpec(memory_space=pl.ANY), weight(up, 0), weight(up, 1),
                  weight(up, 0), weight(up, 1), weight(down, 0), weight(down, 1)],
        out_specs=pl.BlockSpec((MOE_R, D_MODEL), lambda k, ea, eb, valid, tok: (k, 0)),
        scratch_shapes=[pltpu.VMEM((2, MOE_R // 8, 8, HP_COLS), F32),
                        pltpu.SemaphoreType.DMA((2,))],
    )
    return pl.pallas_call(
        _experts_kernel,
        grid_spec=grid_spec,
        out_shape=jax.ShapeDtypeStruct((n_slots, D_MODEL), F32),
        compiler_params=_params("arbitrary"),
        name="moe_experts",
    )(blk_ea, blk_eb, blk_valid, tok, hp, w1, w1, w3, w3, w2, w2)


def _combine_kernel(pos_ref, x_ref, modl_ref, modc_ref, fg_ref, ys_hbm, o_ref, buf, sem,
                    *, tm, has_ctx, final_norm):
    def body(i, carry):
        for j in range(8):
            pltpu.make_async_copy(ys_hbm.at[pl.ds(pos_ref[0, 0, i * 8 + j], 1)],
                                  buf.at[i, pl.ds(j, 1)], sem).start()
        return carry

    lax.fori_loop(0, tm // 8, body, 0)
    _row_copies_wait(buf, sem)
    pick = _mod_picker(modl_ref, modc_ref, tm, has_ctx)
    x = x_ref[...] + pick(5) * buf[...].reshape(tm, D_MODEL)
    o_ref[...] = _rms(x, fg_ref[...]) if final_norm else x


def _combine(pos, x, ys, mod, final_g, has_ctx, final_norm):
    n = x.shape[0]
    tm = COMBINE_TM if has_ctx else COMBINE_TM_LATENT
    tpb = (SEQ_ALL if has_ctx else SEQ) // tm
    return pl.pallas_call(
        functools.partial(_combine_kernel, tm=tm, has_ctx=has_ctx, final_norm=final_norm),
        grid=(n // tm,),
        in_specs=[
            pl.BlockSpec((1, 1, tm), lambda i: (i, 0, 0), memory_space=pltpu.SMEM),
            pl.BlockSpec((tm, D_MODEL), lambda i: (i, 0)),
            *_mod_specs(tpb),
            pl.BlockSpec((1, D_MODEL), lambda i: (0, 0)),
            pl.BlockSpec(memory_space=pl.ANY),
        ],
        out_specs=pl.BlockSpec((tm, D_MODEL), lambda i: (i, 0)),
        out_shape=jax.ShapeDtypeStruct((n, D_MODEL), F32),
        scratch_shapes=[pltpu.VMEM((tm // 8, 8, D_MODEL), F32), pltpu.SemaphoreType.DMA],
        compiler_params=_params("arbitrary"),
        name="moe_combine",
    )(pos.reshape(n // tm, 1, tm), x, mod, mod, final_g.reshape(1, D_MODEL), ys)


def _routing_tables(slot, counts, n):
    n_blk = n // MOE_R + N_BUCKETS
    n_b = counts[:N_BUCKETS, 0]
    nb = (n_b + MOE_R - 1) // MOE_R
    ends = jnp.cumsum(nb)
    starts = ends - nb
    b_tok = slot // n
    pos = slot - b_tok * n + starts[b_tok] * MOE_R
    k = jnp.arange(n_blk, dtype=jnp.int32)
    b = jnp.minimum(jnp.sum((k[:, None] >= ends[None, :]).astype(jnp.int32), axis=1), N_BUCKETS - 1)
    valid = jnp.where(k < ends[-1], jnp.clip(n_b[b] - (k - starts[b]) * MOE_R, 0, MOE_R), 0)
    pair = jnp.array(PAIRS, jnp.int32)[b % len(PAIRS)]
    first = (b // len(PAIRS)) * EXP_PER_GROUP
    pad = jnp.stack([jnp.append(starts * MOE_R + n_b, ends[-1] * MOE_R),
                     jnp.append(ends * MOE_R, n_blk * MOE_R)], axis=1)
    return (pos.astype(jnp.int32), (first + pair[:, 0]).astype(jnp.int32),
            (first + pair[:, 1]).astype(jnp.int32), valid.astype(jnp.int32), pad.astype(jnp.int32))


def _moe(hp, slot, counts, x, w1, w3, w2, layer, mod, final_g, has_ctx, final_norm):
    n = x.shape[0]
    assert n % MOE_R == 0
    pos, blk_ea, blk_eb, blk_valid, pad = _routing_tables(slot.reshape(n), counts, n)
    tok = _invert(pos, pad, n + N_BUCKETS * MOE_R)
    ys = _experts(hp, tok, blk_ea, blk_eb, blk_valid, w1, w3, w2, layer)
    return _combine(pos, x, ys, mod, final_g, has_ctx, final_norm)


def _block_diag(w):
    eye = jnp.eye(HEADS, dtype=w.dtype)
    return jnp.einsum('hij,hg->higj', w, eye).reshape(WIDTH, WIDTH)


def _rope_tables():
    rows = SEQ // GRID_W
    row_ids = jnp.repeat(jnp.arange(rows, dtype=F32), GRID_W)
    col_ids = jnp.tile(jnp.arange(GRID_W, dtype=F32), rows)
    n_freq = DA_QD // 4
    freqs = ROPE_THETA ** (-jnp.arange(n_freq, dtype=F32) / n_freq)
    ang = jnp.stack([row_ids[:, None] * freqs, col_ids[:, None] * freqs], axis=1)
    cos = jnp.cos(ang)
    sin = jnp.sin(ang)
    cos_l = jnp.broadcast_to(cos[:, None, None, :, None, :], (SEQ, HEADS, 2, 2, 2, n_freq))
    sin_l = jnp.broadcast_to(sin[:, None, None, :, None, :], (SEQ, HEADS, 2, 2, 2, n_freq))
    sign = jnp.array([-1.0, 1.0], F32)[None, None, None, None, :, None]
    cos_l = cos_l.reshape(SEQ, WIDTH)
    sin_l = (sin_l * sign).reshape(SEQ, WIDTH)
    cos_all = jnp.concatenate([jnp.ones((CTX_LEN, WIDTH), F32), cos_l], axis=0)
    sin_all = jnp.concatenate([jnp.zeros((CTX_LEN, WIDTH), F32), sin_l], axis=0)
    return cos_all, sin_all


def kernel(x, c, ctx, c_ctx, w_ada, b_ada, norm1_g, norm2_g, w_in, w_out, lru_conv_w, lru_conv_b,
           lru_wr, lru_br, lru_wi, lru_bi, lru_lam, da_lam, da_subln_g, sg_norm_g, sg_w, sg_b,
           hg_lb, hg_norm_g, router_w, router_b, moe_w1, moe_w3, moe_w2, final_norm_g):
    nb = x.shape[0]
    assert nb <= CTX_ROW and x.shape[1:] == (SEQ, D_MODEL) and ctx.shape[1:] == (CTX_LEN, D_MODEL)
    xs = jnp.concatenate([ctx, x], axis=1).reshape(nb * SEQ_ALL, D_MODEL)

    cvec = jnp.zeros((ADA_ROWS, D_MODEL), F32).at[:nb].set(c).at[CTX_ROW].set(c_ctx)
    mods = _ada(cvec, w_ada, b_ada).reshape(DEPTH, ADA_ROWS, 6, D_MODEL)

    cos_all, sin_all = _rope_tables()
    lb_cum = jnp.cumsum(jax.nn.softmax(hg_lb.astype(F32), axis=1), axis=1)
    lb_all = lb_cum - lb_cum[:, :1]
    wr_t = router_w.T
    br = router_b.reshape(N_EXPERTS, 1)
    w1_bf16, w3_bf16, w2_bf16 = (w.astype(BF16) for w in (moe_w1, moe_w3, moe_w2))

    for l in range(DEPTH):
        last = l == DEPTH - 1
        out_rows = SEQ if last else SEQ_ALL
        z = _inproj(xs, norm1_g[l], mods[l], w_in[l].astype(BF16))

        ya = _lru(z, lru_conv_w[l], lru_conv_b[l].reshape(1, WIDTH),
                  jax.vmap(_block_diag)(lru_wr[l]).astype(BF16), lru_br[l],
                  jax.vmap(_block_diag)(lru_wi[l]).astype(BF16), lru_bi[l], lru_lam[l], out_rows)

        lam_init = 0.8 - 0.6 * math.exp(-0.3 * l)
        lf = da_lam[l].astype(F32)
        lam = jnp.exp(jnp.sum(lf[0] * lf[1])) - jnp.exp(jnp.sum(lf[2] * lf[3])) + lam_init
        yb = _attn(z, lam.reshape(1, 1), cos_all, sin_all,
                   jnp.tile(da_subln_g[l], HEADS).reshape(1, WIDTH),
                   with_ctx=not last, out_scale=1.0 - lam_init)

        w_cat = jnp.transpose(sg_w[l], (1, 0, 2)).reshape(SG_CHUNK, HEADS * SG_CHUNK).astype(BF16)
        bias2d = jnp.repeat(sg_b[l].T, HEAD_DIM, axis=1)
        ys = _sgu(z, sg_norm_g[l].reshape(1, WIDTH), w_cat, bias2d, out_rows)

        yh = _hgrn(z, lb_all[:, l], hg_norm_g[l].reshape(1, WIDTH), out_rows)

        xs, hp, pos, counts = _outproj(xs, (ya, yb, ys, yh), w_out[l].astype(BF16), mods[l],
                                       norm2_g[l], wr_t, br, has_ctx=not last)
        xs = _moe(hp, pos, counts, xs, w1_bf16, w3_bf16, w2_bf16, l, mods[l], final_norm_g,
                  has_ctx=not last, final_norm=last)

    return xs.reshape(nb, SEQ, D_MODEL)
```

```python
import functools
import math

import jax
import jax.numpy as jnp
from jax import lax
from jax.experimental import pallas as pl
from jax.experimental.pallas import tpu as pltpu

F32 = jnp.float32
BF16 = jnp.bfloat16

D_MODEL = 1024
SEQ = 2048
CTX_LEN = 256
SEQ_ALL = CTX_LEN + SEQ
DEPTH = 2
GRID_W = 64
EPS = 1e-6
LOG2_E = 1.4426950408889634
WIDTH = 256
HEADS = 4
HEAD_DIM = 64
CONV_W = 4
RG_C = 8.0
DA_QD = 32
ROPE_THETA = 10000.0
SG_CHUNK = 128
N_EXPERTS = 16
N_GROUPS = 4
EXP_PER_GROUP = 4
D_EXPERT = 512
IN_COLS = 3072
COL_A_X, COL_A_G = 0, 1
COL_B_Q, COL_B_K, COL_B_V = 2, 3, 4
COL_C_U, COL_C_V = 5, 6
COL_D_Q, COL_D_FF, COL_D_FB, COL_D_I, COL_D_G = 7, 8, 9, 10, 11

ADA_ROWS = 16
CTX_ROW = 8
VMEM_LIMIT = 56 * 1024 * 1024

LRU_BLK = 8
HG_CHUNK = 16
ATT_TQ = 256
DENSE_TM = 768
MOE_R = 512
HP_COLS = D_MODEL + 128
COMBINE_TM = 1152
COMBINE_TM_LATENT = 1024


def _params(*sem):
    return pltpu.CompilerParams(dimension_semantics=sem, vmem_limit_bytes=VMEM_LIMIT)


def _rms(xf, g):
    return xf * lax.rsqrt(jnp.mean(xf * xf, axis=-1, keepdims=True) + EPS) * g


def _sigmoid(x):
    return 1.0 / (1.0 + jnp.exp(-x))


def _silu(x):
    return x * _sigmoid(x)


def _gelu(x):
    return jax.nn.gelu(x)


def _split3(x):
    hi = x.astype(BF16)
    r = x - hi.astype(F32)
    mid = r.astype(BF16)
    lo = (r - mid.astype(F32)).astype(BF16)
    return hi, mid, lo


def _dot_f32_rhs(m_bf16, x):
    return functools.reduce(jnp.add, [jnp.dot(m_bf16, p, preferred_element_type=F32)
                                      for p in _split3(x)])


def _dot_f32_lhs(x, m_bf16):
    return functools.reduce(jnp.add, [jnp.dot(p, m_bf16, preferred_element_type=F32)
                                      for p in _split3(x)])


def _head_ones(dtype):
    r = lax.broadcasted_iota(jnp.int32, (WIDTH, WIDTH), 0) // HEAD_DIM
    c = lax.broadcasted_iota(jnp.int32, (WIDTH, WIDTH), 1) // HEAD_DIM
    return (r == c).astype(dtype)


def _mod_picker(modl_ref, modc_ref, tm, has_ctx):
    ml = modl_ref[0]
    if not has_ctx:
        return lambda r: ml[r:r + 1]
    mc = modc_ref[0]
    row0 = (pl.program_id(0) % (SEQ_ALL // tm)) * tm
    is_ctx = row0 + lax.broadcasted_iota(jnp.int32, (tm, 1), 0) < CTX_LEN
    return lambda r: jnp.where(is_ctx, mc[r:r + 1], ml[r:r + 1])


def _ada_kernel(c_ref, w_ref, b_ref, o_ref):
    s = _silu(c_ref[...]).astype(BF16)
    o_ref[...] = jnp.dot(s, w_ref[...].astype(BF16), preferred_element_type=F32) + b_ref[...]


def _ada(cvec, w_ada, b_ada):
    tn = 1536
    return pl.pallas_call(
        _ada_kernel,
        grid=(DEPTH, 6 * D_MODEL // tn),
        in_specs=[
            pl.BlockSpec((ADA_ROWS, D_MODEL), lambda l, j: (0, 0)),
            pl.BlockSpec((None, D_MODEL, tn), lambda l, j: (l, 0, j)),
            pl.BlockSpec((None, 1, tn), lambda l, j: (l, 0, j)),
        ],
        out_specs=pl.BlockSpec((None, ADA_ROWS, tn), lambda l, j: (l, 0, j)),
        out_shape=jax.ShapeDtypeStruct((DEPTH, ADA_ROWS, 6 * D_MODEL), F32),
        compiler_params=_params("arbitrary", "arbitrary"),
        name="ada",
    )(cvec, w_ada, b_ada.reshape(DEPTH, 1, 6 * D_MODEL))


def _inproj_kernel(x_ref, g_ref, modl_ref, modc_ref, w_ref, z_ref, *, tm):
    pick = _mod_picker(modl_ref, modc_ref, tm, True)
    h = _rms(x_ref[...], g_ref[...]) * (1.0 + pick(1)) + pick(0)
    z_ref[...] = jnp.dot(h.astype(BF16), w_ref[...], preferred_element_type=F32)


def _mod_specs(tiles_per_batch):
    return [
        pl.BlockSpec((1, 6, D_MODEL), lambda i, *_: (i // tiles_per_batch, 0, 0)),
        pl.BlockSpec((1, 6, D_MODEL), lambda i, *_: (CTX_ROW, 0, 0)),
    ]


def _inproj(x, g, mod, w_bf16):
    n = x.shape[0]
    tm = DENSE_TM
    return pl.pallas_call(
        functools.partial(_inproj_kernel, tm=tm),
        grid=(n // tm,),
        in_specs=[
            pl.BlockSpec((tm, D_MODEL), lambda i: (i, 0)),
            pl.BlockSpec((1, D_MODEL), lambda i: (0, 0)),
            *_mod_specs(SEQ_ALL // tm),
            pl.BlockSpec((D_MODEL, IN_COLS), lambda i: (0, 0)),
        ],
        out_specs=pl.BlockSpec((tm, IN_COLS), lambda i: (i, 0)),
        out_shape=jax.ShapeDtypeStruct((n, IN_COLS), F32),
        compiler_params=_params("arbitrary"),
        name="inproj",
    )(x, g.reshape(1, D_MODEL), mod, mod, w_bf16)


def _lru_kernel(x_ref, gate_ref, cw_ref, cb_ref, wr_ref, br_ref, wi_ref, bi_ref, lam_ref,
                y_ref, a_s, b_s, h_s, *, out_rows):
    x = x_ref[...]
    rows = lax.broadcasted_iota(jnp.int32, (SEQ_ALL, 1), 0)
    seg = rows < CTX_LEN
    u = jnp.zeros_like(x)
    for j in range(CONV_W):
        off = j - CONV_W // 2
        xs = x if off == 0 else pltpu.roll(x, (-off) % SEQ_ALL, 0)
        src = rows + off
        ok = (src >= 0) & (src < SEQ_ALL) & ((src < CTX_LEN) == seg)
        u = u + jnp.where(ok, xs, 0.0) * cw_ref[j:j + 1, :]
    u = u + cb_ref[...]
    ub = u.astype(BF16)
    for d in range(2):
        r = _sigmoid(jnp.dot(ub, wr_ref[d], preferred_element_type=F32) + br_ref[d:d + 1, :])
        i = _sigmoid(jnp.dot(ub, wi_ref[d], preferred_element_type=F32) + bi_ref[d:d + 1, :])
        nl = -lam_ref[d:d + 1, :]
        softplus = jnp.maximum(nl, 0.0) + jnp.log(1.0 + jnp.exp(-jnp.abs(nl)))
        log_a = -RG_C * r * softplus
        a = jnp.exp(log_a)
        a_s[d] = a
        b_s[d] = jnp.sqrt(1.0 - a * a) * i * u

    n_blk = SEQ_ALL // LRU_BLK
    n_ctx_blk = CTX_LEN // LRU_BLK
    sub = lax.broadcasted_iota(jnp.int32, (LRU_BLK, 1), 0)

    def block_scan(a, b, reverse):
        s = 1
        while s < LRU_BLK:
            if reverse:
                a_sh = pltpu.roll(a, LRU_BLK - s, 0)
                b_sh = pltpu.roll(b, LRU_BLK - s, 0)
                ok = sub < LRU_BLK - s
            else:
                a_sh = pltpu.roll(a, s, 0)
                b_sh = pltpu.roll(b, s, 0)
                ok = sub >= s
            b = jnp.where(ok, a * b_sh + b, b)
            a = jnp.where(ok, a * a_sh, a)
            s *= 2
        return a, b

    def body(n, carry):
        hf, hb = carry
        rf = pl.multiple_of(n * LRU_BLK, LRU_BLK)
        af, bf = block_scan(a_s[0, pl.ds(rf, LRU_BLK), :], b_s[0, pl.ds(rf, LRU_BLK), :], False)
        h = af * hf + bf
        h_s[0, pl.ds(rf, LRU_BLK), :] = h
        hf = h[LRU_BLK - 1:LRU_BLK, :]
        nb = jnp.where(n < n_ctx_blk, n_ctx_blk - 1 - n, n_blk + n_ctx_blk - 1 - n)
        rb = pl.multiple_of(nb * LRU_BLK, LRU_BLK)
        ab, bb = block_scan(a_s[1, pl.ds(rb, LRU_BLK), :], b_s[1, pl.ds(rb, LRU_BLK), :], True)
        h = ab * hb + bb
        h_s[1, pl.ds(rb, LRU_BLK), :] = h
        hb = h[0:1, :]
        return hf, hb

    zero = jnp.zeros((1, WIDTH), F32)
    lax.fori_loop(0, n_blk, body, (zero, zero))
    first = SEQ_ALL - out_rows
    y_ref[...] = (h_s[0, first:, :] + h_s[1, first:, :]) * _gelu(gate_ref[first:, :])


def _lru(z, cw, cb, wr_bd, br, wi_bd, bi, lam, out_rows):
    nb = z.shape[0] // SEQ_ALL
    full = lambda shape: pl.BlockSpec(shape, lambda b: (0,) * len(shape))
    return pl.pallas_call(
        functools.partial(_lru_kernel, out_rows=out_rows),
        grid=(nb,),
        in_specs=[
            pl.BlockSpec((SEQ_ALL, WIDTH), lambda b: (b, COL_A_X)),
            pl.BlockSpec((SEQ_ALL, WIDTH), lambda b: (b, COL_A_G)),
            full((CONV_W, WIDTH)), full((1, WIDTH)),
            full((2, WIDTH, WIDTH)), full((2, WIDTH)),
            full((2, WIDTH, WIDTH)), full((2, WIDTH)),
            full((2, WIDTH)),
        ],
        out_specs=pl.BlockSpec((out_rows, WIDTH), lambda b: (b, 0)),
        out_shape=jax.ShapeDtypeStruct((nb * out_rows, WIDTH), F32),
        scratch_shapes=[
            pltpu.VMEM((2, SEQ_ALL, WIDTH), F32),
            pltpu.VMEM((2, SEQ_ALL, WIDTH), F32),
            pltpu.VMEM((2, SEQ_ALL, WIDTH), F32),
        ],
        compiler_params=_params("arbitrary"),
        name="rglru",
    )(z, z, cw, cb, wr_bd, br, wi_bd, bi, lam)


def _rope(x, cos, sin_signed):
    lane = lax.broadcasted_iota(jnp.int32, (1, WIDTH), 1)
    first_half = (lane % 16) < 8
    partner = jnp.where(first_half, pltpu.roll(x, WIDTH - 8, 1), pltpu.roll(x, 8, 1))
    return x * cos + partner * sin_signed


def _attn_kernel(lam_ref, q_ref, k_ref, v_ref, cos_ref, sin_ref, g_ref, y_ref, kt_s, v_s,
                 *, with_ctx, out_scale):
    j = pl.program_id(1)

    @pl.when(j == 0)
    def _():
        kr = _rope(k_ref[...], cos_ref[...], sin_ref[...])
        kt_s[...] = kr.T.astype(BF16)
        v = v_ref[...]
        ones = jnp.ones((SEQ_ALL, HEAD_DIM), F32)
        for h in range(HEADS):
            v_s[h] = jnp.concatenate([v[:, h * HEAD_DIM:(h + 1) * HEAD_DIM], ones],
                                     axis=1).astype(BF16)

    lam = lam_ref[0, 0]
    tile = j if with_ctx else j + 1
    row0 = pl.multiple_of(tile * ATT_TQ, ATT_TQ)
    q = _rope(q_ref[...], cos_ref[pl.ds(row0, ATT_TQ), :], sin_ref[pl.ds(row0, ATT_TQ), :])
    q = q * (DA_QD ** -0.5)

    def attend(n_keys):
        outs = []
        for h in range(HEADS):
            scores = []
            for n in range(2):
                c0 = h * HEAD_DIM + n * DA_QD
                scores.append(jnp.dot(q[:, c0:c0 + DA_QD].astype(BF16),
                                      kt_s[c0:c0 + DA_QD, 0:n_keys], preferred_element_type=F32))
            probs = [jnp.exp(s - jnp.max(s, axis=-1, keepdims=True)).astype(BF16) for s in scores]
            branch = []
            for p in probs:
                ov = jnp.dot(p, v_s[h, 0:n_keys, :], preferred_element_type=F32)
                branch.append(ov[:, 0:HEAD_DIM] * (1.0 / ov[:, HEAD_DIM:HEAD_DIM + 1]))
            o = branch[0] - lam * branch[1]
            o = o * lax.rsqrt(jnp.mean(o * o, axis=-1, keepdims=True) + EPS)
            outs.append(o)
        y_ref[...] = jnp.concatenate(outs, axis=-1) * g_ref[...] * out_scale

    if with_ctx:
        @pl.when(j == 0)
        def _():
            attend(CTX_LEN)

        @pl.when(j > 0)
        def _():
            attend(SEQ_ALL)
    else:
        attend(SEQ_ALL)


def _attn(z, lam, cos, sin_signed, sub_g4, with_ctx, out_scale):
    nb = z.shape[0] // SEQ_ALL
    tiles = SEQ_ALL // ATT_TQ
    nq = tiles if with_ctx else tiles - 1
    first = 0 if with_ctx else 1
    return pl.pallas_call(
        functools.partial(_attn_kernel, with_ctx=with_ctx, out_scale=out_scale),
        grid=(nb, nq),
        in_specs=[
            pl.BlockSpec(memory_space=pltpu.SMEM),
            pl.BlockSpec((ATT_TQ, WIDTH), lambda b, j: (b * tiles + j + first, COL_B_Q)),
            pl.BlockSpec((SEQ_ALL, WIDTH), lambda b, j: (b, COL_B_K)),
            pl.BlockSpec((SEQ_ALL, WIDTH), lambda b, j: (b, COL_B_V)),
            pl.BlockSpec((SEQ_ALL, WIDTH), lambda b, j: (0, 0)),
            pl.BlockSpec((SEQ_ALL, WIDTH), lambda b, j: (0, 0)),
            pl.BlockSpec((1, WIDTH), lambda b, j: (0, 0)),
        ],
        out_specs=pl.BlockSpec((ATT_TQ, WIDTH), lambda b, j: (b * nq + j, 0)),
        out_shape=jax.ShapeDtypeStruct((nb * nq * ATT_TQ, WIDTH), F32),
        scratch_shapes=[
            pltpu.VMEM((WIDTH, SEQ_ALL), BF16),
            pltpu.VMEM((HEADS, SEQ_ALL, 2 * HEAD_DIM), BF16),
        ],
        compiler_params=_params("arbitrary", "arbitrary"),
        name="diffattn",
    )(lam, z, z, z, cos, sin_signed, sub_g4)


def _sgu_kernel(u_ref, v_ref, g_ref, w_ref, b_ref, y_ref, *, out_rows):
    row_head = lax.broadcasted_iota(jnp.int32, (HEADS * SG_CHUNK, WIDTH), 0) // SG_CHUNK
    col_head = lax.broadcasted_iota(jnp.int32, (HEADS * SG_CHUNK, WIDTH), 1) // HEAD_DIM
    head_mask = row_head == col_head
    w = w_ref[...]
    bias = b_ref[...]
    g = g_ref[...]
    first = SEQ_ALL - out_rows
    for n in range(first // SG_CHUNK, SEQ_ALL // SG_CHUNK):
        rows = slice(n * SG_CHUNK, (n + 1) * SG_CHUNK)
        vn = _rms(_gelu(v_ref[rows, :]), g).astype(BF16)
        stacked = jnp.where(head_mask, jnp.concatenate([vn] * HEADS, axis=0), jnp.zeros((), BF16))
        vm = jnp.dot(w, stacked, preferred_element_type=F32) + bias
        y_ref[n * SG_CHUNK - first:(n + 1) * SG_CHUNK - first, :] = _gelu(u_ref[rows, :]) * vm


def _sgu(z, norm_g, w_cat, bias2d, out_rows):
    nb = z.shape[0] // SEQ_ALL
    return pl.pallas_call(
        functools.partial(_sgu_kernel, out_rows=out_rows),
        grid=(nb,),
        in_specs=[
            pl.BlockSpec((SEQ_ALL, WIDTH), lambda b: (b, COL_C_U)),
            pl.BlockSpec((SEQ_ALL, WIDTH), lambda b: (b, COL_C_V)),
            pl.BlockSpec((1, WIDTH), lambda b: (0, 0)),
            pl.BlockSpec((SG_CHUNK, HEADS * SG_CHUNK), lambda b: (0, 0)),
            pl.BlockSpec((SG_CHUNK, WIDTH), lambda b: (0, 0)),
        ],
        out_specs=pl.BlockSpec((out_rows, WIDTH), lambda b: (b, 0)),
        out_shape=jax.ShapeDtypeStruct((nb * out_rows, WIDTH), F32),
        compiler_params=_params("arbitrary"),
        name="sgu",
    )(z, z, norm_g, w_cat, bias2d)


def _hgrn_kernel(q_ref, ff_ref, fb_ref, i_ref, gate_ref, lb_ref, gn_ref, y_ref,
                 b_s, k_s, o_s, st_s, stb_s, *, out_rows):
    c = HG_CHUNK
    pre = 128
    ones_bf = _head_ones(BF16)
    pr = lax.broadcasted_iota(jnp.int32, (pre, pre), 0)
    pc = lax.broadcasted_iota(jnp.int32, (pre, pre), 1)
    same = (pr // c) == (pc // c)
    tri = ((same & (pc <= pr)).astype(BF16), (same & (pc >= pr)).astype(BF16))

    for d, f_ref in enumerate((ff_ref, fb_ref)):
        lb = lb_ref[d:d + 1, :]
        for n in range(SEQ_ALL // pre):
            rows = slice(n * pre, (n + 1) * pre)
            f = lb + (1.0 - lb) * _sigmoid(f_ref[rows, :])
            k_s[d, rows, :] = 1.0 - f
            b_s[d, rows, :] = _dot_f32_rhs(tri[d], jnp.log(f) * LOG2_E)
    st_s[...] = jnp.zeros_like(st_s)
    stb_s[...] = jnp.zeros_like(stb_s)

    n_chunks = SEQ_ALL // c
    n_ctx = CTX_LEN // c
    half = c // 2
    sub = lax.broadcasted_iota(jnp.int32, (half, 1), 0)
    lane_head = lax.broadcasted_iota(jnp.int32, (1, 128), 1) // HEAD_DIM
    scale = HEAD_DIM ** -0.5

    def masked_out(d, s, tile):
        return (d == 0 and (tile + 1) * half <= s) or (d == 1 and tile * half > s)

    def front(d, chunk, need_out):
        r0 = pl.multiple_of(chunk * c, c)
        q = q_ref[pl.ds(r0, c), :] * scale
        v = i_ref[pl.ds(r0, c), :]
        b = b_s[d, pl.ds(r0, c), :]
        k = k_s[d, pl.ds(r0, c), :]
        tot = b[c - 1:c, :] if d == 0 else b[0:1, :]
        ke = (k * jnp.exp2(tot - b)).astype(BF16)
        upd = lax.dot_general(v.astype(BF16), ke, (((0,), (0,)), ((), ())),
                              preferred_element_type=F32)
        if not need_out:
            return r0, None, None, upd, v, jnp.exp2(tot)
        o = lax.dot_general((q * jnp.exp2(b)).astype(BF16), stb_s[d],
                            (((1,), (1,)), ((), ())), preferred_element_type=F32)
        parts = []
        for s in range(c):
            for tile in range(2):
                lo_row, hi_row = tile * half, (tile + 1) * half
                if masked_out(d, s, tile):
                    parts.append(jnp.zeros((half, WIDTH), F32))
                    continue
                diff = b[lo_row:hi_row, :] - b[s:s + 1, :]
                if d == 0 and lo_row < s:
                    diff = jnp.where(sub + lo_row >= s, diff, -jnp.inf)
                elif d == 1 and hi_row - 1 > s:
                    diff = jnp.where(sub + lo_row <= s, diff, -jnp.inf)
                parts.append(q[lo_row:hi_row, :] * k[s:s + 1, :] * jnp.exp2(diff))
        att = jnp.dot(jnp.concatenate(parts, axis=0).astype(BF16), ones_bf,
                      preferred_element_type=F32)
        return r0, o, att, upd, v, jnp.exp2(tot)

    def back(d, r0, o, att, upd, v, decay):
        if o is not None:
            tiles = [o[0:half, :], o[half:c, :]]
            for s in range(c):
                for tile in range(2):
                    if not masked_out(d, s, tile):
                        r = s * c + tile * half
                        tiles[tile] = tiles[tile] + att[r:r + half, :] * v[s:s + 1, :]
            o_s[d, pl.ds(r0, c), :] = jnp.concatenate(tiles, axis=0)
        for h in range(HEADS):
            r = slice(h * HEAD_DIM, (h + 1) * HEAD_DIM)
            ct = slice((h // 2) * 128, (h // 2 + 1) * 128)
            new = st_s[d, r, ct] * decay[:, ct] + jnp.where(lane_head == h % 2, upd[r, ct], 0.0)
            st_s[d, r, ct] = new
            stb_s[d, r, ct] = new.astype(BF16)

    def run(lo, hi, need_out):
        def body(n, carry):
            fwd = front(0, n, need_out)
            bwd = front(1, jnp.where(n < n_ctx, n_ctx - 1 - n, n_chunks + n_ctx - 1 - n), need_out)
            back(0, *fwd)
            back(1, *bwd)
            return carry

        lax.fori_loop(lo, hi, body, 0, unroll=4)

    if out_rows == SEQ_ALL:
        run(0, n_chunks, True)
    else:
        run(0, n_ctx, False)
        run(n_ctx, n_chunks, True)

    first = SEQ_ALL - out_rows
    o = o_s[0, first:, :] + o_s[1, first:, :]
    ms = _dot_f32_lhs(o * o, ones_bf) * (1.0 / HEAD_DIM)
    y_ref[...] = o * lax.rsqrt(ms + EPS) * gn_ref[...] * _silu(gate_ref[first:, :])


def _hgrn(z, lb, gn, out_rows):
    nb = z.shape[0] // SEQ_ALL
    col = lambda cidx: pl.BlockSpec((SEQ_ALL, WIDTH), lambda b: (b, cidx))
    return pl.pallas_call(
        functools.partial(_hgrn_kernel, out_rows=out_rows),
        grid=(nb,),
        in_specs=[
            col(COL_D_Q), col(COL_D_FF), col(COL_D_FB), col(COL_D_I), col(COL_D_G),
            pl.BlockSpec((2, WIDTH), lambda b: (0, 0)),
            pl.BlockSpec((1, WIDTH), lambda b: (0, 0)),
        ],
        out_specs=pl.BlockSpec((out_rows, WIDTH), lambda b: (b, 0)),
        out_shape=jax.ShapeDtypeStruct((nb * out_rows, WIDTH), F32),
        scratch_shapes=[
            pltpu.VMEM((2, SEQ_ALL, WIDTH), F32),
            pltpu.VMEM((2, SEQ_ALL, WIDTH), F32),
            pltpu.VMEM((2, SEQ_ALL, WIDTH), F32),
            pltpu.VMEM((2, WIDTH, WIDTH), F32),
            pltpu.VMEM((2, WIDTH, WIDTH), BF16),
        ],
        compiler_params=_params("arbitrary"),
        name="hgrn2",
    )(z, z, z, z, z, lb, gn)


def _route(logits):
    rows = [logits[e:e + 1, :] for e in range(N_EXPERTS)]
    m = functools.reduce(jnp.maximum, rows)
    ex = [jnp.exp(r - m) for r in rows]
    inv = 1.0 / functools.reduce(jnp.add, ex)
    sc = [e * inv for e in ex]
    g_score = []
    for g in range(N_GROUPS):
        grp = sc[g * EXP_PER_GROUP:(g + 1) * EXP_PER_GROUP]
        pairs = [grp[a] + grp[b] for a in range(EXP_PER_GROUP) for b in range(a + 1, EXP_PER_GROUP)]
        g_score.append(functools.reduce(jnp.maximum, pairs))
    gate4 = [jnp.zeros_like(sc[0]) for _ in range(EXP_PER_GROUP)]
    onehot = []
    for g in range(N_GROUPS):
        g_ok = None
        for o in range(N_GROUPS):
            if o == g:
                continue
            t = (g_score[g] > g_score[o]) if o < g else (g_score[g] >= g_score[o])
            g_ok = t if g_ok is None else (g_ok & t)
        grp = sc[g * EXP_PER_GROUP:(g + 1) * EXP_PER_GROUP]
        picked = []
        for a in range(EXP_PER_GROUP):
            beaten = jnp.zeros_like(grp[a])
            for o in range(EXP_PER_GROUP):
                if o == a:
                    continue
                t = (grp[o] >= grp[a]) if o < a else (grp[o] > grp[a])
                beaten = beaten + jnp.where(t, 1.0, 0.0)
            picked.append(jnp.where((beaten < 1.5) & g_ok, grp[a], 0.0))
        denom = functools.reduce(jnp.add, picked)
        denom = jnp.where(g_ok, denom, 1.0)
        gate4 = [acc + p / denom for acc, p in zip(gate4, picked)]
        onehot.append(jnp.where(g_ok, 1.0, 0.0))
    return gate4, onehot


def _outproj_kernel(x_ref, ya_ref, yb_ref, ys_ref, yh_ref, w_ref, modl_ref, modc_ref, g_ref,
                    wr_ref, br_ref, xo_ref, hp_ref, pos_ref, cnt_ref, cnt_s, *, tm, has_ctx, cap):
    @pl.when(pl.program_id(0) == 0)
    def _():
        cnt_s[...] = jnp.zeros_like(cnt_s)

    pick = _mod_picker(modl_ref, modc_ref, tm, has_ctx)
    acc = jnp.zeros((tm, D_MODEL), F32)
    for kblk, y_ref in enumerate((ya_ref, yb_ref, ys_ref, yh_ref)):
        acc = acc + jnp.dot(y_ref[...].astype(BF16), w_ref[kblk * WIDTH:(kblk + 1) * WIDTH, :],
                            preferred_element_type=F32)
    x = x_ref[...] + pick(2) * acc
    xo_ref[...] = x
    h = _rms(x, g_ref[...]) * (1.0 + pick(4)) + pick(3)
    nt = (((1,), (1,)), ((), ()))
    wr = wr_ref[...]
    wr_hi = wr.astype(BF16)
    wr_lo = (wr - wr_hi.astype(F32)).astype(BF16)
    h_hi = h.astype(BF16)
    h_lo = (h - h_hi.astype(F32)).astype(BF16)
    logits = (lax.dot_general(wr_hi, h_hi, nt, preferred_element_type=F32)
              + lax.dot_general(wr_hi, h_lo, nt, preferred_element_type=F32)
              + lax.dot_general(wr_lo, h_hi, nt, preferred_element_type=F32)) + br_ref[...]
    gate4, onehot = _route(logits)
    hp_ref[:, 0:D_MODEL] = h
    gate_rows = jnp.concatenate(gate4 + [jnp.zeros((128 - EXP_PER_GROUP, tm), F32)], axis=0)
    hp_ref[:, D_MODEL:] = gate_rows.T
    sel = jnp.concatenate(onehot + [jnp.zeros((8 - N_GROUPS, tm), F32)], axis=0)
    before = (lax.broadcasted_iota(jnp.int32, (tm, tm), 0)
              <= lax.broadcasted_iota(jnp.int32, (tm, tm), 1)).astype(BF16)
    seen = jnp.dot(sel.astype(BF16), before, preferred_element_type=F32)
    carried = cnt_s[:, 0:1]
    base = lax.broadcasted_iota(jnp.int32, (8, 1), 0).astype(F32) * float(cap)
    slot = jnp.sum(sel * (base + carried + seen - 1.0), axis=0, keepdims=True)
    pos_ref[0] = slot.astype(jnp.int32)
    cnt_s[...] = cnt_s[...] + jnp.sum(sel, axis=1, keepdims=True)
    cnt_ref[...] = cnt_s[...].astype(jnp.int32)


def _outproj(x, ys, w_bf16, mod, g2, wr_t, br, has_ctx):
    n = ys[0].shape[0]
    if has_ctx:
        tm = DENSE_TM
        tpb = SEQ_ALL // tm
        x_spec = pl.BlockSpec((tm, D_MODEL), lambda i: (i, 0))
    else:
        tm = CTX_LEN
        tpb = SEQ // tm
        x_spec = pl.BlockSpec((tm, D_MODEL),
                              lambda i: ((i // tpb) * (SEQ_ALL // tm) + 1 + i % tpb, 0))
    tile = lambda w: pl.BlockSpec((tm, w), lambda i: (i, 0))
    return pl.pallas_call(
        functools.partial(_outproj_kernel, tm=tm, has_ctx=has_ctx, cap=n),
        grid=(n // tm,),
        in_specs=[
            x_spec, tile(WIDTH), tile(WIDTH), tile(WIDTH), tile(WIDTH),
            pl.BlockSpec((D_MODEL, D_MODEL), lambda i: (0, 0)),
            *_mod_specs(tpb),
            pl.BlockSpec((1, D_MODEL), lambda i: (0, 0)),
            pl.BlockSpec((N_EXPERTS, D_MODEL), lambda i: (0, 0)),
            pl.BlockSpec((N_EXPERTS, 1), lambda i: (0, 0)),
        ],
        out_specs=[
            tile(D_MODEL), tile(HP_COLS),
            pl.BlockSpec((1, 1, tm), lambda i: (i, 0, 0)),
            pl.BlockSpec((8, 128), lambda i: (0, 0)),
        ],
        out_shape=[
            jax.ShapeDtypeStruct((n, D_MODEL), F32),
            jax.ShapeDtypeStruct((n, HP_COLS), F32),
            jax.ShapeDtypeStruct((n // tm, 1, tm), jnp.int32),
            jax.ShapeDtypeStruct((8, 128), jnp.int32),
        ],
        scratch_shapes=[pltpu.VMEM((8, 128), F32)],
        compiler_params=_params("arbitrary"),
        name="outproj_router",
    )(x, *ys, w_bf16, mod, mod, g2.reshape(1, D_MODEL), wr_t, br)


def _row_copies_wait(dst_rows, sem):
    pltpu.make_async_copy(dst_rows, dst_rows, sem).wait()


def _invert_kernel(pos_ref, pad_ref, tok_ref):
    def clear(p, carry):
        tok_ref[p] = 0
        return carry

    def put(t, carry):
        tok_ref[pos_ref[t]] = t
        return carry

    for r in range(N_GROUPS + 1):
        lax.fori_loop(pad_ref[r, 0], pad_ref[r, 1], clear, 0)
    lax.fori_loop(0, pos_ref.shape[0], put, 0, unroll=8)


def _invert(pos, pad, n_slots):
    return pl.pallas_call(
        _invert_kernel,
        in_specs=[pl.BlockSpec(memory_space=pltpu.SMEM), pl.BlockSpec(memory_space=pltpu.SMEM)],
        out_specs=pl.BlockSpec(memory_space=pltpu.SMEM),
        out_shape=jax.ShapeDtypeStruct((n_slots,), jnp.int32),
        name="moe_invert",
    )(pos, pad)


def _experts_kernel(grp_ref, valid_ref, tok_ref, hp_hbm, w1_ref, w3_ref, w2_ref, ys_ref, buf, sems):
    k = pl.program_id(0)
    n_blk = pl.num_programs(0)
    n_valid = valid_ref[k]
    slot = k % 2

    def gather(blk, buf_slot):
        def body(i, carry):
            base = blk * MOE_R + i * 8
            for j in range(8):
                pltpu.make_async_copy(hp_hbm.at[pl.ds(tok_ref[base + j], 1)],
                                      buf.at[buf_slot, i, pl.ds(j, 1)], sems.at[buf_slot]).start()
            return carry

        lax.fori_loop(0, MOE_R // 8, body, 0)

    @pl.when(k == 0)
    def _():
        gather(0, 0)

    nxt = jnp.minimum(k + 1, n_blk - 1)

    @pl.when((k + 1 < n_blk) & (valid_ref[nxt] > 0))
    def _():
        gather(k + 1, 1 - slot)

    @pl.when(n_valid > 0)
    def _():
        _row_copies_wait(buf.at[slot], sems.at[slot])
        w = buf[slot].reshape(MOE_R, HP_COLS)
        ok = lax.broadcasted_iota(jnp.int32, (MOE_R, 1), 0) < n_valid
        h = jnp.where(ok, w[:, 0:D_MODEL], 0.0).astype(BF16)
        gates = jnp.where(ok, w[:, D_MODEL:], 0.0)
        acc = jnp.zeros((MOE_R, D_MODEL), F32)
        for e in range(EXP_PER_GROUP):
            a = jnp.dot(h, w1_ref[e], preferred_element_type=F32)
            b = jnp.dot(h, w3_ref[e], preferred_element_type=F32)
            he = (_silu(a) * b * gates[:, e:e + 1]).astype(BF16)
            acc = acc + jnp.dot(he, w2_ref[e], preferred_element_type=F32)
        ys_ref[...] = acc

    @pl.when(n_valid == 0)
    def _():
        ys_ref[...] = jnp.zeros_like(ys_ref)


def _experts(hp, tok, blk_grp, blk_valid, w1, w3, w2, layer):
    n_slots = tok.shape[0]
    up = pl.BlockSpec((None, EXP_PER_GROUP, D_MODEL, D_EXPERT),
                      lambda k, grp, valid, tok: (layer, grp[k], 0, 0))
    down = pl.BlockSpec((None, EXP_PER_GROUP, D_EXPERT, D_MODEL),
                        lambda k, grp, valid, tok: (layer, grp[k], 0, 0))
    grid_spec = pltpu.PrefetchScalarGridSpec(
        num_scalar_prefetch=3,
        grid=(n_slots // MOE_R,),
        in_specs=[pl.BlockSpec(memory_space=pl.ANY), up, up, down],
        out_specs=pl.BlockSpec((MOE_R, D_MODEL), lambda k, grp, valid, tok: (k, 0)),
        scratch_shapes=[pltpu.VMEM((2, MOE_R // 8, 8, HP_COLS), F32),
                        pltpu.SemaphoreType.DMA((2,))],
    )
    return pl.pallas_call(
        _experts_kernel,
        grid_spec=grid_spec,
        out_shape=jax.ShapeDtypeStruct((n_slots, D_MODEL), F32),
        compiler_params=_params("arbitrary"),
        name="moe_experts",
    )(blk_grp, blk_valid, tok, hp, w1, w3, w2)


def _combine_kernel(pos_ref, x_ref, modl_ref, modc_ref, fg_ref, ys_hbm, o_ref, buf, sem,
                    *, tm, has_ctx, final_norm):
    def body(i, carry):
        for j in range(8):
            pltpu.make_async_copy(ys_hbm.at[pl.ds(pos_ref[0, 0, i * 8 + j], 1)],
                                  buf.at[i, pl.ds(j, 1)], sem).start()
        return carry

    lax.fori_loop(0, tm // 8, body, 0)
    _row_copies_wait(buf, sem)
    pick = _mod_picker(modl_ref, modc_ref, tm, has_ctx)
    x = x_ref[...] + pick(5) * buf[...].reshape(tm, D_MODEL)
    o_ref[...] = _rms(x, fg_ref[...]) if final_norm else x


def _combine(pos, x, ys, mod, final_g, has_ctx, final_norm):
    n = x.shape[0]
    tm = COMBINE_TM if has_ctx else COMBINE_TM_LATENT
    tpb = (SEQ_ALL if has_ctx else SEQ) // tm
    return pl.pallas_call(
        functools.partial(_combine_kernel, tm=tm, has_ctx=has_ctx, final_norm=final_norm),
        grid=(n // tm,),
        in_specs=[
            pl.BlockSpec((1, 1, tm), lambda i: (i, 0, 0), memory_space=pltpu.SMEM),
            pl.BlockSpec((tm, D_MODEL), lambda i: (i, 0)),
            *_mod_specs(tpb),
            pl.BlockSpec((1, D_MODEL), lambda i: (0, 0)),
            pl.BlockSpec(memory_space=pl.ANY),
        ],
        out_specs=pl.BlockSpec((tm, D_MODEL), lambda i: (i, 0)),
        out_shape=jax.ShapeDtypeStruct((n, D_MODEL), F32),
        scratch_shapes=[pltpu.VMEM((tm // 8, 8, D_MODEL), F32), pltpu.SemaphoreType.DMA],
        compiler_params=_params("arbitrary"),
        name="moe_combine",
    )(pos.reshape(n // tm, 1, tm), x, mod, mod, final_g.reshape(1, D_MODEL), ys)


def _routing_tables(slot, counts, n):
    n_blk = n // MOE_R + N_GROUPS
    n_g = counts[:N_GROUPS, 0]
    nb = (n_g + MOE_R - 1) // MOE_R
    ends = jnp.cumsum(nb)
    starts = ends - nb
    g_tok = slot // n
    pos = slot - g_tok * n + starts[g_tok] * MOE_R
    k = jnp.arange(n_blk, dtype=jnp.int32)
    g = jnp.minimum(jnp.sum((k[:, None] >= ends[None, :]).astype(jnp.int32), axis=1), N_GROUPS - 1)
    valid = jnp.where(k < ends[-1], jnp.clip(n_g[g] - (k - starts[g]) * MOE_R, 0, MOE_R), 0)
    pad = jnp.stack([jnp.append(starts * MOE_R + n_g, ends[-1] * MOE_R),
                     jnp.append(ends * MOE_R, n_blk * MOE_R)], axis=1)
    return (pos.astype(jnp.int32), g.astype(jnp.int32), valid.astype(jnp.int32),
            pad.astype(jnp.int32))


def _moe(hp, slot, counts, x, w1, w3, w2, layer, mod, final_g, has_ctx, final_norm):
    n = x.shape[0]
    assert n % MOE_R == 0
    pos, blk_grp, blk_valid, pad = _routing_tables(slot.reshape(n), counts, n)
    tok = _invert(pos, pad, n + N_GROUPS * MOE_R)
    ys = _experts(hp, tok, blk_grp, blk_valid, w1, w3, w2, layer)
    return _combine(pos, x, ys, mod, final_g, has_ctx, final_norm)


def _block_diag(w):
    eye = jnp.eye(HEADS, dtype=w.dtype)
    return jnp.einsum('hij,hg->higj', w, eye).reshape(WIDTH, WIDTH)


def _rope_tables():
    rows = SEQ // GRID_W
    row_ids = jnp.repeat(jnp.arange(rows, dtype=F32), GRID_W)
    col_ids = jnp.tile(jnp.arange(GRID_W, dtype=F32), rows)
    n_freq = DA_QD // 4
    freqs = ROPE_THETA ** (-jnp.arange(n_freq, dtype=F32) / n_freq)
    ang = jnp.stack([row_ids[:, None] * freqs, col_ids[:, None] * freqs], axis=1)
    cos = jnp.cos(ang)
    sin = jnp.sin(ang)
    cos_l = jnp.broadcast_to(cos[:, None, None, :, None, :], (SEQ, HEADS, 2, 2, 2, n_freq))
    sin_l = jnp.broadcast_to(sin[:, None, None, :, None, :], (SEQ, HEADS, 2, 2, 2, n_freq))
    sign = jnp.array([-1.0, 1.0], F32)[None, None, None, None, :, None]
    cos_l = cos_l.reshape(SEQ, WIDTH)
    sin_l = (sin_l * sign).reshape(SEQ, WIDTH)
    cos_all = jnp.concatenate([jnp.ones((CTX_LEN, WIDTH), F32), cos_l], axis=0)
    sin_all = jnp.concatenate([jnp.zeros((CTX_LEN, WIDTH), F32), sin_l], axis=0)
    return cos_all, sin_all


def kernel(x, c, ctx, c_ctx, w_ada, b_ada, norm1_g, norm2_g, w_in, w_out, lru_conv_w, lru_conv_b,
           lru_wr, lru_br, lru_wi, lru_bi, lru_lam, da_lam, da_subln_g, sg_norm_g, sg_w, sg_b,
           hg_lb, hg_norm_g, router_w, router_b, moe_w1, moe_w3, moe_w2, final_norm_g):
    nb = x.shape[0]
    assert nb <= CTX_ROW and x.shape[1:] == (SEQ, D_MODEL) and ctx.shape[1:] == (CTX_LEN, D_MODEL)
    xs = jnp.concatenate([ctx, x], axis=1).reshape(nb * SEQ_ALL, D_MODEL)

    cvec = jnp.zeros((ADA_ROWS, D_MODEL), F32).at[:nb].set(c).at[CTX_ROW].set(c_ctx)
    mods = _ada(cvec, w_ada, b_ada).reshape(DEPTH, ADA_ROWS, 6, D_MODEL)

    cos_all, sin_all = _rope_tables()
    lb_cum = jnp.cumsum(jax.nn.softmax(hg_lb.astype(F32), axis=1), axis=1)
    lb_all = lb_cum - lb_cum[:, :1]
    wr_t = router_w.T
    br = router_b.reshape(N_EXPERTS, 1)
    w1_bf16, w3_bf16, w2_bf16 = (w.astype(BF16) for w in (moe_w1, moe_w3, moe_w2))

    for l in range(DEPTH):
        last = l == DEPTH - 1
        out_rows = SEQ if last else SEQ_ALL
        z = _inproj(xs, norm1_g[l], mods[l], w_in[l].astype(BF16))

        ya = _lru(z, lru_conv_w[l], lru_conv_b[l].reshape(1, WIDTH),
                  jax.vmap(_block_diag)(lru_wr[l]).astype(BF16), lru_br[l],
                  jax.vmap(_block_diag)(lru_wi[l]).astype(BF16), lru_bi[l], lru_lam[l], out_rows)

        lam_init = 0.8 - 0.6 * math.exp(-0.3 * l)
        lf = da_lam[l].astype(F32)
        lam = jnp.exp(jnp.sum(lf[0] * lf[1])) - jnp.exp(jnp.sum(lf[2] * lf[3])) + lam_init
        yb = _attn(z, lam.reshape(1, 1), cos_all, sin_all,
                   jnp.tile(da_subln_g[l], HEADS).reshape(1, WIDTH),
                   with_ctx=not last, out_scale=1.0 - lam_init)

        w_cat = jnp.transpose(sg_w[l], (1, 0, 2)).reshape(SG_CHUNK, HEADS * SG_CHUNK).astype(BF16)
        bias2d = jnp.repeat(sg_b[l].T, HEAD_DIM, axis=1)
        ys = _sgu(z, sg_norm_g[l].reshape(1, WIDTH), w_cat, bias2d, out_rows)

        yh = _hgrn(z, lb_all[:, l], hg_norm_g[l].reshape(1, WIDTH), out_rows)

        xs, hp, pos, counts = _outproj(xs, (ya, yb, ys, yh), w_out[l].astype(BF16), mods[l],
                                       norm2_g[l], wr_t, br, has_ctx=not last)
        xs = _moe(hp, pos, counts, xs, w1_bf16, w3_bf16, w2_bf16, l, mods[l], final_norm_g,
                  has_ctx=not last, final_norm=last)

    return xs.reshape(nb, SEQ, D_MODEL)
```

```python
import functools
import math

import jax
import jax.numpy as jnp
from jax import lax
from jax.experimental import pallas as pl
from jax.experimental.pallas import tpu as pltpu

F32 = jnp.float32
BF16 = jnp.bfloat16

D_MODEL = 1024
SEQ = 2048
CTX_LEN = 256
SEQ_ALL = CTX_LEN + SEQ
DEPTH = 2
GRID_W = 64
EPS = 1e-6
LOG2_E = 1.4426950408889634
WIDTH = 256
HEADS = 4
HEAD_DIM = 64
CONV_W = 4
RG_C = 8.0
DA_QD = 32
ROPE_THETA = 10000.0
SG_CHUNK = 128
N_EXPERTS = 16
N_GROUPS = 4
EXP_PER_GROUP = 4
D_EXPERT = 512
IN_COLS = 3072
COL_A_X, COL_A_G = 0, 1
COL_B_Q, COL_B_K, COL_B_V = 2, 3, 4
COL_C_U, COL_C_V = 5, 6
COL_D_Q, COL_D_FF, COL_D_FB, COL_D_I, COL_D_G = 7, 8, 9, 10, 11

ADA_ROWS = 16
CTX_ROW = 8
VMEM_LIMIT = 56 * 1024 * 1024

LRU_BLK = 8
HG_CHUNK = 16
ATT_TQ = 256
DENSE_TM = 768
MOE_R = 512
HP_COLS = D_MODEL + 128
COMBINE_TM = 1152
COMBINE_TM_LATENT = 1024


def _params(*sem):
    return pltpu.CompilerParams(dimension_semantics=sem, vmem_limit_bytes=VMEM_LIMIT)


def _rms(xf, g):
    return xf * lax.rsqrt(jnp.mean(xf * xf, axis=-1, keepdims=True) + EPS) * g


def _sigmoid(x):
    return 1.0 / (1.0 + jnp.exp(-x))


def _silu(x):
    return x * _sigmoid(x)


def _gelu(x):
    return jax.nn.gelu(x)


def _split3(x):
    hi = x.astype(BF16)
    r = x - hi.astype(F32)
    mid = r.astype(BF16)
    lo = (r - mid.astype(F32)).astype(BF16)
    return hi, mid, lo


def _dot_f32_rhs(m_bf16, x):
    return functools.reduce(jnp.add, [jnp.dot(m_bf16, p, preferred_element_type=F32)
                                      for p in _split3(x)])


def _dot_f32_lhs(x, m_bf16):
    return functools.reduce(jnp.add, [jnp.dot(p, m_bf16, preferred_element_type=F32)
                                      for p in _split3(x)])


def _head_ones(dtype):
    r = lax.broadcasted_iota(jnp.int32, (WIDTH, WIDTH), 0) // HEAD_DIM
    c = lax.broadcasted_iota(jnp.int32, (WIDTH, WIDTH), 1) // HEAD_DIM
    return (r == c).astype(dtype)


def _mod_picker(modl_ref, modc_ref, tm, has_ctx):
    ml = modl_ref[0]
    if not has_ctx:
        return lambda r: ml[r:r + 1]
    mc = modc_ref[0]
    row0 = (pl.program_id(0) % (SEQ_ALL // tm)) * tm
    is_ctx = row0 + lax.broadcasted_iota(jnp.int32, (tm, 1), 0) < CTX_LEN
    return lambda r: jnp.where(is_ctx, mc[r:r + 1], ml[r:r + 1])


def _ada_kernel(c_ref, w_ref, b_ref, o_ref):
    s = _silu(c_ref[...]).astype(BF16)
    o_ref[...] = jnp.dot(s, w_ref[...].astype(BF16), preferred_element_type=F32) + b_ref[...]


def _ada(cvec, w_ada, b_ada):
    tn = 1536
    return pl.pallas_call(
        _ada_kernel,
        grid=(DEPTH, 6 * D_MODEL // tn),
        in_specs=[
            pl.BlockSpec((ADA_ROWS, D_MODEL), lambda l, j: (0, 0)),
            pl.BlockSpec((None, D_MODEL, tn), lambda l, j: (l, 0, j)),
            pl.BlockSpec((None, 1, tn), lambda l, j: (l, 0, j)),
        ],
        out_specs=pl.BlockSpec((None, ADA_ROWS, tn), lambda l, j: (l, 0, j)),
        out_shape=jax.ShapeDtypeStruct((DEPTH, ADA_ROWS, 6 * D_MODEL), F32),
        compiler_params=_params("arbitrary", "arbitrary"),
        name="ada",
    )(cvec, w_ada, b_ada.reshape(DEPTH, 1, 6 * D_MODEL))


def _inproj_kernel(x_ref, g_ref, modl_ref, modc_ref, w_ref, z_ref, *, tm):
    pick = _mod_picker(modl_ref, modc_ref, tm, True)
    h = _rms(x_ref[...], g_ref[...]) * (1.0 + pick(1)) + pick(0)
    z_ref[...] = jnp.dot(h.astype(BF16), w_ref[...], preferred_element_type=F32)


def _mod_specs(tiles_per_batch):
    return [
        pl.BlockSpec((1, 6, D_MODEL), lambda i, *_: (i // tiles_per_batch, 0, 0)),
        pl.BlockSpec((1, 6, D_MODEL), lambda i, *_: (CTX_ROW, 0, 0)),
    ]


def _inproj(x, g, mod, w_bf16):
    n = x.shape[0]
    tm = DENSE_TM
    return pl.pallas_call(
        functools.partial(_inproj_kernel, tm=tm),
        grid=(n // tm,),
        in_specs=[
            pl.BlockSpec((tm, D_MODEL), lambda i: (i, 0)),
            pl.BlockSpec((1, D_MODEL), lambda i: (0, 0)),
            *_mod_specs(SEQ_ALL // tm),
            pl.BlockSpec((D_MODEL, IN_COLS), lambda i: (0, 0)),
        ],
        out_specs=pl.BlockSpec((tm, IN_COLS), lambda i: (i, 0)),
        out_shape=jax.ShapeDtypeStruct((n, IN_COLS), F32),
        compiler_params=_params("arbitrary"),
        name="inproj",
    )(x, g.reshape(1, D_MODEL), mod, mod, w_bf16)


def _lru_kernel(x_ref, gate_ref, cw_ref, cb_ref, wr_ref, br_ref, wi_ref, bi_ref, lam_ref,
                y_ref, a_s, b_s, h_s, *, out_rows):
    x = x_ref[...]
    rows = lax.broadcasted_iota(jnp.int32, (SEQ_ALL, 1), 0)
    seg = rows < CTX_LEN
    u = jnp.zeros_like(x)
    for j in range(CONV_W):
        off = j - CONV_W // 2
        xs = x if off == 0 else pltpu.roll(x, (-off) % SEQ_ALL, 0)
        src = rows + off
        ok = (src >= 0) & (src < SEQ_ALL) & ((src < CTX_LEN) == seg)
        u = u + jnp.where(ok, xs, 0.0) * cw_ref[j:j + 1, :]
    u = u + cb_ref[...]
    ub = u.astype(BF16)
    for d in range(2):
        r = _sigmoid(jnp.dot(ub, wr_ref[d], preferred_element_type=F32) + br_ref[d:d + 1, :])
        i = _sigmoid(jnp.dot(ub, wi_ref[d], preferred_element_type=F32) + bi_ref[d:d + 1, :])
        nl = -lam_ref[d:d + 1, :]
        softplus = jnp.maximum(nl, 0.0) + jnp.log(1.0 + jnp.exp(-jnp.abs(nl)))
        log_a = -RG_C * r * softplus
        a = jnp.exp(log_a)
        a_s[d] = a
        b_s[d] = jnp.sqrt(1.0 - a * a) * i * u

    n_blk = SEQ_ALL // LRU_BLK
    n_ctx_blk = CTX_LEN // LRU_BLK
    sub = lax.broadcasted_iota(jnp.int32, (LRU_BLK, 1), 0)

    def block_scan(a, b, reverse):
        s = 1
        while s < LRU_BLK:
            if reverse:
                a_sh = pltpu.roll(a, LRU_BLK - s, 0)
                b_sh = pltpu.roll(b, LRU_BLK - s, 0)
                ok = sub < LRU_BLK - s
            else:
                a_sh = pltpu.roll(a, s, 0)
                b_sh = pltpu.roll(b, s, 0)
                ok = sub >= s
            b = jnp.where(ok, a * b_sh + b, b)
            a = jnp.where(ok, a * a_sh, a)
            s *= 2
        return a, b

    def body(n, carry):
        hf, hb = carry
        rf = pl.multiple_of(n * LRU_BLK, LRU_BLK)
        af, bf = block_scan(a_s[0, pl.ds(rf, LRU_BLK), :], b_s[0, pl.ds(rf, LRU_BLK), :], False)
        h = af * hf + bf
        h_s[0, pl.ds(rf, LRU_BLK), :] = h
        hf = h[LRU_BLK - 1:LRU_BLK, :]
        nb = jnp.where(n < n_ctx_blk, n_ctx_blk - 1 - n, n_blk + n_ctx_blk - 1 - n)
        rb = pl.multiple_of(nb * LRU_BLK, LRU_BLK)
        ab, bb = block_scan(a_s[1, pl.ds(rb, LRU_BLK), :], b_s[1, pl.ds(rb, LRU_BLK), :], True)
        h = ab * hb + bb
        h_s[1, pl.ds(rb, LRU_BLK), :] = h
        hb = h[0:1, :]
        return hf, hb

    zero = jnp.zeros((1, WIDTH), F32)
    lax.fori_loop(0, n_blk, body, (zero, zero))
    first = SEQ_ALL - out_rows
    y_ref[...] = (h_s[0, first:, :] + h_s[1, first:, :]) * _gelu(gate_ref[first:, :])


def _lru(z, cw, cb, wr_bd, br, wi_bd, bi, lam, out_rows):
    nb = z.shape[0] // SEQ_ALL
    full = lambda shape: pl.BlockSpec(shape, lambda b: (0,) * len(shape))
    return pl.pallas_call(
        functools.partial(_lru_kernel, out_rows=out_rows),
        grid=(nb,),
        in_specs=[
            pl.BlockSpec((SEQ_ALL, WIDTH), lambda b: (b, COL_A_X)),
            pl.BlockSpec((SEQ_ALL, WIDTH), lambda b: (b, COL_A_G)),
            full((CONV_W, WIDTH)), full((1, WIDTH)),
            full((2, WIDTH, WIDTH)), full((2, WIDTH)),
            full((2, WIDTH, WIDTH)), full((2, WIDTH)),
            full((2, WIDTH)),
        ],
        out_specs=pl.BlockSpec((out_rows, WIDTH), lambda b: (b, 0)),
        out_shape=jax.ShapeDtypeStruct((nb * out_rows, WIDTH), F32),
        scratch_shapes=[
            pltpu.VMEM((2, SEQ_ALL, WIDTH), F32),
            pltpu.VMEM((2, SEQ_ALL, WIDTH), F32),
            pltpu.VMEM((2, SEQ_ALL, WIDTH), F32),
        ],
        compiler_params=_params("arbitrary"),
        name="rglru",
    )(z, z, cw, cb, wr_bd, br, wi_bd, bi, lam)


def _rope(x, cos, sin_signed):
    lane = lax.broadcasted_iota(jnp.int32, (1, WIDTH), 1)
    first_half = (lane % 16) < 8
    partner = jnp.where(first_half, pltpu.roll(x, WIDTH - 8, 1), pltpu.roll(x, 8, 1))
    return x * cos + partner * sin_signed


def _attn_kernel(lam_ref, q_ref, k_ref, v_ref, cos_ref, sin_ref, g_ref, y_ref, kt_s, v_s,
                 *, with_ctx, out_scale):
    j = pl.program_id(1)

    @pl.when(j == 0)
    def _():
        kr = _rope(k_ref[...], cos_ref[...], sin_ref[...])
        kt_s[...] = kr.T.astype(BF16)
        v = v_ref[...]
        ones = jnp.ones((SEQ_ALL, HEAD_DIM), F32)
        for h in range(HEADS):
            v_s[h] = jnp.concatenate([v[:, h * HEAD_DIM:(h + 1) * HEAD_DIM], ones],
                                     axis=1).astype(BF16)

    lam = lam_ref[0, 0]
    tile = j if with_ctx else j + 1
    row0 = pl.multiple_of(tile * ATT_TQ, ATT_TQ)
    q = _rope(q_ref[...], cos_ref[pl.ds(row0, ATT_TQ), :], sin_ref[pl.ds(row0, ATT_TQ), :])
    q = q * (DA_QD ** -0.5)

    def attend(n_keys):
        outs = []
        for h in range(HEADS):
            scores = []
            for n in range(2):
                c0 = h * HEAD_DIM + n * DA_QD
                scores.append(jnp.dot(q[:, c0:c0 + DA_QD].astype(BF16),
                                      kt_s[c0:c0 + DA_QD, 0:n_keys], preferred_element_type=F32))
            probs = [jnp.exp(s - jnp.max(s, axis=-1, keepdims=True)).astype(BF16) for s in scores]
            branch = []
            for p in probs:
                ov = jnp.dot(p, v_s[h, 0:n_keys, :], preferred_element_type=F32)
                branch.append(ov[:, 0:HEAD_DIM] * (1.0 / ov[:, HEAD_DIM:HEAD_DIM + 1]))
            o = branch[0] - lam * branch[1]
            o = o * lax.rsqrt(jnp.mean(o * o, axis=-1, keepdims=True) + EPS)
            outs.append(o)
        y_ref[...] = jnp.concatenate(outs, axis=-1) * g_ref[...] * out_scale

    if with_ctx:
        @pl.when(j == 0)
        def _():
            attend(CTX_LEN)

        @pl.when(j > 0)
        def _():
            attend(SEQ_ALL)
    else:
        attend(SEQ_ALL)


def _attn(z, lam, cos, sin_signed, sub_g4, with_ctx, out_scale):
    nb = z.shape[0] // SEQ_ALL
    tiles = SEQ_ALL // ATT_TQ
    nq = tiles if with_ctx else tiles - 1
    first = 0 if with_ctx else 1
    return pl.pallas_call(
        functools.partial(_attn_kernel, with_ctx=with_ctx, out_scale=out_scale),
        grid=(nb, nq),
        in_specs=[
            pl.BlockSpec(memory_space=pltpu.SMEM),
            pl.BlockSpec((ATT_TQ, WIDTH), lambda b, j: (b * tiles + j + first, COL_B_Q)),
            pl.BlockSpec((SEQ_ALL, WIDTH), lambda b, j: (b, COL_B_K)),
            pl.BlockSpec((SEQ_ALL, WIDTH), lambda b, j: (b, COL_B_V)),
            pl.BlockSpec((SEQ_ALL, WIDTH), lambda b, j: (0, 0)),
            pl.BlockSpec((SEQ_ALL, WIDTH), lambda b, j: (0, 0)),
            pl.BlockSpec((1, WIDTH), lambda b, j: (0, 0)),
        ],
        out_specs=pl.BlockSpec((ATT_TQ, WIDTH), lambda b, j: (b * nq + j, 0)),
        out_shape=jax.ShapeDtypeStruct((nb * nq * ATT_TQ, WIDTH), F32),
        scratch_shapes=[
            pltpu.VMEM((WIDTH, SEQ_ALL), BF16),
            pltpu.VMEM((HEADS, SEQ_ALL, 2 * HEAD_DIM), BF16),
        ],
        compiler_params=_params("arbitrary", "arbitrary"),
        name="diffattn",
    )(lam, z, z, z, cos, sin_signed, sub_g4)


def _sgu_kernel(u_ref, v_ref, g_ref, w_ref, b_ref, y_ref, *, out_rows):
    row_head = lax.broadcasted_iota(jnp.int32, (HEADS * SG_CHUNK, WIDTH), 0) // SG_CHUNK
    col_head = lax.broadcasted_iota(jnp.int32, (HEADS * SG_CHUNK, WIDTH), 1) // HEAD_DIM
    head_mask = row_head == col_head
    w = w_ref[...]
    bias = b_ref[...]
    g = g_ref[...]
    first = SEQ_ALL - out_rows
    for n in range(first // SG_CHUNK, SEQ_ALL // SG_CHUNK):
        rows = slice(n * SG_CHUNK, (n + 1) * SG_CHUNK)
        vn = _rms(_gelu(v_ref[rows, :]), g).astype(BF16)
        stacked = jnp.where(head_mask, jnp.concatenate([vn] * HEADS, axis=0), jnp.zeros((), BF16))
        vm = jnp.dot(w, stacked, preferred_element_type=F32) + bias
        y_ref[n * SG_CHUNK - first:(n + 1) * SG_CHUNK - first, :] = _gelu(u_ref[rows, :]) * vm


def _sgu(z, norm_g, w_cat, bias2d, out_rows):
    nb = z.shape[0] // SEQ_ALL
    return pl.pallas_call(
        functools.partial(_sgu_kernel, out_rows=out_rows),
        grid=(nb,),
        in_specs=[
            pl.BlockSpec((SEQ_ALL, WIDTH), lambda b: (b, COL_C_U)),
            pl.BlockSpec((SEQ_ALL, WIDTH), lambda b: (b, COL_C_V)),
            pl.BlockSpec((1, WIDTH), lambda b: (0, 0)),
            pl.BlockSpec((SG_CHUNK, HEADS * SG_CHUNK), lambda b: (0, 0)),
            pl.BlockSpec((SG_CHUNK, WIDTH), lambda b: (0, 0)),
        ],
        out_specs=pl.BlockSpec((out_rows, WIDTH), lambda b: (b, 0)),
        out_shape=jax.ShapeDtypeStruct((nb * out_rows, WIDTH), F32),
        compiler_params=_params("arbitrary"),
        name="sgu",
    )(z, z, norm_g, w_cat, bias2d)


def _hgrn_kernel(q_ref, ff_ref, fb_ref, i_ref, gate_ref, lb_ref, gn_ref, y_ref,
                 b_s, k_s, o_s, st_s, stb_s, *, out_rows):
    c = HG_CHUNK
    pre = 128
    ones_bf = _head_ones(BF16)
    pr = lax.broadcasted_iota(jnp.int32, (pre, pre), 0)
    pc = lax.broadcasted_iota(jnp.int32, (pre, pre), 1)
    same = (pr // c) == (pc // c)
    tri = ((same & (pc <= pr)).astype(BF16), (same & (pc >= pr)).astype(BF16))

    for d, f_ref in enumerate((ff_ref, fb_ref)):
        lb = lb_ref[d:d + 1, :]
        for n in range(SEQ_ALL // pre):
            rows = slice(n * pre, (n + 1) * pre)
            f = lb + (1.0 - lb) * _sigmoid(f_ref[rows, :])
            k_s[d, rows, :] = 1.0 - f
            b_s[d, rows, :] = _dot_f32_rhs(tri[d], jnp.log(f) * LOG2_E)
    st_s[...] = jnp.zeros_like(st_s)
    stb_s[...] = jnp.zeros_like(stb_s)

    n_chunks = SEQ_ALL // c
    n_ctx = CTX_LEN // c
    half = c // 2
    sub = lax.broadcasted_iota(jnp.int32, (half, 1), 0)
    lane_head = lax.broadcasted_iota(jnp.int32, (1, 128), 1) // HEAD_DIM
    scale = HEAD_DIM ** -0.5

    def masked_out(d, s, tile):
        return (d == 0 and (tile + 1) * half <= s) or (d == 1 and tile * half > s)

    def front(d, chunk, need_out):
        r0 = pl.multiple_of(chunk * c, c)
        q = q_ref[pl.ds(r0, c), :] * scale
        v = i_ref[pl.ds(r0, c), :]
        b = b_s[d, pl.ds(r0, c), :]
        k = k_s[d, pl.ds(r0, c), :]
        tot = b[c - 1:c, :] if d == 0 else b[0:1, :]
        ke = (k * jnp.exp2(tot - b)).astype(BF16)
        upd = lax.dot_general(v.astype(BF16), ke, (((0,), (0,)), ((), ())),
                              preferred_element_type=F32)
        if not need_out:
            return r0, None, None, upd, v, jnp.exp2(tot)
        o = lax.dot_general((q * jnp.exp2(b)).astype(BF16), stb_s[d],
                            (((1,), (1,)), ((), ())), preferred_element_type=F32)
        parts = []
        for s in range(c):
            for tile in range(2):
                lo_row, hi_row = tile * half, (tile + 1) * half
                if masked_out(d, s, tile):
                    parts.append(jnp.zeros((half, WIDTH), F32))
                    continue
                diff = b[lo_row:hi_row, :] - b[s:s + 1, :]
                if d == 0 and lo_row < s:
                    diff = jnp.where(sub + lo_row >= s, diff, -jnp.inf)
                elif d == 1 and hi_row - 1 > s:
                    diff = jnp.where(sub + lo_row <= s, diff, -jnp.inf)
                parts.append(q[lo_row:hi_row, :] * k[s:s + 1, :] * jnp.exp2(diff))
        att = jnp.dot(jnp.concatenate(parts, axis=0).astype(BF16), ones_bf,
                      preferred_element_type=F32)
        return r0, o, att, upd, v, jnp.exp2(tot)

    def back(d, r0, o, att, upd, v, decay):
        if o is not None:
            tiles = [o[0:half, :], o[half:c, :]]
            for s in range(c):
                for tile in range(2):
                    if not masked_out(d, s, tile):
                        r = s * c + tile * half
                        tiles[tile] = tiles[tile] + att[r:r + half, :] * v[s:s + 1, :]
            o_s[d, pl.ds(r0, c), :] = jnp.concatenate(tiles, axis=0)
        for h in range(HEADS):
            r = slice(h * HEAD_DIM, (h + 1) * HEAD_DIM)
            ct = slice((h // 2) * 128, (h // 2 + 1) * 128)
            new = st_s[d, r, ct] * decay[:, ct] + jnp.where(lane_head == h % 2, upd[r, ct], 0.0)
            st_s[d, r, ct] = new
            stb_s[d, r, ct] = new.astype(BF16)

    def run(lo, hi, need_out):
        def body(n, carry):
            fwd = front(0, n, need_out)
            bwd = front(1, jnp.where(n < n_ctx, n_ctx - 1 - n, n_chunks + n_ctx - 1 - n), need_out)
            back(0, *fwd)
            back(1, *bwd)
            return carry

        lax.fori_loop(lo, hi, body, 0, unroll=4)

    if out_rows == SEQ_ALL:
        run(0, n_chunks, True)
    else:
        run(0, n_ctx, False)
        run(n_ctx, n_chunks, True)

    first = SEQ_ALL - out_rows
    o = o_s[0, first:, :] + o_s[1, first:, :]
    ms = _dot_f32_lhs(o * o, ones_bf) * (1.0 / HEAD_DIM)
    y_ref[...] = o * lax.rsqrt(ms + EPS) * gn_ref[...] * _silu(gate_ref[first:, :])


def _hgrn(z, lb, gn, out_rows):
    nb = z.shape[0] // SEQ_ALL
    col = lambda cidx: pl.BlockSpec((SEQ_ALL, WIDTH), lambda b: (b, cidx))
    return pl.pallas_call(
        functools.partial(_hgrn_kernel, out_rows=out_rows),
        grid=(nb,),
        in_specs=[
            col(COL_D_Q), col(COL_D_FF), col(COL_D_FB), col(COL_D_I), col(COL_D_G),
            pl.BlockSpec((2, WIDTH), lambda b: (0, 0)),
            pl.BlockSpec((1, WIDTH), lambda b: (0, 0)),
        ],
        out_specs=pl.BlockSpec((out_rows, WIDTH), lambda b: (b, 0)),
        out_shape=jax.ShapeDtypeStruct((nb * out_rows, WIDTH), F32),
        scratch_shapes=[
            pltpu.VMEM((2, SEQ_ALL, WIDTH), F32),
            pltpu.VMEM((2, SEQ_ALL, WIDTH), F32),
            pltpu.VMEM((2, SEQ_ALL, WIDTH), F32),
            pltpu.VMEM((2, WIDTH, WIDTH), F32),
            pltpu.VMEM((2, WIDTH, WIDTH), BF16),
        ],
        compiler_params=_params("arbitrary"),
        name="hgrn2",
    )(z, z, z, z, z, lb, gn)


def _route(logits):
    rows = [logits[e:e + 1, :] for e in range(N_EXPERTS)]
    m = functools.reduce(jnp.maximum, rows)
    ex = [jnp.exp(r - m) for r in rows]
    inv = 1.0 / functools.reduce(jnp.add, ex)
    sc = [e * inv for e in ex]
    g_score = []
    for g in range(N_GROUPS):
        grp = sc[g * EXP_PER_GROUP:(g + 1) * EXP_PER_GROUP]
        pairs = [grp[a] + grp[b] for a in range(EXP_PER_GROUP) for b in range(a + 1, EXP_PER_GROUP)]
        g_score.append(functools.reduce(jnp.maximum, pairs))
    gate4 = [jnp.zeros_like(sc[0]) for _ in range(EXP_PER_GROUP)]
    onehot = []
    for g in range(N_GROUPS):
        g_ok = None
        for o in range(N_GROUPS):
            if o == g:
                continue
            t = (g_score[g] > g_score[o]) if o < g else (g_score[g] >= g_score[o])
            g_ok = t if g_ok is None else (g_ok & t)
        grp = sc[g * EXP_PER_GROUP:(g + 1) * EXP_PER_GROUP]
        picked = []
        for a in range(EXP_PER_GROUP):
            beaten = jnp.zeros_like(grp[a])
            for o in range(EXP_PER_GROUP):
                if o == a:
                    continue
                t = (grp[o] >= grp[a]) if o < a else (grp[o] > grp[a])
                beaten = beaten + jnp.where(t, 1.0, 0.0)
            picked.append(jnp.where((beaten < 1.5) & g_ok, grp[a], 0.0))
        denom = functools.reduce(jnp.add, picked)
        denom = jnp.where(g_ok, denom, 1.0)
        gate4 = [acc + p / denom for acc, p in zip(gate4, picked)]
        onehot.append(jnp.where(g_ok, 1.0, 0.0))
    return gate4, onehot


def _outproj_kernel(x_ref, ya_ref, yb_ref, ys_ref, yh_ref, w_ref, modl_ref, modc_ref, g_ref,
                    wr_ref, br_ref, xo_ref, hp_ref, pos_ref, cnt_ref, cnt_s, before_s,
                    *, tm, has_ctx, cap):
    @pl.when(pl.program_id(0) == 0)
    def _():
        cnt_s[...] = jnp.zeros_like(cnt_s)
        before_s[...] = (lax.broadcasted_iota(jnp.int32, (tm, tm), 0)
                         <= lax.broadcasted_iota(jnp.int32, (tm, tm), 1)).astype(BF16)

    pick = _mod_picker(modl_ref, modc_ref, tm, has_ctx)
    acc = jnp.zeros((tm, D_MODEL), F32)
    for kblk, y_ref in enumerate((ya_ref, yb_ref, ys_ref, yh_ref)):
        acc = acc + jnp.dot(y_ref[...].astype(BF16), w_ref[kblk * WIDTH:(kblk + 1) * WIDTH, :],
                            preferred_element_type=F32)
    x = x_ref[...] + pick(2) * acc
    xo_ref[...] = x
    h = _rms(x, g_ref[...]) * (1.0 + pick(4)) + pick(3)
    nt = (((1,), (1,)), ((), ()))
    wr = wr_ref[...]
    wr_hi = wr.astype(BF16)
    wr_lo = (wr - wr_hi.astype(F32)).astype(BF16)
    h_hi = h.astype(BF16)
    h_lo = (h - h_hi.astype(F32)).astype(BF16)
    logits = (lax.dot_general(wr_hi, h_hi, nt, preferred_element_type=F32)
              + lax.dot_general(wr_hi, h_lo, nt, preferred_element_type=F32)
              + lax.dot_general(wr_lo, h_hi, nt, preferred_element_type=F32)) + br_ref[...]
    gate4, onehot = _route(logits)
    hp_ref[:, 0:D_MODEL] = h
    gate_rows = jnp.concatenate(gate4 + [jnp.zeros((128 - EXP_PER_GROUP, tm), F32)], axis=0)
    hp_ref[:, D_MODEL:] = gate_rows.T
    sel = jnp.concatenate(onehot + [jnp.zeros((8 - N_GROUPS, tm), F32)], axis=0)
    seen = jnp.dot(sel.astype(BF16), before_s[...], preferred_element_type=F32)
    carried = cnt_s[:, 0:1]
    base = lax.broadcasted_iota(jnp.int32, (8, 1), 0).astype(F32) * float(cap)
    slot = jnp.sum(sel * (base + carried + seen - 1.0), axis=0, keepdims=True)
    pos_ref[0] = slot.astype(jnp.int32)
    cnt_s[...] = cnt_s[...] + jnp.sum(sel, axis=1, keepdims=True)
    cnt_ref[...] = cnt_s[...].astype(jnp.int32)


def _outproj(x, ys, w_bf16, mod, g2, wr_t, br, has_ctx):
    n = ys[0].shape[0]
    if has_ctx:
        tm = DENSE_TM
        tpb = SEQ_ALL // tm
        x_spec = pl.BlockSpec((tm, D_MODEL), lambda i: (i, 0))
    else:
        tm = CTX_LEN
        tpb = SEQ // tm
        x_spec = pl.BlockSpec((tm, D_MODEL),
                              lambda i: ((i // tpb) * (SEQ_ALL // tm) + 1 + i % tpb, 0))
    tile = lambda w: pl.BlockSpec((tm, w), lambda i: (i, 0))
    return pl.pallas_call(
        functools.partial(_outproj_kernel, tm=tm, has_ctx=has_ctx, cap=n),
        grid=(n // tm,),
        in_specs=[
            x_spec, tile(WIDTH), tile(WIDTH), tile(WIDTH), tile(WIDTH),
            pl.BlockSpec((D_MODEL, D_MODEL), lambda i: (0, 0)),
            *_mod_specs(tpb),
            pl.BlockSpec((1, D_MODEL), lambda i: (0, 0)),
            pl.BlockSpec((N_EXPERTS, D_MODEL), lambda i: (0, 0)),
            pl.BlockSpec((N_EXPERTS, 1), lambda i: (0, 0)),
        ],
        out_specs=[
            tile(D_MODEL), tile(HP_COLS),
            pl.BlockSpec((1, 1, tm), lambda i: (i, 0, 0)),
            pl.BlockSpec((8, 128), lambda i: (0, 0)),
        ],
        out_shape=[
            jax.ShapeDtypeStruct((n, D_MODEL), F32),
            jax.ShapeDtypeStruct((n, HP_COLS), F32),
            jax.ShapeDtypeStruct((n // tm, 1, tm), jnp.int32),
            jax.ShapeDtypeStruct((8, 128), jnp.int32),
        ],
        scratch_shapes=[pltpu.VMEM((8, 128), F32), pltpu.VMEM((tm, tm), BF16)],
        compiler_params=_params("arbitrary"),
        name="outproj_router",
    )(x, *ys, w_bf16, mod, mod, g2.reshape(1, D_MODEL), wr_t, br)


def _row_copies_wait(dst_rows, sem):
    pltpu.make_async_copy(dst_rows, dst_rows, sem).wait()


def _invert_kernel(pos_ref, pad_ref, tok_ref):
    def clear(p, carry):
        tok_ref[p] = 0
        return carry

    def put(t, carry):
        tok_ref[pos_ref[t]] = t
        return carry

    for r in range(N_GROUPS + 1):
        lax.fori_loop(pad_ref[r, 0], pad_ref[r, 1], clear, 0)
    lax.fori_loop(0, pos_ref.shape[0], put, 0, unroll=16)


def _invert(pos, pad, n_slots):
    return pl.pallas_call(
        _invert_kernel,
        in_specs=[pl.BlockSpec(memory_space=pltpu.SMEM), pl.BlockSpec(memory_space=pltpu.SMEM)],
        out_specs=pl.BlockSpec(memory_space=pltpu.SMEM),
        out_shape=jax.ShapeDtypeStruct((n_slots,), jnp.int32),
        name="moe_invert",
    )(pos, pad)


def _experts_kernel(grp_ref, valid_ref, tok_ref, hp_hbm, w1_ref, w3_ref, w2_ref, ys_ref, buf, sems):
    k = pl.program_id(0)
    n_blk = pl.num_programs(0)
    n_valid = valid_ref[k]
    slot = k % 2

    def gather(blk, buf_slot):
        def body(i, carry):
            base = blk * MOE_R + i * 8
            for j in range(8):
                pltpu.make_async_copy(hp_hbm.at[pl.ds(tok_ref[base + j], 1)],
                                      buf.at[buf_slot, i, pl.ds(j, 1)], sems.at[buf_slot]).start()
            return carry

        lax.fori_loop(0, MOE_R // 8, body, 0)

    @pl.when(k == 0)
    def _():
        gather(0, 0)

    nxt = jnp.minimum(k + 1, n_blk - 1)

    @pl.when((k + 1 < n_blk) & (valid_ref[nxt] > 0))
    def _():
        gather(k + 1, 1 - slot)

    @pl.when(n_valid > 0)
    def _():
        _row_copies_wait(buf.at[slot], sems.at[slot])
        w = buf[slot].reshape(MOE_R, HP_COLS)
        ok = lax.broadcasted_iota(jnp.int32, (MOE_R, 1), 0) < n_valid
        h = jnp.where(ok, w[:, 0:D_MODEL], 0.0).astype(BF16)
        gates = jnp.where(ok, w[:, D_MODEL:], 0.0)
        acc = jnp.zeros((MOE_R, D_MODEL), F32)
        for e in range(EXP_PER_GROUP):
            a = jnp.dot(h, w1_ref[e], preferred_element_type=F32)
            b = jnp.dot(h, w3_ref[e], preferred_element_type=F32)
            he = (_silu(a) * b * gates[:, e:e + 1]).astype(BF16)
            acc = acc + jnp.dot(he, w2_ref[e], preferred_element_type=F32)
        ys_ref[...] = acc

    @pl.when(n_valid == 0)
    def _():
        ys_ref[...] = jnp.zeros_like(ys_ref)


def _experts(hp, tok, blk_grp, blk_valid, w1, w3, w2, layer):
    n_slots = tok.shape[0]
    up = pl.BlockSpec((None, EXP_PER_GROUP, D_MODEL, D_EXPERT),
                      lambda k, grp, valid, tok: (layer, grp[k], 0, 0))
    down = pl.BlockSpec((None, EXP_PER_GROUP, D_EXPERT, D_MODEL),
                        lambda k, grp, valid, tok: (layer, grp[k], 0, 0))
    grid_spec = pltpu.PrefetchScalarGridSpec(
        num_scalar_prefetch=3,
        grid=(n_slots // MOE_R,),
        in_specs=[pl.BlockSpec(memory_space=pl.ANY), up, up, down],
        out_specs=pl.BlockSpec((MOE_R, D_MODEL), lambda k, grp, valid, tok: (k, 0)),
        scratch_shapes=[pltpu.VMEM((2, MOE_R // 8, 8, HP_COLS), F32),
                        pltpu.SemaphoreType.DMA((2,))],
    )
    return pl.pallas_call(
        _experts_kernel,
        grid_spec=grid_spec,
        out_shape=jax.ShapeDtypeStruct((n_slots, D_MODEL), F32),
        compiler_params=_params("arbitrary"),
        name="moe_experts",
    )(blk_grp, blk_valid, tok, hp, w1, w3, w2)


def _combine_kernel(pos_ref, x_ref, modl_ref, modc_ref, fg_ref, ys_hbm, o_ref, buf, sem,
                    *, tm, has_ctx, final_norm):
    def body(i, carry):
        for j in range(8):
            pltpu.make_async_copy(ys_hbm.at[pl.ds(pos_ref[0, 0, i * 8 + j], 1)],
                                  buf.at[i, pl.ds(j, 1)], sem).start()
        return carry

    lax.fori_loop(0, tm // 8, body, 0)
    _row_copies_wait(buf, sem)
    pick = _mod_picker(modl_ref, modc_ref, tm, has_ctx)
    x = x_ref[...] + pick(5) * buf[...].reshape(tm, D_MODEL)
    o_ref[...] = _rms(x, fg_ref[...]) if final_norm else x


def _combine(pos, x, ys, mod, final_g, has_ctx, final_norm):
    n = x.shape[0]
    tm = COMBINE_TM if has_ctx else COMBINE_TM_LATENT
    tpb = (SEQ_ALL if has_ctx else SEQ) // tm
    return pl.pallas_call(
        functools.partial(_combine_kernel, tm=tm, has_ctx=has_ctx, final_norm=final_norm),
        grid=(n // tm,),
        in_specs=[
            pl.BlockSpec((1, 1, tm), lambda i: (i, 0, 0), memory_space=pltpu.SMEM),
            pl.BlockSpec((tm, D_MODEL), lambda i: (i, 0)),
            *_mod_specs(tpb),
            pl.BlockSpec((1, D_MODEL), lambda i: (0, 0)),
            pl.BlockSpec(memory_space=pl.ANY),
        ],
        out_specs=pl.BlockSpec((tm, D_MODEL), lambda i: (i, 0)),
        out_shape=jax.ShapeDtypeStruct((n, D_MODEL), F32),
        scratch_shapes=[pltpu.VMEM((tm // 8, 8, D_MODEL), F32), pltpu.SemaphoreType.DMA],
        compiler_params=_params("arbitrary"),
        name="moe_combine",
    )(pos.reshape(n // tm, 1, tm), x, mod, mod, final_g.reshape(1, D_MODEL), ys)


def _routing_tables(slot, counts, n):
    n_blk = n // MOE_R + N_GROUPS
    n_g = counts[:N_GROUPS, 0]
    nb = (n_g + MOE_R - 1) // MOE_R
    ends = jnp.cumsum(nb)
    starts = ends - nb
    g_tok = slot // n
    pos = slot - g_tok * n + starts[g_tok] * MOE_R
    k = jnp.arange(n_blk, dtype=jnp.int32)
    g = jnp.minimum(jnp.sum((k[:, None] >= ends[None, :]).astype(jnp.int32), axis=1), N_GROUPS - 1)
    valid = jnp.where(k < ends[-1], jnp.clip(n_g[g] - (k - starts[g]) * MOE_R, 0, MOE_R), 0)
    pad = jnp.stack([jnp.append(starts * MOE_R + n_g, ends[-1] * MOE_R),
                     jnp.append(ends * MOE_R, n_blk * MOE_R)], axis=1)
    return (pos.astype(jnp.int32), g.astype(jnp.int32), valid.astype(jnp.int32),
            pad.astype(jnp.int32))


def _moe(hp, slot, counts, x, w1, w3, w2, layer, mod, final_g, has_ctx, final_norm):
    n = x.shape[0]
    assert n % MOE_R == 0
    pos, blk_grp, blk_valid, pad = _routing_tables(slot.reshape(n), counts, n)
    tok = _invert(pos, pad, n + N_GROUPS * MOE_R)
    ys = _experts(hp, tok, blk_grp, blk_valid, w1, w3, w2, layer)
    return _combine(pos, x, ys, mod, final_g, has_ctx, final_norm)


def _block_diag(w):
    eye = jnp.eye(HEADS, dtype=w.dtype)
    return jnp.einsum('hij,hg->higj', w, eye).reshape(WIDTH, WIDTH)


def _rope_tables():
    rows = SEQ // GRID_W
    row_ids = jnp.repeat(jnp.arange(rows, dtype=F32), GRID_W)
    col_ids = jnp.tile(jnp.arange(GRID_W, dtype=F32), rows)
    n_freq = DA_QD // 4
    freqs = ROPE_THETA ** (-jnp.arange(n_freq, dtype=F32) / n_freq)
    ang = jnp.stack([row_ids[:, None] * freqs, col_ids[:, None] * freqs], axis=1)
    cos = jnp.cos(ang)
    sin = jnp.sin(ang)
    cos_l = jnp.broadcast_to(cos[:, None, None, :, None, :], (SEQ, HEADS, 2, 2, 2, n_freq))
    sin_l = jnp.broadcast_to(sin[:, None, None, :, None, :], (SEQ, HEADS, 2, 2, 2, n_freq))
    sign = jnp.array([-1.0, 1.0], F32)[None, None, None, None, :, None]
    cos_l = cos_l.reshape(SEQ, WIDTH)
    sin_l = (sin_l * sign).reshape(SEQ, WIDTH)
    cos_all = jnp.concatenate([jnp.ones((CTX_LEN, WIDTH), F32), cos_l], axis=0)
    sin_all = jnp.concatenate([jnp.zeros((CTX_LEN, WIDTH), F32), sin_l], axis=0)
    return cos_all, sin_all


def kernel(x, c, ctx, c_ctx, w_ada, b_ada, norm1_g, norm2_g, w_in, w_out, lru_conv_w, lru_conv_b,
           lru_wr, lru_br, lru_wi, lru_bi, lru_lam, da_lam, da_subln_g, sg_norm_g, sg_w, sg_b,
           hg_lb, hg_norm_g, router_w, router_b, moe_w1, moe_w3, moe_w2, final_norm_g):
    nb = x.shape[0]
    assert nb <= CTX_ROW and x.shape[1:] == (SEQ, D_MODEL) and ctx.shape[1:] == (CTX_LEN, D_MODEL)
    xs = jnp.concatenate([ctx, x], axis=1).reshape(nb * SEQ_ALL, D_MODEL)

    cvec = jnp.zeros((ADA_ROWS, D_MODEL), F32).at[:nb].set(c).at[CTX_ROW].set(c_ctx)
    mods = _ada(cvec, w_ada, b_ada).reshape(DEPTH, ADA_ROWS, 6, D_MODEL)

    cos_all, sin_all = _rope_tables()
    lb_cum = jnp.cumsum(jax.nn.softmax(hg_lb.astype(F32), axis=1), axis=1)
    lb_all = lb_cum - lb_cum[:, :1]
    wr_t = router_w.T
    br = router_b.reshape(N_EXPERTS, 1)
    w1_bf16, w3_bf16, w2_bf16 = (w.astype(BF16) for w in (moe_w1, moe_w3, moe_w2))

    for l in range(DEPTH):
        last = l == DEPTH - 1
        out_rows = SEQ if last else SEQ_ALL
        z = _inproj(xs, norm1_g[l], mods[l], w_in[l].astype(BF16))

        ya = _lru(z, lru_conv_w[l], lru_conv_b[l].reshape(1, WIDTH),
                  jax.vmap(_block_diag)(lru_wr[l]).astype(BF16), lru_br[l],
                  jax.vmap(_block_diag)(lru_wi[l]).astype(BF16), lru_bi[l], lru_lam[l], out_rows)

        lam_init = 0.8 - 0.6 * math.exp(-0.3 * l)
        lf = da_lam[l].astype(F32)
        lam = jnp.exp(jnp.sum(lf[0] * lf[1])) - jnp.exp(jnp.sum(lf[2] * lf[3])) + lam_init
        yb = _attn(z, lam.reshape(1, 1), cos_all, sin_all,
                   jnp.tile(da_subln_g[l], HEADS).reshape(1, WIDTH),
                   with_ctx=not last, out_scale=1.0 - lam_init)

        w_cat = jnp.transpose(sg_w[l], (1, 0, 2)).reshape(SG_CHUNK, HEADS * SG_CHUNK).astype(BF16)
        bias2d = jnp.repeat(sg_b[l].T, HEAD_DIM, axis=1)
        ys = _sgu(z, sg_norm_g[l].reshape(1, WIDTH), w_cat, bias2d, out_rows)

        yh = _hgrn(z, lb_all[:, l], hg_norm_g[l].reshape(1, WIDTH), out_rows)

        xs, hp, pos, counts = _outproj(xs, (ya, yb, ys, yh), w_out[l].astype(BF16), mods[l],
                                       norm2_g[l], wr_t, br, has_ctx=not last)
        xs = _moe(hp, pos, counts, xs, w1_bf16, w3_bf16, w2_bf16, l, mods[l], final_norm_g,
                  has_ctx=not last, final_norm=last)

    return xs.reshape(nb, SEQ, D_MODEL)
```

```python
import functools
import math

import jax
import jax.numpy as jnp
from jax import lax
from jax.experimental import pallas as pl
from jax.experimental.pallas import tpu as pltpu

F32 = jnp.float32
BF16 = jnp.bfloat16

D_MODEL = 1024
SEQ = 2048
CTX_LEN = 256
SEQ_ALL = CTX_LEN + SEQ
DEPTH = 2
GRID_W = 64
EPS = 1e-6
LOG2_E = 1.4426950408889634
WIDTH = 256
HEADS = 4
HEAD_DIM = 64
CONV_W = 4
RG_C = 8.0
DA_QD = 32
ROPE_THETA = 10000.0
SG_CHUNK = 128
N_EXPERTS = 16
N_GROUPS = 4
EXP_PER_GROUP = 4
D_EXPERT = 512
IN_COLS = 3072
COL_A_X, COL_A_G = 0, 1
COL_B_Q, COL_B_K, COL_B_V = 2, 3, 4
COL_C_U, COL_C_V = 5, 6
COL_D_Q, COL_D_FF, COL_D_FB, COL_D_I, COL_D_G = 7, 8, 9, 10, 11

ADA_ROWS = 16
CTX_ROW = 8
VMEM_LIMIT = 56 * 1024 * 1024

LRU_BLK = 8
HG_CHUNK = 16
ATT_TQ = 256
DENSE_TM = 768
MOE_R = 512
HP_COLS = D_MODEL + 128
COMBINE_TM = 1152
COMBINE_TM_LATENT = 1024


def _params(*sem):
    return pltpu.CompilerParams(dimension_semantics=sem, vmem_limit_bytes=VMEM_LIMIT)


def _rms(xf, g):
    return xf * lax.rsqrt(jnp.mean(xf * xf, axis=-1, keepdims=True) + EPS) * g


def _sigmoid(x):
    return 1.0 / (1.0 + jnp.exp(-x))


def _silu(x):
    return x * _sigmoid(x)


def _gelu(x):
    return jax.nn.gelu(x)


def _split3(x):
    hi = x.astype(BF16)
    r = x - hi.astype(F32)
    mid = r.astype(BF16)
    lo = (r - mid.astype(F32)).astype(BF16)
    return hi, mid, lo


def _dot_f32_rhs(m_bf16, x):
    return functools.reduce(jnp.add, [jnp.dot(m_bf16, p, preferred_element_type=F32)
                                      for p in _split3(x)])


def _dot_f32_lhs(x, m_bf16):
    return functools.reduce(jnp.add, [jnp.dot(p, m_bf16, preferred_element_type=F32)
                                      for p in _split3(x)])


def _head_ones(dtype):
    r = lax.broadcasted_iota(jnp.int32, (WIDTH, WIDTH), 0) // HEAD_DIM
    c = lax.broadcasted_iota(jnp.int32, (WIDTH, WIDTH), 1) // HEAD_DIM
    return (r == c).astype(dtype)


def _mod_picker(modl_ref, modc_ref, tm, has_ctx):
    ml = modl_ref[0]
    if not has_ctx:
        return lambda r: ml[r:r + 1]
    mc = modc_ref[0]
    row0 = (pl.program_id(0) % (SEQ_ALL // tm)) * tm
    is_ctx = row0 + lax.broadcasted_iota(jnp.int32, (tm, 1), 0) < CTX_LEN
    return lambda r: jnp.where(is_ctx, mc[r:r + 1], ml[r:r + 1])


def _ada_kernel(c_ref, w_ref, b_ref, o_ref):
    s = _silu(c_ref[...]).astype(BF16)
    o_ref[...] = jnp.dot(s, w_ref[...].astype(BF16), preferred_element_type=F32) + b_ref[...]


def _ada(cvec, w_ada, b_ada):
    tn = 1536
    return pl.pallas_call(
        _ada_kernel,
        grid=(DEPTH, 6 * D_MODEL // tn),
        in_specs=[
            pl.BlockSpec((ADA_ROWS, D_MODEL), lambda l, j: (0, 0)),
            pl.BlockSpec((None, D_MODEL, tn), lambda l, j: (l, 0, j)),
            pl.BlockSpec((None, 1, tn), lambda l, j: (l, 0, j)),
        ],
        out_specs=pl.BlockSpec((None, ADA_ROWS, tn), lambda l, j: (l, 0, j)),
        out_shape=jax.ShapeDtypeStruct((DEPTH, ADA_ROWS, 6 * D_MODEL), F32),
        compiler_params=_params("arbitrary", "arbitrary"),
        name="ada",
    )(cvec, w_ada, b_ada.reshape(DEPTH, 1, 6 * D_MODEL))


def _inproj_kernel(x_ref, g_ref, modl_ref, modc_ref, w_ref, z_ref, *, tm):
    pick = _mod_picker(modl_ref, modc_ref, tm, True)
    h = _rms(x_ref[...], g_ref[...]) * (1.0 + pick(1)) + pick(0)
    z_ref[...] = jnp.dot(h.astype(BF16), w_ref[...], preferred_element_type=F32)


def _mod_specs(tiles_per_batch):
    return [
        pl.BlockSpec((1, 6, D_MODEL), lambda i, *_: (i // tiles_per_batch, 0, 0)),
        pl.BlockSpec((1, 6, D_MODEL), lambda i, *_: (CTX_ROW, 0, 0)),
    ]


def _inproj(x, g, mod, w_bf16):
    n = x.shape[0]
    tm = DENSE_TM
    return pl.pallas_call(
        functools.partial(_inproj_kernel, tm=tm),
        grid=(n // tm,),
        in_specs=[
            pl.BlockSpec((tm, D_MODEL), lambda i: (i, 0)),
            pl.BlockSpec((1, D_MODEL), lambda i: (0, 0)),
            *_mod_specs(SEQ_ALL // tm),
            pl.BlockSpec((D_MODEL, IN_COLS), lambda i: (0, 0)),
        ],
        out_specs=pl.BlockSpec((tm, IN_COLS), lambda i: (i, 0)),
        out_shape=jax.ShapeDtypeStruct((n, IN_COLS), F32),
        compiler_params=_params("arbitrary"),
        name="inproj",
    )(x, g.reshape(1, D_MODEL), mod, mod, w_bf16)


def _lru_kernel(x_ref, gate_ref, cw_ref, cb_ref, wr_ref, br_ref, wi_ref, bi_ref, lam_ref,
                y_ref, a_s, b_s, h_s, *, out_rows):
    x = x_ref[...]
    rows = lax.broadcasted_iota(jnp.int32, (SEQ_ALL, 1), 0)
    seg = rows < CTX_LEN
    u = jnp.zeros_like(x)
    for j in range(CONV_W):
        off = j - CONV_W // 2
        xs = x if off == 0 else pltpu.roll(x, (-off) % SEQ_ALL, 0)
        src = rows + off
        ok = (src >= 0) & (src < SEQ_ALL) & ((src < CTX_LEN) == seg)
        u = u + jnp.where(ok, xs, 0.0) * cw_ref[j:j + 1, :]
    u = u + cb_ref[...]
    ub = u.astype(BF16)
    for d in range(2):
        r = _sigmoid(jnp.dot(ub, wr_ref[d], preferred_element_type=F32) + br_ref[d:d + 1, :])
        i = _sigmoid(jnp.dot(ub, wi_ref[d], preferred_element_type=F32) + bi_ref[d:d + 1, :])
        nl = -lam_ref[d:d + 1, :]
        softplus = jnp.maximum(nl, 0.0) + jnp.log(1.0 + jnp.exp(-jnp.abs(nl)))
        log_a = -RG_C * r * softplus
        a = jnp.exp(log_a)
        a_s[d] = a
        b_s[d] = jnp.sqrt(1.0 - a * a) * i * u

    n_blk = SEQ_ALL // LRU_BLK
    n_ctx_blk = CTX_LEN // LRU_BLK
    sub = lax.broadcasted_iota(jnp.int32, (LRU_BLK, 1), 0)

    def block_scan(a, b, reverse):
        s = 1
        while s < LRU_BLK:
            if reverse:
                a_sh = pltpu.roll(a, LRU_BLK - s, 0)
                b_sh = pltpu.roll(b, LRU_BLK - s, 0)
                ok = sub < LRU_BLK - s
            else:
                a_sh = pltpu.roll(a, s, 0)
                b_sh = pltpu.roll(b, s, 0)
                ok = sub >= s
            b = jnp.where(ok, a * b_sh + b, b)
            a = jnp.where(ok, a * a_sh, a)
            s *= 2
        return a, b

    def body(n, carry):
        hf, hb = carry
        rf = pl.multiple_of(n * LRU_BLK, LRU_BLK)
        af, bf = block_scan(a_s[0, pl.ds(rf, LRU_BLK), :], b_s[0, pl.ds(rf, LRU_BLK), :], False)
        h = af * hf + bf
        h_s[0, pl.ds(rf, LRU_BLK), :] = h
        hf = h[LRU_BLK - 1:LRU_BLK, :]
        nb = jnp.where(n < n_ctx_blk, n_ctx_blk - 1 - n, n_blk + n_ctx_blk - 1 - n)
        rb = pl.multiple_of(nb * LRU_BLK, LRU_BLK)
        ab, bb = block_scan(a_s[1, pl.ds(rb, LRU_BLK), :], b_s[1, pl.ds(rb, LRU_BLK), :], True)
        h = ab * hb + bb
        h_s[1, pl.ds(rb, LRU_BLK), :] = h
        hb = h[0:1, :]
        return hf, hb

    zero = jnp.zeros((1, WIDTH), F32)
    lax.fori_loop(0, n_blk, body, (zero, zero))
    first = SEQ_ALL - out_rows
    y_ref[...] = (h_s[0, first:, :] + h_s[1, first:, :]) * _gelu(gate_ref[first:, :])


def _lru(z, cw, cb, wr_bd, br, wi_bd, bi, lam, out_rows):
    nb = z.shape[0] // SEQ_ALL
    full = lambda shape: pl.BlockSpec(shape, lambda b: (0,) * len(shape))
    return pl.pallas_call(
        functools.partial(_lru_kernel, out_rows=out_rows),
        grid=(nb,),
        in_specs=[
            pl.BlockSpec((SEQ_ALL, WIDTH), lambda b: (b, COL_A_X)),
            pl.BlockSpec((SEQ_ALL, WIDTH), lambda b: (b, COL_A_G)),
            full((CONV_W, WIDTH)), full((1, WIDTH)),
            full((2, WIDTH, WIDTH)), full((2, WIDTH)),
            full((2, WIDTH, WIDTH)), full((2, WIDTH)),
            full((2, WIDTH)),
        ],
        out_specs=pl.BlockSpec((out_rows, WIDTH), lambda b: (b, 0)),
        out_shape=jax.ShapeDtypeStruct((nb * out_rows, WIDTH), F32),
        scratch_shapes=[
            pltpu.VMEM((2, SEQ_ALL, WIDTH), F32),
            pltpu.VMEM((2, SEQ_ALL, WIDTH), F32),
            pltpu.VMEM((2, SEQ_ALL, WIDTH), F32),
        ],
        compiler_params=_params("arbitrary"),
        name="rglru",
    )(z, z, cw, cb, wr_bd, br, wi_bd, bi, lam)


def _rope(x, cos, sin_signed):
    lane = lax.broadcasted_iota(jnp.int32, (1, WIDTH), 1)
    first_half = (lane % 16) < 8
    partner = jnp.where(first_half, pltpu.roll(x, WIDTH - 8, 1), pltpu.roll(x, 8, 1))
    return x * cos + partner * sin_signed


def _attn_kernel(lam_ref, q_ref, k_ref, v_ref, cos_ref, sin_ref, g_ref, y_ref, kt_s, v_s,
                 *, with_ctx, out_scale):
    j = pl.program_id(1)

    @pl.when(j == 0)
    def _():
        kr = _rope(k_ref[...], cos_ref[...], sin_ref[...])
        kt_s[...] = kr.T.astype(BF16)
        v = v_ref[...]
        ones = jnp.ones((SEQ_ALL, HEAD_DIM), F32)
        for h in range(HEADS):
            v_s[h] = jnp.concatenate([v[:, h * HEAD_DIM:(h + 1) * HEAD_DIM], ones],
                                     axis=1).astype(BF16)

    lam = lam_ref[0, 0]
    tile = j if with_ctx else j + 1
    row0 = pl.multiple_of(tile * ATT_TQ, ATT_TQ)
    q = _rope(q_ref[...], cos_ref[pl.ds(row0, ATT_TQ), :], sin_ref[pl.ds(row0, ATT_TQ), :])
    q = q * (DA_QD ** -0.5)

    def attend(n_keys):
        scores = []
        for hn in range(2 * HEADS):
            c0 = hn * DA_QD
            scores.append(jnp.dot(q[:, c0:c0 + DA_QD].astype(BF16), kt_s[c0:c0 + DA_QD, 0:n_keys],
                                  preferred_element_type=F32))
        probs = [jnp.exp(s - jnp.max(s, axis=-1, keepdims=True)).astype(BF16) for s in scores]
        outs = []
        for h in range(HEADS):
            branch = []
            for p in probs[2 * h:2 * h + 2]:
                ov = jnp.dot(p, v_s[h, 0:n_keys, :], preferred_element_type=F32)
                branch.append(ov[:, 0:HEAD_DIM] * (1.0 / ov[:, HEAD_DIM:HEAD_DIM + 1]))
            o = branch[0] - lam * branch[1]
            o = o * lax.rsqrt(jnp.mean(o * o, axis=-1, keepdims=True) + EPS)
            outs.append(o)
        y_ref[...] = jnp.concatenate(outs, axis=-1) * g_ref[...] * out_scale

    if with_ctx:
        @pl.when(j == 0)
        def _():
            attend(CTX_LEN)

        @pl.when(j > 0)
        def _():
            attend(SEQ_ALL)
    else:
        attend(SEQ_ALL)


def _attn(z, lam, cos, sin_signed, sub_g4, with_ctx, out_scale):
    nb = z.shape[0] // SEQ_ALL
    tiles = SEQ_ALL // ATT_TQ
    nq = tiles if with_ctx else tiles - 1
    first = 0 if with_ctx else 1
    return pl.pallas_call(
        functools.partial(_attn_kernel, with_ctx=with_ctx, out_scale=out_scale),
        grid=(nb, nq),
        in_specs=[
            pl.BlockSpec(memory_space=pltpu.SMEM),
            pl.BlockSpec((ATT_TQ, WIDTH), lambda b, j: (b * tiles + j + first, COL_B_Q)),
            pl.BlockSpec((SEQ_ALL, WIDTH), lambda b, j: (b, COL_B_K)),
            pl.BlockSpec((SEQ_ALL, WIDTH), lambda b, j: (b, COL_B_V)),
            pl.BlockSpec((SEQ_ALL, WIDTH), lambda b, j: (0, 0)),
            pl.BlockSpec((SEQ_ALL, WIDTH), lambda b, j: (0, 0)),
            pl.BlockSpec((1, WIDTH), lambda b, j: (0, 0)),
        ],
        out_specs=pl.BlockSpec((ATT_TQ, WIDTH), lambda b, j: (b * nq + j, 0)),
        out_shape=jax.ShapeDtypeStruct((nb * nq * ATT_TQ, WIDTH), F32),
        scratch_shapes=[
            pltpu.VMEM((WIDTH, SEQ_ALL), BF16),
            pltpu.VMEM((HEADS, SEQ_ALL, 2 * HEAD_DIM), BF16),
        ],
        compiler_params=_params("arbitrary", "arbitrary"),
        name="diffattn",
    )(lam, z, z, z, cos, sin_signed, sub_g4)


def _sgu_kernel(u_ref, v_ref, g_ref, w_ref, b_ref, y_ref, *, out_rows):
    row_head = lax.broadcasted_iota(jnp.int32, (HEADS * SG_CHUNK, WIDTH), 0) // SG_CHUNK
    col_head = lax.broadcasted_iota(jnp.int32, (HEADS * SG_CHUNK, WIDTH), 1) // HEAD_DIM
    head_mask = row_head == col_head
    w = w_ref[...]
    bias = b_ref[...]
    g = g_ref[...]
    first = SEQ_ALL - out_rows
    for n in range(first // SG_CHUNK, SEQ_ALL // SG_CHUNK):
        rows = slice(n * SG_CHUNK, (n + 1) * SG_CHUNK)
        vn = _rms(_gelu(v_ref[rows, :]), g).astype(BF16)
        stacked = jnp.where(head_mask, jnp.concatenate([vn] * HEADS, axis=0), jnp.zeros((), BF16))
        vm = jnp.dot(w, stacked, preferred_element_type=F32) + bias
        y_ref[n * SG_CHUNK - first:(n + 1) * SG_CHUNK - first, :] = _gelu(u_ref[rows, :]) * vm


def _sgu(z, norm_g, w_cat, bias2d, out_rows):
    nb = z.shape[0] // SEQ_ALL
    return pl.pallas_call(
        functools.partial(_sgu_kernel, out_rows=out_rows),
        grid=(nb,),
        in_specs=[
            pl.BlockSpec((SEQ_ALL, WIDTH), lambda b: (b, COL_C_U)),
            pl.BlockSpec((SEQ_ALL, WIDTH), lambda b: (b, COL_C_V)),
            pl.BlockSpec((1, WIDTH), lambda b: (0, 0)),
            pl.BlockSpec((SG_CHUNK, HEADS * SG_CHUNK), lambda b: (0, 0)),
            pl.BlockSpec((SG_CHUNK, WIDTH), lambda b: (0, 0)),
        ],
        out_specs=pl.BlockSpec((out_rows, WIDTH), lambda b: (b, 0)),
        out_shape=jax.ShapeDtypeStruct((nb * out_rows, WIDTH), F32),
        compiler_params=_params("arbitrary"),
        name="sgu",
    )(z, z, norm_g, w_cat, bias2d)


def _hgrn_kernel(q_ref, ff_ref, fb_ref, i_ref, gate_ref, lb_ref, gn_ref, y_ref,
                 b_s, k_s, o_s, st_s, stb_s, *, out_rows):
    c = HG_CHUNK
    pre = 128
    ones_bf = _head_ones(BF16)
    pr = lax.broadcasted_iota(jnp.int32, (pre, pre), 0)
    pc = lax.broadcasted_iota(jnp.int32, (pre, pre), 1)
    same = (pr // c) == (pc // c)
    tri = ((same & (pc <= pr)).astype(BF16), (same & (pc >= pr)).astype(BF16))

    for d, f_ref in enumerate((ff_ref, fb_ref)):
        lb = lb_ref[d:d + 1, :]
        for n in range(SEQ_ALL // pre):
            rows = slice(n * pre, (n + 1) * pre)
            f = lb + (1.0 - lb) * _sigmoid(f_ref[rows, :])
            k_s[d, rows, :] = 1.0 - f
            b_s[d, rows, :] = _dot_f32_rhs(tri[d], jnp.log(f) * LOG2_E)
    st_s[...] = jnp.zeros_like(st_s)
    stb_s[...] = jnp.zeros_like(stb_s)

    n_chunks = SEQ_ALL // c
    n_ctx = CTX_LEN // c
    half = c // 2
    sub = lax.broadcasted_iota(jnp.int32, (half, 1), 0)
    lane_head = lax.broadcasted_iota(jnp.int32, (1, 128), 1) // HEAD_DIM
    scale = HEAD_DIM ** -0.5

    def masked_out(d, s, tile):
        return (d == 0 and (tile + 1) * half <= s) or (d == 1 and tile * half > s)

    def front(d, chunk, need_out):
        r0 = pl.multiple_of(chunk * c, c)
        q = q_ref[pl.ds(r0, c), :] * scale
        v = i_ref[pl.ds(r0, c), :]
        b = b_s[d, pl.ds(r0, c), :]
        k = k_s[d, pl.ds(r0, c), :]
        tot = b[c - 1:c, :] if d == 0 else b[0:1, :]
        ke = (k * jnp.exp2(tot - b)).astype(BF16)
        upd = lax.dot_general(v.astype(BF16), ke, (((0,), (0,)), ((), ())),
                              preferred_element_type=F32)
        if not need_out:
            return r0, None, None, upd, v, jnp.exp2(tot)
        o = lax.dot_general((q * jnp.exp2(b)).astype(BF16), stb_s[d],
                            (((1,), (1,)), ((), ())), preferred_element_type=F32)
        parts = []
        for s in range(c):
            for tile in range(2):
                lo_row, hi_row = tile * half, (tile + 1) * half
                if masked_out(d, s, tile):
                    parts.append(jnp.zeros((half, WIDTH), F32))
                    continue
                diff = b[lo_row:hi_row, :] - b[s:s + 1, :]
                if d == 0 and lo_row < s:
                    diff = jnp.where(sub + lo_row >= s, diff, -jnp.inf)
                elif d == 1 and hi_row - 1 > s:
                    diff = jnp.where(sub + lo_row <= s, diff, -jnp.inf)
                parts.append(q[lo_row:hi_row, :] * k[s:s + 1, :] * jnp.exp2(diff))
        att = jnp.dot(jnp.concatenate(parts, axis=0).astype(BF16), ones_bf,
                      preferred_element_type=F32)
        return r0, o, att, upd, v, jnp.exp2(tot)

    def back(d, r0, o, att, upd, v, decay):
        if o is not None:
            tiles = [o[0:half, :], o[half:c, :]]
            for s in range(c):
                for tile in range(2):
                    if not masked_out(d, s, tile):
                        r = s * c + tile * half
                        tiles[tile] = tiles[tile] + att[r:r + half, :] * v[s:s + 1, :]
            o_s[d, pl.ds(r0, c), :] = jnp.concatenate(tiles, axis=0)
        for h in range(HEADS):
            r = slice(h * HEAD_DIM, (h + 1) * HEAD_DIM)
            ct = slice((h // 2) * 128, (h // 2 + 1) * 128)
            new = st_s[d, r, ct] * decay[:, ct] + jnp.where(lane_head == h % 2, upd[r, ct], 0.0)
            st_s[d, r, ct] = new
            stb_s[d, r, ct] = new.astype(BF16)

    def run(lo, hi, need_out):
        def body(n, carry):
            fwd = front(0, n, need_out)
            bwd = front(1, jnp.where(n < n_ctx, n_ctx - 1 - n, n_chunks + n_ctx - 1 - n), need_out)
            back(0, *fwd)
            back(1, *bwd)
            return carry

        lax.fori_loop(lo, hi, body, 0, unroll=4)

    if out_rows == SEQ_ALL:
        run(0, n_chunks, True)
    else:
        run(0, n_ctx, False)
        run(n_ctx, n_chunks, True)

    first = SEQ_ALL - out_rows
    o = o_s[0, first:, :] + o_s[1, first:, :]
    ms = _dot_f32_lhs(o * o, ones_bf) * (1.0 / HEAD_DIM)
    y_ref[...] = o * lax.rsqrt(ms + EPS) * gn_ref[...] * _silu(gate_ref[first:, :])


def _hgrn(z, lb, gn, out_rows):
    nb = z.shape[0] // SEQ_ALL
    col = lambda cidx: pl.BlockSpec((SEQ_ALL, WIDTH), lambda b: (b, cidx))
    return pl.pallas_call(
        functools.partial(_hgrn_kernel, out_rows=out_rows),
        grid=(nb,),
        in_specs=[
            col(COL_D_Q), col(COL_D_FF), col(COL_D_FB), col(COL_D_I), col(COL_D_G),
            pl.BlockSpec((2, WIDTH), lambda b: (0, 0)),
            pl.BlockSpec((1, WIDTH), lambda b: (0, 0)),
        ],
        out_specs=pl.BlockSpec((out_rows, WIDTH), lambda b: (b, 0)),
        out_shape=jax.ShapeDtypeStruct((nb * out_rows, WIDTH), F32),
        scratch_shapes=[
            pltpu.VMEM((2, SEQ_ALL, WIDTH), F32),
            pltpu.VMEM((2, SEQ_ALL, WIDTH), F32),
            pltpu.VMEM((2, SEQ_ALL, WIDTH), F32),
            pltpu.VMEM((2, WIDTH, WIDTH), F32),
            pltpu.VMEM((2, WIDTH, WIDTH), BF16),
        ],
        compiler_params=_params("arbitrary"),
        name="hgrn2",
    )(z, z, z, z, z, lb, gn)


def _route(logits):
    rows = [logits[e:e + 1, :] for e in range(N_EXPERTS)]
    m = functools.reduce(jnp.maximum, rows)
    ex = [jnp.exp(r - m) for r in rows]
    inv = 1.0 / functools.reduce(jnp.add, ex)
    sc = [e * inv for e in ex]
    g_score = []
    for g in range(N_GROUPS):
        grp = sc[g * EXP_PER_GROUP:(g + 1) * EXP_PER_GROUP]
        pairs = [grp[a] + grp[b] for a in range(EXP_PER_GROUP) for b in range(a + 1, EXP_PER_GROUP)]
        g_score.append(functools.reduce(jnp.maximum, pairs))
    gate4 = [jnp.zeros_like(sc[0]) for _ in range(EXP_PER_GROUP)]
    onehot = []
    for g in range(N_GROUPS):
        g_ok = None
        for o in range(N_GROUPS):
            if o == g:
                continue
            t = (g_score[g] > g_score[o]) if o < g else (g_score[g] >= g_score[o])
            g_ok = t if g_ok is None else (g_ok & t)
        grp = sc[g * EXP_PER_GROUP:(g + 1) * EXP_PER_GROUP]
        picked = []
        for a in range(EXP_PER_GROUP):
            beaten = jnp.zeros_like(grp[a])
            for o in range(EXP_PER_GROUP):
                if o == a:
                    continue
                t = (grp[o] >= grp[a]) if o < a else (grp[o] > grp[a])
                beaten = beaten + jnp.where(t, 1.0, 0.0)
            picked.append(jnp.where((beaten < 1.5) & g_ok, grp[a], 0.0))
        denom = functools.reduce(jnp.add, picked)
        denom = jnp.where(g_ok, denom, 1.0)
        gate4 = [acc + p / denom for acc, p in zip(gate4, picked)]
        onehot.append(jnp.where(g_ok, 1.0, 0.0))
    return gate4, onehot


def _outproj_kernel(x_ref, ya_ref, yb_ref, ys_ref, yh_ref, w_ref, modl_ref, modc_ref, g_ref,
                    wr_ref, br_ref, xo_ref, hp_ref, pos_ref, cnt_ref, cnt_s, before_s,
                    *, tm, has_ctx, cap):
    @pl.when(pl.program_id(0) == 0)
    def _():
        cnt_s[...] = jnp.zeros_like(cnt_s)
        before_s[...] = (lax.broadcasted_iota(jnp.int32, (tm, tm), 0)
                         <= lax.broadcasted_iota(jnp.int32, (tm, tm), 1)).astype(BF16)

    pick = _mod_picker(modl_ref, modc_ref, tm, has_ctx)
    acc = jnp.zeros((tm, D_MODEL), F32)
    for kblk, y_ref in enumerate((ya_ref, yb_ref, ys_ref, yh_ref)):
        acc = acc + jnp.dot(y_ref[...].astype(BF16), w_ref[kblk * WIDTH:(kblk + 1) * WIDTH, :],
                            preferred_element_type=F32)
    x = x_ref[...] + pick(2) * acc
    xo_ref[...] = x
    h = _rms(x, g_ref[...]) * (1.0 + pick(4)) + pick(3)
    nt = (((1,), (1,)), ((), ()))
    wr = wr_ref[...]
    wr_hi = wr.astype(BF16)
    wr_lo = (wr - wr_hi.astype(F32)).astype(BF16)
    h_hi = h.astype(BF16)
    h_lo = (h - h_hi.astype(F32)).astype(BF16)
    logits = (lax.dot_general(wr_hi, h_hi, nt, preferred_element_type=F32)
              + lax.dot_general(wr_hi, h_lo, nt, preferred_element_type=F32)
              + lax.dot_general(wr_lo, h_hi, nt, preferred_element_type=F32)) + br_ref[...]
    gate4, onehot = _route(logits)
    hp_ref[:, 0:D_MODEL] = h
    gate_rows = jnp.concatenate(gate4 + [jnp.zeros((128 - EXP_PER_GROUP, tm), F32)], axis=0)
    hp_ref[:, D_MODEL:] = gate_rows.T
    sel = jnp.concatenate(onehot + [jnp.zeros((8 - N_GROUPS, tm), F32)], axis=0)
    seen = jnp.dot(sel.astype(BF16), before_s[...], preferred_element_type=F32)
    carried = cnt_s[:, 0:1]
    base = lax.broadcasted_iota(jnp.int32, (8, 1), 0).astype(F32) * float(cap)
    slot = jnp.sum(sel * (base + carried + seen - 1.0), axis=0, keepdims=True)
    pos_ref[0] = slot.astype(jnp.int32)
    cnt_s[...] = cnt_s[...] + jnp.sum(sel, axis=1, keepdims=True)
    cnt_ref[...] = cnt_s[...].astype(jnp.int32)


def _outproj(x, ys, w_bf16, mod, g2, wr_t, br, has_ctx):
    n = ys[0].shape[0]
    if has_ctx:
        tm = DENSE_TM
        tpb = SEQ_ALL // tm
        x_spec = pl.BlockSpec((tm, D_MODEL), lambda i: (i, 0))
    else:
        tm = CTX_LEN
        tpb = SEQ // tm
        x_spec = pl.BlockSpec((tm, D_MODEL),
                              lambda i: ((i // tpb) * (SEQ_ALL // tm) + 1 + i % tpb, 0))
    tile = lambda w: pl.BlockSpec((tm, w), lambda i: (i, 0))
    return pl.pallas_call(
        functools.partial(_outproj_kernel, tm=tm, has_ctx=has_ctx, cap=n),
        grid=(n // tm,),
        in_specs=[
            x_spec, tile(WIDTH), tile(WIDTH), tile(WIDTH), tile(WIDTH),
            pl.BlockSpec((D_MODEL, D_MODEL), lambda i: (0, 0)),
            *_mod_specs(tpb),
            pl.BlockSpec((1, D_MODEL), lambda i: (0, 0)),
            pl.BlockSpec((N_EXPERTS, D_MODEL), lambda i: (0, 0)),
            pl.BlockSpec((N_EXPERTS, 1), lambda i: (0, 0)),
        ],
        out_specs=[
            tile(D_MODEL), tile(HP_COLS),
            pl.BlockSpec((1, 1, tm), lambda i: (i, 0, 0)),
            pl.BlockSpec((8, 128), lambda i: (0, 0)),
        ],
        out_shape=[
            jax.ShapeDtypeStruct((n, D_MODEL), F32),
            jax.ShapeDtypeStruct((n, HP_COLS), F32),
            jax.ShapeDtypeStruct((n // tm, 1, tm), jnp.int32),
            jax.ShapeDtypeStruct((8, 128), jnp.int32),
        ],
        scratch_shapes=[pltpu.VMEM((8, 128), F32), pltpu.VMEM((tm, tm), BF16)],
        compiler_params=_params("arbitrary"),
        name="outproj_router",
    )(x, *ys, w_bf16, mod, mod, g2.reshape(1, D_MODEL), wr_t, br)


def _row_copies_wait(dst_rows, sem):
    pltpu.make_async_copy(dst_rows, dst_rows, sem).wait()


def _invert_kernel(pos_ref, pad_ref, tok_ref):
    def clear(p, carry):
        tok_ref[p] = 0
        return carry

    def put(t, carry):
        tok_ref[pos_ref[t]] = t
        return carry

    for r in range(N_GROUPS + 1):
        lax.fori_loop(pad_ref[r, 0], pad_ref[r, 1], clear, 0)
    lax.fori_loop(0, pos_ref.shape[0], put, 0, unroll=16)


def _invert(pos, pad, n_slots):
    return pl.pallas_call(
        _invert_kernel,
        in_specs=[pl.BlockSpec(memory_space=pltpu.SMEM), pl.BlockSpec(memory_space=pltpu.SMEM)],
        out_specs=pl.BlockSpec(memory_space=pltpu.SMEM),
        out_shape=jax.ShapeDtypeStruct((n_slots,), jnp.int32),
        name="moe_invert",
    )(pos, pad)


def _experts_kernel(grp_ref, valid_ref, tok_ref, hp_hbm, w1_ref, w3_ref, w2_ref, ys_ref, buf, sems):
    k = pl.program_id(0)
    n_blk = pl.num_programs(0)
    n_valid = valid_ref[k]
    slot = k % 2

    def gather(blk, buf_slot):
        def body(i, carry):
            base = blk * MOE_R + i * 8
            for j in range(8):
                pltpu.make_async_copy(hp_hbm.at[pl.ds(tok_ref[base + j], 1)],
                                      buf.at[buf_slot, i, pl.ds(j, 1)], sems.at[buf_slot]).start()
            return carry

        lax.fori_loop(0, MOE_R // 8, body, 0)

    @pl.when(k == 0)
    def _():
        gather(0, 0)

    nxt = jnp.minimum(k + 1, n_blk - 1)

    @pl.when((k + 1 < n_blk) & (valid_ref[nxt] > 0))
    def _():
        gather(k + 1, 1 - slot)

    @pl.when(n_valid > 0)
    def _():
        _row_copies_wait(buf.at[slot], sems.at[slot])
        w = buf[slot].reshape(MOE_R, HP_COLS)
        ok = lax.broadcasted_iota(jnp.int32, (MOE_R, 1), 0) < n_valid
        h = jnp.where(ok, w[:, 0:D_MODEL], 0.0).astype(BF16)
        gates = jnp.where(ok, w[:, D_MODEL:], 0.0)
        acc = jnp.zeros((MOE_R, D_MODEL), F32)
        for e in range(EXP_PER_GROUP):
            a = jnp.dot(h, w1_ref[e], preferred_element_type=F32)
            b = jnp.dot(h, w3_ref[e], preferred_element_type=F32)
            he = (_silu(a) * b * gates[:, e:e + 1]).astype(BF16)
            acc = acc + jnp.dot(he, w2_ref[e], preferred_element_type=F32)
        ys_ref[...] = acc

    @pl.when(n_valid == 0)
    def _():
        ys_ref[...] = jnp.zeros_like(ys_ref)


def _experts(hp, tok, blk_grp, blk_valid, w1, w3, w2, layer):
    n_slots = tok.shape[0]
    up = pl.BlockSpec((None, EXP_PER_GROUP, D_MODEL, D_EXPERT),
                      lambda k, grp, valid, tok: (layer, grp[k], 0, 0))
    down = pl.BlockSpec((None, EXP_PER_GROUP, D_EXPERT, D_MODEL),
                        lambda k, grp, valid, tok: (layer, grp[k], 0, 0))
    grid_spec = pltpu.PrefetchScalarGridSpec(
        num_scalar_prefetch=3,
        grid=(n_slots // MOE_R,),
        in_specs=[pl.BlockSpec(memory_space=pl.ANY), up, up, down],
        out_specs=pl.BlockSpec((MOE_R, D_MODEL), lambda k, grp, valid, tok: (k, 0)),
        scratch_shapes=[pltpu.VMEM((2, MOE_R // 8, 8, HP_COLS), F32),
                        pltpu.SemaphoreType.DMA((2,))],
    )
    return pl.pallas_call(
        _experts_kernel,
        grid_spec=grid_spec,
        out_shape=jax.ShapeDtypeStruct((n_slots, D_MODEL), F32),
        compiler_params=_params("arbitrary"),
        name="moe_experts",
    )(blk_grp, blk_valid, tok, hp, w1, w3, w2)


def _combine_kernel(pos_ref, x_ref, modl_ref, modc_ref, fg_ref, ys_hbm, o_ref, buf, sem,
                    *, tm, has_ctx, final_norm):
    def body(i, carry):
        for j in range(8):
            pltpu.make_async_copy(ys_hbm.at[pl.ds(pos_ref[0, 0, i * 8 + j], 1)],
                                  buf.at[i, pl.ds(j, 1)], sem).start()
        return carry

    lax.fori_loop(0, tm // 8, body, 0)
    _row_copies_wait(buf, sem)
    pick = _mod_picker(modl_ref, modc_ref, tm, has_ctx)
    x = x_ref[...] + pick(5) * buf[...].reshape(tm, D_MODEL)
    o_ref[...] = _rms(x, fg_ref[...]) if final_norm else x


def _combine(pos, x, ys, mod, final_g, has_ctx, final_norm):
    n = x.shape[0]
    tm = COMBINE_TM if has_ctx else COMBINE_TM_LATENT
    tpb = (SEQ_ALL if has_ctx else SEQ) // tm
    return pl.pallas_call(
        functools.partial(_combine_kernel, tm=tm, has_ctx=has_ctx, final_norm=final_norm),
        grid=(n // tm,),
        in_specs=[
            pl.BlockSpec((1, 1, tm), lambda i: (i, 0, 0), memory_space=pltpu.SMEM),
            pl.BlockSpec((tm, D_MODEL), lambda i: (i, 0)),
            *_mod_specs(tpb),
            pl.BlockSpec((1, D_MODEL), lambda i: (0, 0)),
            pl.BlockSpec(memory_space=pl.ANY),
        ],
        out_specs=pl.BlockSpec((tm, D_MODEL), lambda i: (i, 0)),
        out_shape=jax.ShapeDtypeStruct((n, D_MODEL), F32),
        scratch_shapes=[pltpu.VMEM((tm // 8, 8, D_MODEL), F32), pltpu.SemaphoreType.DMA],
        compiler_params=_params("arbitrary"),
        name="moe_combine",
    )(pos.reshape(n // tm, 1, tm), x, mod, mod, final_g.reshape(1, D_MODEL), ys)


def _routing_tables(slot, counts, n):
    n_blk = n // MOE_R + N_GROUPS
    n_g = counts[:N_GROUPS, 0]
    nb = (n_g + MOE_R - 1) // MOE_R
    ends = jnp.cumsum(nb)
    starts = ends - nb
    g_tok = slot // n
    pos = slot - g_tok * n + starts[g_tok] * MOE_R
    k = jnp.arange(n_blk, dtype=jnp.int32)
    g = jnp.minimum(jnp.sum((k[:, None] >= ends[None, :]).astype(jnp.int32), axis=1), N_GROUPS - 1)
    valid = jnp.where(k < ends[-1], jnp.clip(n_g[g] - (k - starts[g]) * MOE_R, 0, MOE_R), 0)
    pad = jnp.stack([jnp.append(starts * MOE_R + n_g, ends[-1] * MOE_R),
                     jnp.append(ends * MOE_R, n_blk * MOE_R)], axis=1)
    return (pos.astype(jnp.int32), g.astype(jnp.int32), valid.astype(jnp.int32),
            pad.astype(jnp.int32))


def _moe(hp, slot, counts, x, w1, w3, w2, layer, mod, final_g, has_ctx, final_norm):
    n = x.shape[0]
    assert n % MOE_R == 0
    pos, blk_grp, blk_valid, pad = _routing_tables(slot.reshape(n), counts, n)
    tok = _invert(pos, pad, n + N_GROUPS * MOE_R)
    ys = _experts(hp, tok, blk_grp, blk_valid, w1, w3, w2, layer)
    return _combine(pos, x, ys, mod, final_g, has_ctx, final_norm)


def _block_diag(w):
    eye = jnp.eye(HEADS, dtype=w.dtype)
    return jnp.einsum('hij,hg->higj', w, eye).reshape(WIDTH, WIDTH)


def _rope_tables():
    rows = SEQ // GRID_W
    row_ids = jnp.repeat(jnp.arange(rows, dtype=F32), GRID_W)
    col_ids = jnp.tile(jnp.arange(GRID_W, dtype=F32), rows)
    n_freq = DA_QD // 4
    freqs = ROPE_THETA ** (-jnp.arange(n_freq, dtype=F32) / n_freq)
    ang = jnp.stack([row_ids[:, None] * freqs, col_ids[:, None] * freqs], axis=1)
    cos = jnp.cos(ang)
    sin = jnp.sin(ang)
    cos_l = jnp.broadcast_to(cos[:, None, None, :, None, :], (SEQ, HEADS, 2, 2, 2, n_freq))
    sin_l = jnp.broadcast_to(sin[:, None, None, :, None, :], (SEQ, HEADS, 2, 2, 2, n_freq))
    sign = jnp.array([-1.0, 1.0], F32)[None, None, None, None, :, None]
    cos_l = cos_l.reshape(SEQ, WIDTH)
    sin_l = (sin_l * sign).reshape(SEQ, WIDTH)
    cos_all = jnp.concatenate([jnp.ones((CTX_LEN, WIDTH), F32), cos_l], axis=0)
    sin_all = jnp.concatenate([jnp.zeros((CTX_LEN, WIDTH), F32), sin_l], axis=0)
    return cos_all, sin_all


def kernel(x, c, ctx, c_ctx, w_ada, b_ada, norm1_g, norm2_g, w_in, w_out, lru_conv_w, lru_conv_b,
           lru_wr, lru_br, lru_wi, lru_bi, lru_lam, da_lam, da_subln_g, sg_norm_g, sg_w, sg_b,
           hg_lb, hg_norm_g, router_w, router_b, moe_w1, moe_w3, moe_w2, final_norm_g):
    nb = x.shape[0]
    assert nb <= CTX_ROW and x.shape[1:] == (SEQ, D_MODEL) and ctx.shape[1:] == (CTX_LEN, D_MODEL)
    xs = jnp.concatenate([ctx, x], axis=1).reshape(nb * SEQ_ALL, D_MODEL)

    cvec = jnp.zeros((ADA_ROWS, D_MODEL), F32).at[:nb].set(c).at[CTX_ROW].set(c_ctx)
    mods = _ada(cvec, w_ada, b_ada).reshape(DEPTH, ADA_ROWS, 6, D_MODEL)

    cos_all, sin_all = _rope_tables()
    lb_cum = jnp.cumsum(jax.nn.softmax(hg_lb.astype(F32), axis=1), axis=1)
    lb_all = lb_cum - lb_cum[:, :1]
    wr_t = router_w.T
    br = router_b.reshape(N_EXPERTS, 1)
    w1_bf16, w3_bf16, w2_bf16 = (w.astype(BF16) for w in (moe_w1, moe_w3, moe_w2))

    for l in range(DEPTH):
        last = l == DEPTH - 1
        out_rows = SEQ if last else SEQ_ALL
        z = _inproj(xs, norm1_g[l], mods[l], w_in[l].astype(BF16))

        ya = _lru(z, lru_conv_w[l], lru_conv_b[l].reshape(1, WIDTH),
                  jax.vmap(_block_diag)(lru_wr[l]).astype(BF16), lru_br[l],
                  jax.vmap(_block_diag)(lru_wi[l]).astype(BF16), lru_bi[l], lru_lam[l], out_rows)

        lam_init = 0.8 - 0.6 * math.exp(-0.3 * l)
        lf = da_lam[l].astype(F32)
        lam = jnp.exp(jnp.sum(lf[0] * lf[1])) - jnp.exp(jnp.sum(lf[2] * lf[3])) + lam_init
        yb = _attn(z, lam.reshape(1, 1), cos_all, sin_all,
                   jnp.tile(da_subln_g[l], HEADS).reshape(1, WIDTH),
                   with_ctx=not last, out_scale=1.0 - lam_init)

        w_cat = jnp.transpose(sg_w[l], (1, 0, 2)).reshape(SG_CHUNK, HEADS * SG_CHUNK).astype(BF16)
        bias2d = jnp.repeat(sg_b[l].T, HEAD_DIM, axis=1)
        ys = _sgu(z, sg_norm_g[l].reshape(1, WIDTH), w_cat, bias2d, out_rows)

        yh = _hgrn(z, lb_all[:, l], hg_norm_g[l].reshape(1, WIDTH), out_rows)

        xs, hp, pos, counts = _outproj(xs, (ya, yb, ys, yh), w_out[l].astype(BF16), mods[l],
                                       norm2_g[l], wr_t, br, has_ctx=not last)
        xs = _moe(hp, pos, counts, xs, w1_bf16, w3_bf16, w2_bf16, l, mods[l], final_norm_g,
                  has_ctx=not last, final_norm=last)

    return xs.reshape(nb, SEQ, D_MODEL)
```

```python
import functools
import math

import jax
import jax.numpy as jnp
from jax import lax
from jax.experimental import pallas as pl
from jax.experimental.pallas import tpu as pltpu

F32 = jnp.float32
BF16 = jnp.bfloat16

D_MODEL = 1024
SEQ = 2048
CTX_LEN = 256
SEQ_ALL = CTX_LEN + SEQ
DEPTH = 2
GRID_W = 64
EPS = 1e-6
LOG2_E = 1.4426950408889634
WIDTH = 256
HEADS = 4
HEAD_DIM = 64
CONV_W = 4
RG_C = 8.0
DA_QD = 32
ROPE_THETA = 10000.0
SG_CHUNK = 128
N_EXPERTS = 16
N_GROUPS = 4
EXP_PER_GROUP = 4
D_EXPERT = 512
IN_COLS = 3072
COL_A_X, COL_A_G = 0, 1
COL_B_Q, COL_B_K, COL_B_V = 2, 3, 4
COL_C_U, COL_C_V = 5, 6
COL_D_Q, COL_D_FF, COL_D_FB, COL_D_I, COL_D_G = 7, 8, 9, 10, 11

ADA_ROWS = 16
CTX_ROW = 8
VMEM_LIMIT = 56 * 1024 * 1024

LRU_BLK = 8
HG_CHUNK = 16
ATT_TQ = 256
DENSE_TM = 768
MOE_R = 512
HP_COLS = D_MODEL + 128
COMBINE_TM = 1152
COMBINE_TM_LATENT = 1024


def _params(*sem):
    return pltpu.CompilerParams(dimension_semantics=sem, vmem_limit_bytes=VMEM_LIMIT)


def _rms(xf, g):
    return xf * lax.rsqrt(jnp.mean(xf * xf, axis=-1, keepdims=True) + EPS) * g


def _sigmoid(x):
    return 1.0 / (1.0 + jnp.exp(-x))


def _silu(x):
    return x * _sigmoid(x)


def _gelu(x):
    return jax.nn.gelu(x)


def _split3(x):
    hi = x.astype(BF16)
    r = x - hi.astype(F32)
    mid = r.astype(BF16)
    lo = (r - mid.astype(F32)).astype(BF16)
    return hi, mid, lo


def _dot_f32_rhs(m_bf16, x):
    return functools.reduce(jnp.add, [jnp.dot(m_bf16, p, preferred_element_type=F32)
                                      for p in _split3(x)])


def _dot_f32_lhs(x, m_bf16):
    return functools.reduce(jnp.add, [jnp.dot(p, m_bf16, preferred_element_type=F32)
                                      for p in _split3(x)])


def _head_ones(dtype):
    r = lax.broadcasted_iota(jnp.int32, (WIDTH, WIDTH), 0) // HEAD_DIM
    c = lax.broadcasted_iota(jnp.int32, (WIDTH, WIDTH), 1) // HEAD_DIM
    return (r == c).astype(dtype)


def _mod_picker(modl_ref, modc_ref, tm, has_ctx):
    ml = modl_ref[0]
    if not has_ctx:
        return lambda r: ml[r:r + 1]
    mc = modc_ref[0]
    row0 = (pl.program_id(0) % (SEQ_ALL // tm)) * tm
    is_ctx = row0 + lax.broadcasted_iota(jnp.int32, (tm, 1), 0) < CTX_LEN
    return lambda r: jnp.where(is_ctx, mc[r:r + 1], ml[r:r + 1])


def _ada_kernel(c_ref, w_ref, b_ref, o_ref):
    s = _silu(c_ref[...]).astype(BF16)
    o_ref[...] = jnp.dot(s, w_ref[...].astype(BF16), preferred_element_type=F32) + b_ref[...]


def _ada(cvec, w_ada, b_ada):
    tn = 1536
    return pl.pallas_call(
        _ada_kernel,
        grid=(DEPTH, 6 * D_MODEL // tn),
        in_specs=[
            pl.BlockSpec((ADA_ROWS, D_MODEL), lambda l, j: (0, 0)),
            pl.BlockSpec((None, D_MODEL, tn), lambda l, j: (l, 0, j)),
            pl.BlockSpec((None, 1, tn), lambda l, j: (l, 0, j)),
        ],
        out_specs=pl.BlockSpec((None, ADA_ROWS, tn), lambda l, j: (l, 0, j)),
        out_shape=jax.ShapeDtypeStruct((DEPTH, ADA_ROWS, 6 * D_MODEL), F32),
        compiler_params=_params("arbitrary", "arbitrary"),
        name="ada",
    )(cvec, w_ada, b_ada.reshape(DEPTH, 1, 6 * D_MODEL))


def _inproj_kernel(x_ref, g_ref, modl_ref, modc_ref, w_ref, z_ref, *, tm):
    pick = _mod_picker(modl_ref, modc_ref, tm, True)
    h = _rms(x_ref[...], g_ref[...]) * (1.0 + pick(1)) + pick(0)
    z_ref[...] = jnp.dot(h.astype(BF16), w_ref[...], preferred_element_type=F32)


def _mod_specs(tiles_per_batch):
    return [
        pl.BlockSpec((1, 6, D_MODEL), lambda i, *_: (i // tiles_per_batch, 0, 0)),
        pl.BlockSpec((1, 6, D_MODEL), lambda i, *_: (CTX_ROW, 0, 0)),
    ]


def _inproj(x, g, mod, w_bf16):
    n = x.shape[0]
    tm = DENSE_TM
    return pl.pallas_call(
        functools.partial(_inproj_kernel, tm=tm),
        grid=(n // tm,),
        in_specs=[
            pl.BlockSpec((tm, D_MODEL), lambda i: (i, 0)),
            pl.BlockSpec((1, D_MODEL), lambda i: (0, 0)),
            *_mod_specs(SEQ_ALL // tm),
            pl.BlockSpec((D_MODEL, IN_COLS), lambda i: (0, 0)),
        ],
        out_specs=pl.BlockSpec((tm, IN_COLS), lambda i: (i, 0)),
        out_shape=jax.ShapeDtypeStruct((n, IN_COLS), F32),
        compiler_params=_params("arbitrary"),
        name="inproj",
    )(x, g.reshape(1, D_MODEL), mod, mod, w_bf16)


def _lru_kernel(x_ref, gate_ref, cw_ref, cb_ref, wr_ref, br_ref, wi_ref, bi_ref, lam_ref,
                y_ref, a_s, b_s, h_s, *, out_rows):
    x = x_ref[...]
    rows = lax.broadcasted_iota(jnp.int32, (SEQ_ALL, 1), 0)
    seg = rows < CTX_LEN
    u = jnp.zeros_like(x)
    for j in range(CONV_W):
        off = j - CONV_W // 2
        xs = x if off == 0 else pltpu.roll(x, (-off) % SEQ_ALL, 0)
        src = rows + off
        ok = (src >= 0) & (src < SEQ_ALL) & ((src < CTX_LEN) == seg)
        u = u + jnp.where(ok, xs, 0.0) * cw_ref[j:j + 1, :]
    u = u + cb_ref[...]
    ub = u.astype(BF16)
    for d in range(2):
        r = _sigmoid(jnp.dot(ub, wr_ref[d], preferred_element_type=F32) + br_ref[d:d + 1, :])
        i = _sigmoid(jnp.dot(ub, wi_ref[d], preferred_element_type=F32) + bi_ref[d:d + 1, :])
        nl = -lam_ref[d:d + 1, :]
        softplus = jnp.maximum(nl, 0.0) + jnp.log(1.0 + jnp.exp(-jnp.abs(nl)))
        log_a = -RG_C * r * softplus
        a = jnp.exp(log_a)
        a_s[d] = a
        b_s[d] = jnp.sqrt(1.0 - a * a) * i * u

    n_blk = SEQ_ALL // LRU_BLK
    n_ctx_blk = CTX_LEN // LRU_BLK
    sub = lax.broadcasted_iota(jnp.int32, (LRU_BLK, 1), 0)

    def block_scan(a, b, reverse):
        s = 1
        while s < LRU_BLK:
            if reverse:
                a_sh = pltpu.roll(a, LRU_BLK - s, 0)
                b_sh = pltpu.roll(b, LRU_BLK - s, 0)
                ok = sub < LRU_BLK - s
            else:
                a_sh = pltpu.roll(a, s, 0)
                b_sh = pltpu.roll(b, s, 0)
                ok = sub >= s
            b = jnp.where(ok, a * b_sh + b, b)
            a = jnp.where(ok, a * a_sh, a)
            s *= 2
        return a, b

    def body(n, carry):
        hf, hb = carry
        rf = pl.multiple_of(n * LRU_BLK, LRU_BLK)
        af, bf = block_scan(a_s[0, pl.ds(rf, LRU_BLK), :], b_s[0, pl.ds(rf, LRU_BLK), :], False)
        h = af * hf + bf
        h_s[0, pl.ds(rf, LRU_BLK), :] = h
        hf = h[LRU_BLK - 1:LRU_BLK, :]
        nb = jnp.where(n < n_ctx_blk, n_ctx_blk - 1 - n, n_blk + n_ctx_blk - 1 - n)
        rb = pl.multiple_of(nb * LRU_BLK, LRU_BLK)
        ab, bb = block_scan(a_s[1, pl.ds(rb, LRU_BLK), :], b_s[1, pl.ds(rb, LRU_BLK), :], True)
        h = ab * hb + bb
        h_s[1, pl.ds(rb, LRU_BLK), :] = h
        hb = h[0:1, :]
        return hf, hb

    zero = jnp.zeros((1, WIDTH), F32)
    lax.fori_loop(0, n_blk, body, (zero, zero))
    first = SEQ_ALL - out_rows
    y_ref[...] = (h_s[0, first:, :] + h_s[1, first:, :]) * _gelu(gate_ref[first:, :])


def _lru(z, cw, cb, wr_bd, br, wi_bd, bi, lam, out_rows):
    nb = z.shape[0] // SEQ_ALL
    full = lambda shape: pl.BlockSpec(shape, lambda b: (0,) * len(shape))
    return pl.pallas_call(
        functools.partial(_lru_kernel, out_rows=out_rows),
        grid=(nb,),
        in_specs=[
            pl.BlockSpec((SEQ_ALL, WIDTH), lambda b: (b, COL_A_X)),
            pl.BlockSpec((SEQ_ALL, WIDTH), lambda b: (b, COL_A_G)),
            full((CONV_W, WIDTH)), full((1, WIDTH)),
            full((2, WIDTH, WIDTH)), full((2, WIDTH)),
            full((2, WIDTH, WIDTH)), full((2, WIDTH)),
            full((2, WIDTH)),
        ],
        out_specs=pl.BlockSpec((out_rows, WIDTH), lambda b: (b, 0)),
        out_shape=jax.ShapeDtypeStruct((nb * out_rows, WIDTH), F32),
        scratch_shapes=[
            pltpu.VMEM((2, SEQ_ALL, WIDTH), F32),
            pltpu.VMEM((2, SEQ_ALL, WIDTH), F32),
            pltpu.VMEM((2, SEQ_ALL, WIDTH), F32),
        ],
        compiler_params=_params("arbitrary"),
        name="rglru",
    )(z, z, cw, cb, wr_bd, br, wi_bd, bi, lam)


def _rope(x, cos, sin_signed):
    lane = lax.broadcasted_iota(jnp.int32, (1, WIDTH), 1)
    first_half = (lane % 16) < 8
    partner = jnp.where(first_half, pltpu.roll(x, WIDTH - 8, 1), pltpu.roll(x, 8, 1))
    return x * cos + partner * sin_signed


def _attn_kernel(lam_ref, q_ref, k_ref, v_ref, cos_ref, sin_ref, g_ref, y_ref, kt_s, v_s,
                 *, with_ctx, out_scale):
    j = pl.program_id(1)

    @pl.when(j == 0)
    def _():
        kr = _rope(k_ref[...], cos_ref[...], sin_ref[...])
        kt_s[...] = kr.T.astype(BF16)
        v = v_ref[...]
        ones = jnp.ones((SEQ_ALL, HEAD_DIM), F32)
        for h in range(HEADS):
            v_s[h] = jnp.concatenate([v[:, h * HEAD_DIM:(h + 1) * HEAD_DIM], ones],
                                     axis=1).astype(BF16)

    lam = lam_ref[0, 0]
    tile = j if with_ctx else j + 1
    row0 = pl.multiple_of(tile * ATT_TQ, ATT_TQ)
    q = _rope(q_ref[...], cos_ref[pl.ds(row0, ATT_TQ), :], sin_ref[pl.ds(row0, ATT_TQ), :])
    q = q * (DA_QD ** -0.5)

    def attend(n_keys):
        scores = []
        for hn in range(2 * HEADS):
            c0 = hn * DA_QD
            scores.append(jnp.dot(q[:, c0:c0 + DA_QD].astype(BF16), kt_s[c0:c0 + DA_QD, 0:n_keys],
                                  preferred_element_type=F32))
        probs = [jnp.exp(s - jnp.max(s, axis=-1, keepdims=True)).astype(BF16) for s in scores]
        outs = []
        for h in range(HEADS):
            branch = []
            for p in probs[2 * h:2 * h + 2]:
                ov = jnp.dot(p, v_s[h, 0:n_keys, :], preferred_element_type=F32)
                branch.append(ov[:, 0:HEAD_DIM] * (1.0 / ov[:, HEAD_DIM:HEAD_DIM + 1]))
            o = branch[0] - lam * branch[1]
            o = o * lax.rsqrt(jnp.mean(o * o, axis=-1, keepdims=True) + EPS)
            outs.append(o)
        y_ref[...] = jnp.concatenate(outs, axis=-1) * g_ref[...] * out_scale

    if with_ctx:
        @pl.when(j == 0)
        def _():
            attend(CTX_LEN)

        @pl.when(j > 0)
        def _():
            attend(SEQ_ALL)
    else:
        attend(SEQ_ALL)


def _attn(z, lam, cos, sin_signed, sub_g4, with_ctx, out_scale):
    nb = z.shape[0] // SEQ_ALL
    tiles = SEQ_ALL // ATT_TQ
    nq = tiles if with_ctx else tiles - 1
    first = 0 if with_ctx else 1
    return pl.pallas_call(
        functools.partial(_attn_kernel, with_ctx=with_ctx, out_scale=out_scale),
        grid=(nb, nq),
        in_specs=[
            pl.BlockSpec(memory_space=pltpu.SMEM),
            pl.BlockSpec((ATT_TQ, WIDTH), lambda b, j: (b * tiles + j + first, COL_B_Q)),
            pl.BlockSpec((SEQ_ALL, WIDTH), lambda b, j: (b, COL_B_K)),
            pl.BlockSpec((SEQ_ALL, WIDTH), lambda b, j: (b, COL_B_V)),
            pl.BlockSpec((SEQ_ALL, WIDTH), lambda b, j: (0, 0)),
            pl.BlockSpec((SEQ_ALL, WIDTH), lambda b, j: (0, 0)),
            pl.BlockSpec((1, WIDTH), lambda b, j: (0, 0)),
        ],
        out_specs=pl.BlockSpec((ATT_TQ, WIDTH), lambda b, j: (b * nq + j, 0)),
        out_shape=jax.ShapeDtypeStruct((nb * nq * ATT_TQ, WIDTH), F32),
        scratch_shapes=[
            pltpu.VMEM((WIDTH, SEQ_ALL), BF16),
            pltpu.VMEM((HEADS, SEQ_ALL, 2 * HEAD_DIM), BF16),
        ],
        compiler_params=_params("arbitrary", "arbitrary"),
        name="diffattn",
    )(lam, z, z, z, cos, sin_signed, sub_g4)


def _sgu_kernel(u_ref, v_ref, g_ref, w_ref, b_ref, y_ref, *, out_rows):
    row_head = lax.broadcasted_iota(jnp.int32, (HEADS * SG_CHUNK, WIDTH), 0) // SG_CHUNK
    col_head = lax.broadcasted_iota(jnp.int32, (HEADS * SG_CHUNK, WIDTH), 1) // HEAD_DIM
    head_mask = row_head == col_head
    w = w_ref[...]
    bias = b_ref[...]
    g = g_ref[...]
    first = SEQ_ALL - out_rows
    for n in range(first // SG_CHUNK, SEQ_ALL // SG_CHUNK):
        rows = slice(n * SG_CHUNK, (n + 1) * SG_CHUNK)
        vn = _rms(_gelu(v_ref[rows, :]), g).astype(BF16)
        stacked = jnp.where(head_mask, jnp.concatenate([vn] * HEADS, axis=0), jnp.zeros((), BF16))
        vm = jnp.dot(w, stacked, preferred_element_type=F32) + bias
        y_ref[n * SG_CHUNK - first:(n + 1) * SG_CHUNK - first, :] = _gelu(u_ref[rows, :]) * vm


def _sgu(z, norm_g, w_cat, bias2d, out_rows):
    nb = z.shape[0] // SEQ_ALL
    return pl.pallas_call(
        functools.partial(_sgu_kernel, out_rows=out_rows),
        grid=(nb,),
        in_specs=[
            pl.BlockSpec((SEQ_ALL, WIDTH), lambda b: (b, COL_C_U)),
            pl.BlockSpec((SEQ_ALL, WIDTH), lambda b: (b, COL_C_V)),
            pl.BlockSpec((1, WIDTH), lambda b: (0, 0)),
            pl.BlockSpec((SG_CHUNK, HEADS * SG_CHUNK), lambda b: (0, 0)),
            pl.BlockSpec((SG_CHUNK, WIDTH), lambda b: (0, 0)),
        ],
        out_specs=pl.BlockSpec((out_rows, WIDTH), lambda b: (b, 0)),
        out_shape=jax.ShapeDtypeStruct((nb * out_rows, WIDTH), F32),
        compiler_params=_params("arbitrary"),
        name="sgu",
    )(z, z, norm_g, w_cat, bias2d)


def _hgrn_kernel(q_ref, ff_ref, fb_ref, i_ref, gate_ref, lb_ref, gn_ref, y_ref,
                 b_s, k_s, o_s, st_s, stb_s, *, out_rows):
    c = HG_CHUNK
    pre = 128
    ones_bf = _head_ones(BF16)
    pr = lax.broadcasted_iota(jnp.int32, (pre, pre), 0)
    pc = lax.broadcasted_iota(jnp.int32, (pre, pre), 1)
    same = (pr // c) == (pc // c)
    tri = ((same & (pc <= pr)).astype(BF16), (same & (pc >= pr)).astype(BF16))

    for d, f_ref in enumerate((ff_ref, fb_ref)):
        lb = lb_ref[d:d + 1, :]
        for n in range(SEQ_ALL // pre):
            rows = slice(n * pre, (n + 1) * pre)
            f = lb + (1.0 - lb) * _sigmoid(f_ref[rows, :])
            k_s[d, rows, :] = 1.0 - f
            b_s[d, rows, :] = _dot_f32_rhs(tri[d], jnp.log(f) * LOG2_E)
    st_s[...] = jnp.zeros_like(st_s)
    stb_s[...] = jnp.zeros_like(stb_s)

    n_chunks = SEQ_ALL // c
    n_ctx = CTX_LEN // c
    half = c // 2
    sub = lax.broadcasted_iota(jnp.int32, (half, 1), 0)
    lane_head = lax.broadcasted_iota(jnp.int32, (1, 128), 1) // HEAD_DIM
    scale = HEAD_DIM ** -0.5

    def masked_out(d, s, tile):
        return (d == 0 and (tile + 1) * half <= s) or (d == 1 and tile * half > s)

    def front(d, chunk, need_out):
        r0 = pl.multiple_of(chunk * c, c)
        q = q_ref[pl.ds(r0, c), :] * scale
        v = i_ref[pl.ds(r0, c), :]
        b = b_s[d, pl.ds(r0, c), :]
        k = k_s[d, pl.ds(r0, c), :]
        tot = b[c - 1:c, :] if d == 0 else b[0:1, :]
        ke = (k * jnp.exp2(tot - b)).astype(BF16)
        upd = lax.dot_general(v.astype(BF16), ke, (((0,), (0,)), ((), ())),
                              preferred_element_type=F32)
        if not need_out:
            return r0, None, None, upd, v, jnp.exp2(tot)
        o = lax.dot_general((q * jnp.exp2(b)).astype(BF16), stb_s[d],
                            (((1,), (1,)), ((), ())), preferred_element_type=F32)
        parts = []
        for s in range(c):
            for tile in range(2):
                lo_row, hi_row = tile * half, (tile + 1) * half
                if masked_out(d, s, tile):
                    parts.append(jnp.zeros((half, WIDTH), F32))
                    continue
                diff = b[lo_row:hi_row, :] - b[s:s + 1, :]
                if d == 0 and lo_row < s:
                    diff = jnp.where(sub + lo_row >= s, diff, -jnp.inf)
                elif d == 1 and hi_row - 1 > s:
                    diff = jnp.where(sub + lo_row <= s, diff, -jnp.inf)
                parts.append(q[lo_row:hi_row, :] * k[s:s + 1, :] * jnp.exp2(diff))
        att = jnp.dot(jnp.concatenate(parts, axis=0).astype(BF16), ones_bf,
                      preferred_element_type=F32)
        return r0, o, att, upd, v, jnp.exp2(tot)

    def back(d, r0, o, att, upd, v, decay):
        if o is not None:
            tiles = [o[0:half, :], o[half:c, :]]
            for s in range(c):
                for tile in range(2):
                    if not masked_out(d, s, tile):
                        r = s * c + tile * half
                        tiles[tile] = tiles[tile] + att[r:r + half, :] * v[s:s + 1, :]
            o_s[d, pl.ds(r0, c), :] = jnp.concatenate(tiles, axis=0)
        for h in range(HEADS):
            r = slice(h * HEAD_DIM, (h + 1) * HEAD_DIM)
            ct = slice((h // 2) * 128, (h // 2 + 1) * 128)
            new = st_s[d, r, ct] * decay[:, ct] + jnp.where(lane_head == h % 2, upd[r, ct], 0.0)
            st_s[d, r, ct] = new
            stb_s[d, r, ct] = new.astype(BF16)

    def run(lo, hi, need_out):
        def body(n, carry):
            fwd = front(0, n, need_out)
            bwd = front(1, jnp.where(n < n_ctx, n_ctx - 1 - n, n_chunks + n_ctx - 1 - n), need_out)
            back(0, *fwd)
            back(1, *bwd)
            return carry

        lax.fori_loop(lo, hi, body, 0, unroll=8)

    if out_rows == SEQ_ALL:
        run(0, n_chunks, True)
    else:
        run(0, n_ctx, False)
        run(n_ctx, n_chunks, True)

    first = SEQ_ALL - out_rows
    o = o_s[0, first:, :] + o_s[1, first:, :]
    ms = _dot_f32_lhs(o * o, ones_bf) * (1.0 / HEAD_DIM)
    y_ref[...] = o * lax.rsqrt(ms + EPS) * gn_ref[...] * _silu(gate_ref[first:, :])


def _hgrn(z, lb, gn, out_rows):
    nb = z.shape[0] // SEQ_ALL
    col = lambda cidx: pl.BlockSpec((SEQ_ALL, WIDTH), lambda b: (b, cidx))
    return pl.pallas_call(
        functools.partial(_hgrn_kernel, out_rows=out_rows),
        grid=(nb,),
        in_specs=[
            col(COL_D_Q), col(COL_D_FF), col(COL_D_FB), col(COL_D_I), col(COL_D_G),
            pl.BlockSpec((2, WIDTH), lambda b: (0, 0)),
            pl.BlockSpec((1, WIDTH), lambda b: (0, 0)),
        ],
        out_specs=pl.BlockSpec((out_rows, WIDTH), lambda b: (b, 0)),
        out_shape=jax.ShapeDtypeStruct((nb * out_rows, WIDTH), F32),
        scratch_shapes=[
            pltpu.VMEM((2, SEQ_ALL, WIDTH), F32),
            pltpu.VMEM((2, SEQ_ALL, WIDTH), F32),
            pltpu.VMEM((2, SEQ_ALL, WIDTH), F32),
            pltpu.VMEM((2, WIDTH, WIDTH), F32),
            pltpu.VMEM((2, WIDTH, WIDTH), BF16),
        ],
        compiler_params=_params("arbitrary"),
        name="hgrn2",
    )(z, z, z, z, z, lb, gn)


def _route(logits):
    rows = [logits[e:e + 1, :] for e in range(N_EXPERTS)]
    m = functools.reduce(jnp.maximum, rows)
    ex = [jnp.exp(r - m) for r in rows]
    inv = 1.0 / functools.reduce(jnp.add, ex)
    sc = [e * inv for e in ex]
    g_score = []
    for g in range(N_GROUPS):
        grp = sc[g * EXP_PER_GROUP:(g + 1) * EXP_PER_GROUP]
        pairs = [grp[a] + grp[b] for a in range(EXP_PER_GROUP) for b in range(a + 1, EXP_PER_GROUP)]
        g_score.append(functools.reduce(jnp.maximum, pairs))
    gate4 = [jnp.zeros_like(sc[0]) for _ in range(EXP_PER_GROUP)]
    onehot = []
    for g in range(N_GROUPS):
        g_ok = None
        for o in range(N_GROUPS):
            if o == g:
                continue
            t = (g_score[g] > g_score[o]) if o < g else (g_score[g] >= g_score[o])
            g_ok = t if g_ok is None else (g_ok & t)
        grp = sc[g * EXP_PER_GROUP:(g + 1) * EXP_PER_GROUP]
        picked = []
        for a in range(EXP_PER_GROUP):
            beaten = jnp.zeros_like(grp[a])
            for o in range(EXP_PER_GROUP):
                if o == a:
                    continue
                t = (grp[o] >= grp[a]) if o < a else (grp[o] > grp[a])
                beaten = beaten + jnp.where(t, 1.0, 0.0)
            picked.append(jnp.where((beaten < 1.5) & g_ok, grp[a], 0.0))
        denom = functools.reduce(jnp.add, picked)
        denom = jnp.where(g_ok, denom, 1.0)
        gate4 = [acc + p / denom for acc, p in zip(gate4, picked)]
        onehot.append(jnp.where(g_ok, 1.0, 0.0))
    return gate4, onehot


def _outproj_kernel(x_ref, ya_ref, yb_ref, ys_ref, yh_ref, w_ref, modl_ref, modc_ref, g_ref,
                    wr_ref, br_ref, xo_ref, hp_ref, pos_ref, cnt_ref, cnt_s, before_s,
                    *, tm, has_ctx, cap):
    @pl.when(pl.program_id(0) == 0)
    def _():
        cnt_s[...] = jnp.zeros_like(cnt_s)
        before_s[...] = (lax.broadcasted_iota(jnp.int32, (tm, tm), 0)
                         <= lax.broadcasted_iota(jnp.int32, (tm, tm), 1)).astype(BF16)

    pick = _mod_picker(modl_ref, modc_ref, tm, has_ctx)
    acc = jnp.zeros((tm, D_MODEL), F32)
    for kblk, y_ref in enumerate((ya_ref, yb_ref, ys_ref, yh_ref)):
        acc = acc + jnp.dot(y_ref[...].astype(BF16), w_ref[kblk * WIDTH:(kblk + 1) * WIDTH, :],
                            preferred_element_type=F32)
    x = x_ref[...] + pick(2) * acc
    xo_ref[...] = x
    h = _rms(x, g_ref[...]) * (1.0 + pick(4)) + pick(3)
    nt = (((1,), (1,)), ((), ()))
    wr = wr_ref[...]
    wr_hi = wr.astype(BF16)
    wr_lo = (wr - wr_hi.astype(F32)).astype(BF16)
    h_hi = h.astype(BF16)
    h_lo = (h - h_hi.astype(F32)).astype(BF16)
    logits = (lax.dot_general(wr_hi, h_hi, nt, preferred_element_type=F32)
              + lax.dot_general(wr_hi, h_lo, nt, preferred_element_type=F32)
              + lax.dot_general(wr_lo, h_hi, nt, preferred_element_type=F32)) + br_ref[...]
    gate4, onehot = _route(logits)
    hp_ref[:, 0:D_MODEL] = h
    gate_rows = jnp.concatenate(gate4 + [jnp.zeros((128 - EXP_PER_GROUP, tm), F32)], axis=0)
    hp_ref[:, D_MODEL:] = gate_rows.T
    sel = jnp.concatenate(onehot + [jnp.zeros((8 - N_GROUPS, tm), F32)], axis=0)
    seen = jnp.dot(sel.astype(BF16), before_s[...], preferred_element_type=F32)
    carried = cnt_s[:, 0:1]
    base = lax.broadcasted_iota(jnp.int32, (8, 1), 0).astype(F32) * float(cap)
    slot = jnp.sum(sel * (base + carried + seen - 1.0), axis=0, keepdims=True)
    pos_ref[0] = slot.astype(jnp.int32)
    cnt_s[...] = cnt_s[...] + jnp.sum(sel, axis=1, keepdims=True)
    cnt_ref[...] = cnt_s[...].astype(jnp.int32)


def _outproj(x, ys, w_bf16, mod, g2, wr_t, br, has_ctx):
    n = ys[0].shape[0]
    if has_ctx:
        tm = DENSE_TM
        tpb = SEQ_ALL // tm
        x_spec = pl.BlockSpec((tm, D_MODEL), lambda i: (i, 0))
    else:
        tm = CTX_LEN
        tpb = SEQ // tm
        x_spec = pl.BlockSpec((tm, D_MODEL),
                              lambda i: ((i // tpb) * (SEQ_ALL // tm) + 1 + i % tpb, 0))
    tile = lambda w: pl.BlockSpec((tm, w), lambda i: (i, 0))
    return pl.pallas_call(
        functools.partial(_outproj_kernel, tm=tm, has_ctx=has_ctx, cap=n),
        grid=(n // tm,),
        in_specs=[
            x_spec, tile(WIDTH), tile(WIDTH), tile(WIDTH), tile(WIDTH),
            pl.BlockSpec((D_MODEL, D_MODEL), lambda i: (0, 0)),
            *_mod_specs(tpb),
            pl.BlockSpec((1, D_MODEL), lambda i: (0, 0)),
            pl.BlockSpec((N_EXPERTS, D_MODEL), lambda i: (0, 0)),
            pl.BlockSpec((N_EXPERTS, 1), lambda i: (0, 0)),
        ],
        out_specs=[
            tile(D_MODEL), tile(HP_COLS),
            pl.BlockSpec((1, 1, tm), lambda i: (i, 0, 0)),
            pl.BlockSpec((8, 128), lambda i: (0, 0)),
        ],
        out_shape=[
            jax.ShapeDtypeStruct((n, D_MODEL), F32),
            jax.ShapeDtypeStruct((n, HP_COLS), F32),
            jax.ShapeDtypeStruct((n // tm, 1, tm), jnp.int32),
            jax.ShapeDtypeStruct((8, 128), jnp.int32),
        ],
        scratch_shapes=[pltpu.VMEM((8, 128), F32), pltpu.VMEM((tm, tm), BF16)],
        compiler_params=_params("arbitrary"),
        name="outproj_router",
    )(x, *ys, w_bf16, mod, mod, g2.reshape(1, D_MODEL), wr_t, br)


def _row_copies_wait(dst_rows, sem):
    pltpu.make_async_copy(dst_rows, dst_rows, sem).wait()


def _invert_kernel(pos_ref, pad_ref, tok_ref):
    def clear(p, carry):
        tok_ref[p] = 0
        return carry

    def put(t, carry):
        tok_ref[pos_ref[t]] = t
        return carry

    for r in range(N_GROUPS + 1):
        lax.fori_loop(pad_ref[r, 0], pad_ref[r, 1], clear, 0)
    lax.fori_loop(0, pos_ref.shape[0], put, 0, unroll=16)


def _invert(pos, pad, n_slots):
    return pl.pallas_call(
        _invert_kernel,
        in_specs=[pl.BlockSpec(memory_space=pltpu.SMEM), pl.BlockSpec(memory_space=pltpu.SMEM)],
        out_specs=pl.BlockSpec(memory_space=pltpu.SMEM),
        out_shape=jax.ShapeDtypeStruct((n_slots,), jnp.int32),
        name="moe_invert",
    )(pos, pad)


def _experts_kernel(grp_ref, valid_ref, tok_ref, hp_hbm, w1_ref, w3_ref, w2_ref, ys_ref, buf, sems):
    k = pl.program_id(0)
    n_blk = pl.num_programs(0)
    n_valid = valid_ref[k]
    slot = k % 2

    def gather(blk, buf_slot):
        def body(i, carry):
            base = blk * MOE_R + i * 8
            for j in range(8):
                pltpu.make_async_copy(hp_hbm.at[pl.ds(tok_ref[base + j], 1)],
                                      buf.at[buf_slot, i, pl.ds(j, 1)], sems.at[buf_slot]).start()
            return carry

        lax.fori_loop(0, MOE_R // 8, body, 0)

    @pl.when(k == 0)
    def _():
        gather(0, 0)

    nxt = jnp.minimum(k + 1, n_blk - 1)

    @pl.when((k + 1 < n_blk) & (valid_ref[nxt] > 0))
    def _():
        gather(k + 1, 1 - slot)

    @pl.when(n_valid > 0)
    def _():
        _row_copies_wait(buf.at[slot], sems.at[slot])
        w = buf[slot].reshape(MOE_R, HP_COLS)
        ok = lax.broadcasted_iota(jnp.int32, (MOE_R, 1), 0) < n_valid
        h = jnp.where(ok, w[:, 0:D_MODEL], 0.0).astype(BF16)
        gates = jnp.where(ok, w[:, D_MODEL:], 0.0)
        acc = jnp.zeros((MOE_R, D_MODEL), F32)
        for e in range(EXP_PER_GROUP):
            a = jnp.dot(h, w1_ref[e], preferred_element_type=F32)
            b = jnp.dot(h, w3_ref[e], preferred_element_type=F32)
            he = (_silu(a) * b * gates[:, e:e + 1]).astype(BF16)
            acc = acc + jnp.dot(he, w2_ref[e], preferred_element_type=F32)
        ys_ref[...] = acc

    @pl.when(n_valid == 0)
    def _():
        ys_ref[...] = jnp.zeros_like(ys_ref)


def _experts(hp, tok, blk_grp, blk_valid, w1, w3, w2, layer):
    n_slots = tok.shape[0]
    up = pl.BlockSpec((None, EXP_PER_GROUP, D_MODEL, D_EXPERT),
                      lambda k, grp, valid, tok: (layer, grp[k], 0, 0))
    down = pl.BlockSpec((None, EXP_PER_GROUP, D_EXPERT, D_MODEL),
                        lambda k, grp, valid, tok: (layer, grp[k], 0, 0))
    grid_spec = pltpu.PrefetchScalarGridSpec(
        num_scalar_prefetch=3,
        grid=(n_slots // MOE_R,),
        in_specs=[pl.BlockSpec(memory_space=pl.ANY), up, up, down],
        out_specs=pl.BlockSpec((MOE_R, D_MODEL), lambda k, grp, valid, tok: (k, 0)),
        scratch_shapes=[pltpu.VMEM((2, MOE_R // 8, 8, HP_COLS), F32),
                        pltpu.SemaphoreType.DMA((2,))],
    )
    return pl.pallas_call(
        _experts_kernel,
        grid_spec=grid_spec,
        out_shape=jax.ShapeDtypeStruct((n_slots, D_MODEL), F32),
        compiler_params=_params("arbitrary"),
        name="moe_experts",
    )(blk_grp, blk_valid, tok, hp, w1, w3, w2)


def _combine_kernel(pos_ref, x_ref, modl_ref, modc_ref, fg_ref, ys_hbm, o_ref, buf, sem,
                    *, tm, has_ctx, final_norm):
    def body(i, carry):
        for j in range(8):
            pltpu.make_async_copy(ys_hbm.at[pl.ds(pos_ref[0, 0, i * 8 + j], 1)],
                                  buf.at[i, pl.ds(j, 1)], sem).start()
        return carry

    lax.fori_loop(0, tm // 8, body, 0)
    _row_copies_wait(buf, sem)
    pick = _mod_picker(modl_ref, modc_ref, tm, has_ctx)
    x = x_ref[...] + pick(5) * buf[...].reshape(tm, D_MODEL)
    o_ref[...] = _rms(x, fg_ref[...]) if final_norm else x


def _combine(pos, x, ys, mod, final_g, has_ctx, final_norm):
    n = x.shape[0]
    tm = COMBINE_TM if has_ctx else COMBINE_TM_LATENT
    tpb = (SEQ_ALL if has_ctx else SEQ) // tm
    return pl.pallas_call(
        functools.partial(_combine_kernel, tm=tm, has_ctx=has_ctx, final_norm=final_norm),
        grid=(n // tm,),
        in_specs=[
            pl.BlockSpec((1, 1, tm), lambda i: (i, 0, 0), memory_space=pltpu.SMEM),
            pl.BlockSpec((tm, D_MODEL), lambda i: (i, 0)),
            *_mod_specs(tpb),
            pl.BlockSpec((1, D_MODEL), lambda i: (0, 0)),
            pl.BlockSpec(memory_space=pl.ANY),
        ],
        out_specs=pl.BlockSpec((tm, D_MODEL), lambda i: (i, 0)),
        out_shape=jax.ShapeDtypeStruct((n, D_MODEL), F32),
        scratch_shapes=[pltpu.VMEM((tm // 8, 8, D_MODEL), F32), pltpu.SemaphoreType.DMA],
        compiler_params=_params("arbitrary"),
        name="moe_combine",
    )(pos.reshape(n // tm, 1, tm), x, mod, mod, final_g.reshape(1, D_MODEL), ys)


def _routing_tables(slot, counts, n):
    n_blk = n // MOE_R + N_GROUPS
    n_g = counts[:N_GROUPS, 0]
    nb = (n_g + MOE_R - 1) // MOE_R
    ends = jnp.cumsum(nb)
    starts = ends - nb
    g_tok = slot // n
    pos = slot - g_tok * n + starts[g_tok] * MOE_R
    k = jnp.arange(n_blk, dtype=jnp.int32)
    g = jnp.minimum(jnp.sum((k[:, None] >= ends[None, :]).astype(jnp.int32), axis=1), N_GROUPS - 1)
    valid = jnp.where(k < ends[-1], jnp.clip(n_g[g] - (k - starts[g]) * MOE_R, 0, MOE_R), 0)
    pad = jnp.stack([jnp.append(starts * MOE_R + n_g, ends[-1] * MOE_R),
                     jnp.append(ends * MOE_R, n_blk * MOE_R)], axis=1)
    return (pos.astype(jnp.int32), g.astype(jnp.int32), valid.astype(jnp.int32),
            pad.astype(jnp.int32))


def _moe(hp, slot, counts, x, w1, w3, w2, layer, mod, final_g, has_ctx, final_norm):
    n = x.shape[0]
    assert n % MOE_R == 0
    pos, blk_grp, blk_valid, pad = _routing_tables(slot.reshape(n), counts, n)
    tok = _invert(pos, pad, n + N_GROUPS * MOE_R)
    ys = _experts(hp, tok, blk_grp, blk_valid, w1, w3, w2, layer)
    return _combine(pos, x, ys, mod, final_g, has_ctx, final_norm)


def _block_diag(w):
    eye = jnp.eye(HEADS, dtype=w.dtype)
    return jnp.einsum('hij,hg->higj', w, eye).reshape(WIDTH, WIDTH)


def _rope_tables():
    rows = SEQ // GRID_W
    row_ids = jnp.repeat(jnp.arange(rows, dtype=F32), GRID_W)
    col_ids = jnp.tile(jnp.arange(GRID_W, dtype=F32), rows)
    n_freq = DA_QD // 4
    freqs = ROPE_THETA ** (-jnp.arange(n_freq, dtype=F32) / n_freq)
    ang = jnp.stack([row_ids[:, None] * freqs, col_ids[:, None] * freqs], axis=1)
    cos = jnp.cos(ang)
    sin = jnp.sin(ang)
    cos_l = jnp.broadcast_to(cos[:, None, None, :, None, :], (SEQ, HEADS, 2, 2, 2, n_freq))
    sin_l = jnp.broadcast_to(sin[:, None, None, :, None, :], (SEQ, HEADS, 2, 2, 2, n_freq))
    sign = jnp.array([-1.0, 1.0], F32)[None, None, None, None, :, None]
    cos_l = cos_l.reshape(SEQ, WIDTH)
    sin_l = (sin_l * sign).reshape(SEQ, WIDTH)
    cos_all = jnp.concatenate([jnp.ones((CTX_LEN, WIDTH), F32), cos_l], axis=0)
    sin_all = jnp.concatenate([jnp.zeros((CTX_LEN, WIDTH), F32), sin_l], axis=0)
    return cos_all, sin_all


def kernel(x, c, ctx, c_ctx, w_ada, b_ada, norm1_g, norm2_g, w_in, w_out, lru_conv_w, lru_conv_b,
           lru_wr, lru_br, lru_wi, lru_bi, lru_lam, da_lam, da_subln_g, sg_norm_g, sg_w, sg_b,
           hg_lb, hg_norm_g, router_w, router_b, moe_w1, moe_w3, moe_w2, final_norm_g):
    nb = x.shape[0]
    assert nb <= CTX_ROW and x.shape[1:] == (SEQ, D_MODEL) and ctx.shape[1:] == (CTX_LEN, D_MODEL)
    xs = jnp.concatenate([ctx, x], axis=1).reshape(nb * SEQ_ALL, D_MODEL)

    cvec = jnp.zeros((ADA_ROWS, D_MODEL), F32).at[:nb].set(c).at[CTX_ROW].set(c_ctx)
    mods = _ada(cvec, w_ada, b_ada).reshape(DEPTH, ADA_ROWS, 6, D_MODEL)

    cos_all, sin_all = _rope_tables()
    lb_cum = jnp.cumsum(jax.nn.softmax(hg_lb.astype(F32), axis=1), axis=1)
    lb_all = lb_cum - lb_cum[:, :1]
    wr_t = router_w.T
    br = router_b.reshape(N_EXPERTS, 1)
    w1_bf16, w3_bf16, w2_bf16 = (w.astype(BF16) for w in (moe_w1, moe_w3, moe_w2))

    for l in range(DEPTH):
        last = l == DEPTH - 1
        out_rows = SEQ if last else SEQ_ALL
        z = _inproj(xs, norm1_g[l], mods[l], w_in[l].astype(BF16))

        ya = _lru(z, lru_conv_w[l], lru_conv_b[l].reshape(1, WIDTH),
                  jax.vmap(_block_diag)(lru_wr[l]).astype(BF16), lru_br[l],
                  jax.vmap(_block_diag)(lru_wi[l]).astype(BF16), lru_bi[l], lru_lam[l], out_rows)

        lam_init = 0.8 - 0.6 * math.exp(-0.3 * l)
        lf = da_lam[l].astype(F32)
        lam = jnp.exp(jnp.sum(lf[0] * lf[1])) - jnp.exp(jnp.sum(lf[2] * lf[3])) + lam_init
        yb = _attn(z, lam.reshape(1, 1), cos_all, sin_all,
                   jnp.tile(da_subln_g[l], HEADS).reshape(1, WIDTH),
                   with_ctx=not last, out_scale=1.0 - lam_init)

        w_cat = jnp.transpose(sg_w[l], (1, 0, 2)).reshape(SG_CHUNK, HEADS * SG_CHUNK).astype(BF16)
        bias2d = jnp.repeat(sg_b[l].T, HEAD_DIM, axis=1)
        ys = _sgu(z, sg_norm_g[l].reshape(1, WIDTH), w_cat, bias2d, out_rows)

        yh = _hgrn(z, lb_all[:, l], hg_norm_g[l].reshape(1, WIDTH), out_rows)

        xs, hp, pos, counts = _outproj(xs, (ya, yb, ys, yh), w_out[l].astype(BF16), mods[l],
                                       norm2_g[l], wr_t, br, has_ctx=not last)
        xs = _moe(hp, pos, counts, xs, w1_bf16, w3_bf16, w2_bf16, l, mods[l], final_norm_g,
                  has_ctx=not last, final_norm=last)

    return xs.reshape(nb, SEQ, D_MODEL)
```
